```python
import math
import jax, jax.numpy as jnp
from jax import lax
import numpy as np

D_MODEL = 2048
BATCH = 4
SEQ = 4096
DEPTH = 1

EPS = 1e-6
ROPE_THETA = 10000.0
NEG = -1e30
Q_BLOCK = 128

MIX_WIDTH = D_MODEL
MLA_NOPE = 128
MLA_ROPE = 64
MLA_V = 128
MLA_WIDTH = MIX_WIDTH // 2
MLA_HEADS = MLA_WIDTH // MLA_V
MLA_Q_RANK = 768
MLA_KV_RANK = 512
MLA_QK = MLA_NOPE + MLA_ROPE
DIL_WIDTH = MIX_WIDTH - MLA_WIDTH
DIL_HEAD_DIM = 128
DIL_HEADS = DIL_WIDTH // DIL_HEAD_DIM
DIL_PATTERNS = ((128, 1), (512, 4), (2048, 16))

IN_SPLITS = (MLA_Q_RANK, MLA_KV_RANK, MLA_ROPE, MLA_WIDTH, 3 * DIL_WIDTH, DIL_WIDTH)
IN_COLS = MLA_Q_RANK + MLA_KV_RANK + MLA_ROPE + MLA_WIDTH + 3 * DIL_WIDTH + DIL_WIDTH

kernel_name = "hybrid_mla_dilated_parallel_heads"


def rmsnorm(x, g):
    xf = x.astype(jnp.float32)
    y = xf * lax.rsqrt(jnp.mean(xf * xf, axis=-1, keepdims=True) + EPS)
    return (y * g.astype(jnp.float32)).astype(x.dtype)


def rope(x, pos):
    d = x.shape[-1]
    inv = ROPE_THETA ** (-jnp.arange(0, d, 2, dtype=jnp.float32) / d)
    ang = pos.astype(jnp.float32)[:, None] * inv[None, :]
    cos = jnp.cos(ang)[:, None, :]
    sin = jnp.sin(ang)[:, None, :]
    xf = x.astype(jnp.float32)
    x1, x2 = xf[..., : d // 2], xf[..., d // 2:]
    out = jnp.concatenate([x1 * cos - x2 * sin, x2 * cos + x1 * sin], axis=-1)
    return out.astype(x.dtype)


def split_cols(proj):
    parts, off = [], 0
    for n in IN_SPLITS:
        parts.append(proj[..., off:off + n])
        off += n
    return parts


def causal_block_attention(q, k, v, scale):
    B, S, H, Dk = q.shape
    Dv = v.shape[-1]
    nb = S // Q_BLOCK
    qb = q.astype(jnp.float32).reshape(B, nb, Q_BLOCK, H, Dk).transpose(1, 0, 2, 3, 4)
    kf = k.astype(jnp.float32)
    vf = v.astype(jnp.float32)
    kpos = jnp.arange(S)

    def one_block(args):
        i, qi = args
        s = jnp.einsum('bqhd,bkhd->bhqk', qi, kf) * scale
        qpos = i * Q_BLOCK + jnp.arange(Q_BLOCK)
        s = jnp.where(kpos[None, :] <= qpos[:, None], s, NEG)
        p = jax.nn.softmax(s, axis=-1)
        return jnp.einsum('bhqk,bkhd->bqhd', p, vf)

    out = lax.map(one_block, (jnp.arange(nb), qb))
    return out.transpose(1, 0, 2, 3, 4).reshape(B, S, H, Dv)


def dilated_window_attention(q, k, v, window, dilation):
    B, S, H, D = q.shape
    n_back = window // dilation
    L = S // dilation
    nb = -(-L // Q_BLOCK)
    Lp = nb * Q_BLOCK

    def sub(t):
        t = t.astype(jnp.float32).reshape(B, L, dilation, H, D).transpose(0, 2, 1, 3, 4)
        return jnp.pad(t, ((0, 0), (0, 0), (0, Lp - L), (0, 0), (0, 0)))

    def band(t):
        tp = jnp.pad(t, ((0, 0), (0, 0), (Q_BLOCK, 0), (0, 0), (0, 0)))
        prev = tp[:, :, :Lp].reshape(B, dilation, nb, Q_BLOCK, H, D)
        cur = t.reshape(B, dilation, nb, Q_BLOCK, H, D)
        return jnp.concatenate([prev, cur], axis=3)

    qb = sub(q).reshape(B, dilation, nb, Q_BLOCK, H, D)
    kb = band(sub(k))
    vb = band(sub(v))
    s = jnp.einsum('brnqhd,brnkhd->brnhqk', qb, kb) * (1.0 / math.sqrt(D))
    qi = jnp.arange(Q_BLOCK)[:, None]
    kj = jnp.arange(2 * Q_BLOCK)[None, :]
    dist = qi + Q_BLOCK - kj
    key_idx = jnp.arange(nb)[:, None, None] * Q_BLOCK + kj[None] - Q_BLOCK
    mask = (dist[None] >= 0) & (dist[None] <= n_back) & (key_idx >= 0)
    s = jnp.where(mask[:, None], s, NEG)
    m = jnp.max(s, axis=-1, keepdims=True)
    e = jnp.exp(s - m)
    den = jnp.sum(e, axis=-1, keepdims=True)
    o = jnp.einsum('brnhqk,brnkhd->brnqhd', e / den, vb)
    lse = (m + jnp.log(den))[..., 0]
    o = o.reshape(B, dilation, Lp, H, D)[:, :, :L].transpose(0, 2, 1, 3, 4).reshape(B, S, H, D)
    lse = lse.transpose(0, 1, 2, 4, 3).reshape(B, dilation, Lp, H)[:, :, :L]
    lse = lse.transpose(0, 2, 1, 3).reshape(B, S, H)
    return o, lse


def setup_inputs(seed: int = 0) -> dict:
    key = jax.random.key(seed)
    ks = jax.random.split(key, 16)

    def gain(k, n):
        return 1.0 + 0.02 * jax.random.normal(k, (DEPTH, n), jnp.float32)

    def w(k, fan_in, fan_out):
        return jax.random.normal(k, (DEPTH, fan_in, fan_out), jnp.float32) * fan_in ** -0.5

    return {
        "x": jax.random.normal(ks[0], (BATCH, SEQ, D_MODEL), jnp.float32),
        "norm_gain": gain(ks[1], D_MODEL),
        "w_in": w(ks[2], D_MODEL, IN_COLS),
        "q_a_norm_gain": gain(ks[3], MLA_Q_RANK),
        "kv_a_norm_gain": gain(ks[4], MLA_KV_RANK),
        "w_uq": w(ks[5], MLA_Q_RANK, MLA_HEADS * MLA_QK),
        "w_ukv": w(ks[6], MLA_KV_RANK, MLA_HEADS * (MLA_NOPE + MLA_V)),
        "mla_q_norm_gain": gain(ks[7], MLA_QK),
        "mla_k_norm_gain": gain(ks[8], MLA_QK),
        "dil_q_norm_gain": gain(ks[9], DIL_HEAD_DIM),
        "dil_k_norm_gain": gain(ks[10], DIL_HEAD_DIM),
        "mla_out_norm_gain": gain(ks[11], MLA_WIDTH),
        "dil_out_norm_gain": gain(ks[12], DIL_WIDTH),
        "w_out": w(ks[13], MIX_WIDTH, D_MODEL),
    }


def reference(x, norm_gain, w_in, q_a_norm_gain, kv_a_norm_gain, w_uq, w_ukv,
              mla_q_norm_gain, mla_k_norm_gain, dil_q_norm_gain, dil_k_norm_gain,
              mla_out_norm_gain, dil_out_norm_gain, w_out):
    B, S, _ = x.shape
    pos = jnp.arange(S, dtype=jnp.int32)
    h = x
    for l in range(DEPTH):
        hn = rmsnorm(h, norm_gain[l])
        proj = jnp.einsum('bsd,de->bse', hn, w_in[l])
        c_q, c_kv, k_r, g_a, qkv_b, g_b = split_cols(proj)

        gq, gk = mla_q_norm_gain[l], mla_k_norm_gain[l]
        q_a = jnp.einsum('bsr,re->bse', rmsnorm(c_q, q_a_norm_gain[l]), w_uq[l])
        q_a = q_a.reshape(B, S, MLA_HEADS, MLA_QK)
        kv_a = jnp.einsum('bsr,re->bse', rmsnorm(c_kv, kv_a_norm_gain[l]), w_ukv[l])
        kv_a = kv_a.reshape(B, S, MLA_HEADS, MLA_NOPE + MLA_V)
        q_nope = rmsnorm(q_a[..., :MLA_NOPE], gq[:MLA_NOPE])
        q_rope = rope(rmsnorm(q_a[..., MLA_NOPE:], gq[MLA_NOPE:]), pos)
        k_nope = rmsnorm(kv_a[..., :MLA_NOPE], gk[:MLA_NOPE])
        v_a = kv_a[..., MLA_NOPE:]
        k_rope = rope(rmsnorm(k_r, gk[MLA_NOPE:])[:, :, None, :], pos)
        q_full = jnp.concatenate([q_nope, q_rope], axis=-1)
        k_full = jnp.concatenate(
            [k_nope, jnp.broadcast_to(k_rope, (B, S, MLA_HEADS, MLA_ROPE))], axis=-1)
        o_a = causal_block_attention(q_full, k_full, v_a, 1.0 / math.sqrt(MLA_QK))
        o_a = o_a.reshape(B, S, MLA_WIDTH).astype(h.dtype)
        y_a = rmsnorm(o_a, mla_out_norm_gain[l]) * jax.nn.silu(g_a)

        q_b, k_b, v_b = jnp.split(qkv_b, 3, axis=-1)
        q_b = rope(rmsnorm(q_b.reshape(B, S, DIL_HEADS, DIL_HEAD_DIM), dil_q_norm_gain[l]), pos)
        k_b = rope(rmsnorm(k_b.reshape(B, S, DIL_HEADS, DIL_HEAD_DIM), dil_k_norm_gain[l]), pos)
        v_b = v_b.reshape(B, S, DIL_HEADS, DIL_HEAD_DIM)
        outs, lses = [], []
        for window, dilation in DIL_PATTERNS:
            o_p, lse_p = dilated_window_attention(q_b, k_b, v_b, window, dilation)
            outs.append(o_p)
            lses.append(lse_p)
        wts = jax.nn.softmax(jnp.stack(lses, axis=0), axis=0)
        o_b = jnp.sum(wts[..., None] * jnp.stack(outs, axis=0), axis=0)
        o_b = o_b.reshape(B, S, DIL_WIDTH).astype(h.dtype)
        y_b = rmsnorm(o_b, dil_out_norm_gain[l]) * jax.nn.silu(g_b)

        y = jnp.concatenate([y_a, y_b], axis=-1)
        h = h + jnp.einsum('bse,ed->bsd', y, w_out[l])
    return h
```

```python
import functools
import math

import jax
import jax.numpy as jnp
from jax import lax
from jax.experimental import pallas as pl
from jax.experimental.pallas import tpu as pltpu

EPS = 1e-6
ROPE_THETA = 10000.0
NEG = -1e30

D_MODEL = 2048
N_HEADS = 8
HEAD = 128
ROPE = 64
Q_RANK = 768
KV_RANK = 512
WIDTH = N_HEADS * HEAD
MLA_QK = HEAD + ROPE
QBLK = 128
N_BACK = 128
DILATIONS = (1, 4, 16)

PROJ_COLS = 6656
LAT_COLS = 1536
VMEM_LIMIT = 56 * 1024 * 1024


def _rms_scale(x, n):
    return lax.rsqrt(jnp.sum(x * x, axis=-1, keepdims=True) * (1.0 / n) + EPS)


def _in_proj_kernel(x_ref, g_ref, w_ref, o_ref, hn_ref, *, row_chunk):
    @pl.when(pl.program_id(1) == 0)
    def _():
        def body(c, carry):
            r0 = pl.multiple_of(c * row_chunk, row_chunk)
            x = x_ref[pl.ds(r0, row_chunk), :]
            hn = x * _rms_scale(x, D_MODEL) * g_ref[...]
            hn_ref[pl.ds(r0, row_chunk), :] = hn.astype(hn_ref.dtype)
            return carry
        lax.fori_loop(0, x_ref.shape[0] // row_chunk, body, 0)

    o_ref[...] = jnp.dot(hn_ref[...], w_ref[...],
                         preferred_element_type=jnp.float32).astype(o_ref.dtype)


def _in_proj(x2, gain, w, *, tm=1024, tn=512):
    m = x2.shape[0]
    return pl.pallas_call(
        functools.partial(_in_proj_kernel, row_chunk=64),
        grid=(m // tm, PROJ_COLS // tn),
        in_specs=[
            pl.BlockSpec((tm, D_MODEL), lambda i, j: (i, 0)),
            pl.BlockSpec((1, D_MODEL), lambda i, j: (0, 0)),
            pl.BlockSpec((D_MODEL, tn), lambda i, j: (0, j)),
        ],
        out_specs=pl.BlockSpec((tm, tn), lambda i, j: (i, j)),
        out_shape=jax.ShapeDtypeStruct((m, PROJ_COLS), jnp.bfloat16),
        scratch_shapes=[pltpu.VMEM((tm, D_MODEL), jnp.bfloat16)],
        compiler_params=pltpu.CompilerParams(
            dimension_semantics=("parallel", "arbitrary"),
            vmem_limit_bytes=VMEM_LIMIT),
        name="in_proj",
    )(x2, gain, w)


def _rope_pad(x, cos, sin):
    return x * cos + pltpu.roll(x, 64, 1) * sin


def _mla_prep_kernel(lat_ref, wq_ref, wkv_ref, gqa_ref, gkva_ref, gq_ref, gk_ref,
                     cos_ref, sin_ref, q_ref, k_ref, v_ref):
    cos = cos_ref[...]
    sin = sin_ref[...]
    gq = gq_ref[...]
    gk = gk_ref[...]

    c_kv = lat_ref[:, 0:KV_RANK].astype(jnp.float32)
    cn = (c_kv * _rms_scale(c_kv, KV_RANK) * gkva_ref[...]).astype(jnp.bfloat16)
    kv = jnp.dot(cn, wkv_ref[...], preferred_element_type=jnp.float32)
    v_ref[...] = kv[:, WIDTH:].astype(v_ref.dtype)

    k_r = lat_ref[:, KV_RANK:KV_RANK + HEAD].astype(jnp.float32)
    k_r = k_r * _rms_scale(k_r, ROPE) * gk[:, HEAD:]
    k_rope = _rope_pad(k_r, cos, sin).astype(k_ref.dtype)
    for h in range(N_HEADS):
        kn = kv[:, h * HEAD:(h + 1) * HEAD]
        kn = kn * _rms_scale(kn, HEAD) * gk[:, :HEAD]
        k_ref[:, 2 * h * HEAD:(2 * h + 1) * HEAD] = kn.astype(k_ref.dtype)
        k_ref[:, (2 * h + 1) * HEAD:(2 * h + 2) * HEAD] = k_rope

    c_q = lat_ref[:, 2 * HEAD + KV_RANK:].astype(jnp.float32)
    cqn = (c_q * _rms_scale(c_q, Q_RANK) * gqa_ref[...]).astype(jnp.bfloat16)
    q = jnp.dot(cqn, wq_ref[...], preferred_element_type=jnp.float32)
    for h in range(N_HEADS):
        qn = q[:, 2 * h * HEAD:(2 * h + 1) * HEAD]
        qn = qn * _rms_scale(qn, HEAD) * gq[:, :HEAD]
        q_ref[:, 2 * h * HEAD:(2 * h + 1) * HEAD] = qn.astype(q_ref.dtype)
        qr = q[:, (2 * h + 1) * HEAD:(2 * h + 2) * HEAD]
        qr = qr * _rms_scale(qr, ROPE) * gq[:, HEAD:]
        q_ref[:, (2 * h + 1) * HEAD:(2 * h + 2) * HEAD] = _rope_pad(qr, cos, sin).astype(q_ref.dtype)


def _mla_prep(proj, wq, wkv, gqa, gkva, gq, gk, cos, sin, *, seq, tm=512):
    m = proj.shape[0]
    nseq = seq // tm
    const = lambda i: (0, 0)
    return pl.pallas_call(
        _mla_prep_kernel,
        grid=(m // tm,),
        in_specs=[
            pl.BlockSpec((tm, LAT_COLS), lambda i: (i, 0)),
            pl.BlockSpec(wq.shape, const),
            pl.BlockSpec(wkv.shape, const),
            pl.BlockSpec(gqa.shape, const),
            pl.BlockSpec(gkva.shape, const),
            pl.BlockSpec(gq.shape, const),
            pl.BlockSpec(gk.shape, const),
            pl.BlockSpec((tm, HEAD), lambda i: (i % nseq, 0)),
            pl.BlockSpec((tm, HEAD), lambda i: (i % nseq, 0)),
        ],
        out_specs=[
            pl.BlockSpec((tm, 2 * WIDTH), lambda i: (i, 0)),
            pl.BlockSpec((tm, 2 * WIDTH), lambda i: (i, 0)),
            pl.BlockSpec((tm, WIDTH), lambda i: (i, 0)),
        ],
        out_shape=[
            jax.ShapeDtypeStruct((m, 2 * WIDTH), jnp.bfloat16),
            jax.ShapeDtypeStruct((m, 2 * WIDTH), jnp.bfloat16),
            jax.ShapeDtypeStruct((m, WIDTH), jnp.bfloat16),
        ],
        compiler_params=pltpu.CompilerParams(
            dimension_semantics=("parallel",), vmem_limit_bytes=VMEM_LIMIT),
        name="mla_prep",
    )(proj, wq, wkv, gqa, gkva, gq, gk, cos, sin)


def _mla_attn_kernel(q_ref, k_ref, v_ref, o_ref, m_ref, l_ref, acc_ref, *, tq, tk):
    i = pl.program_id(2)
    q = q_ref[...]
    m_ref[...] = jnp.full(m_ref.shape, NEG, jnp.float32)
    l_ref[...] = jnp.zeros(l_ref.shape, jnp.float32)
    acc_ref[...] = jnp.zeros(acc_ref.shape, jnp.float32)

    def step(j, masked):
        k0 = pl.multiple_of(j * tk, tk)
        k = k_ref[pl.ds(k0, tk), :]
        v = v_ref[pl.ds(k0, tk), :]
        s = lax.dot_general(q, k, (((1,), (1,)), ((), ())), preferred_element_type=jnp.float32)
        if masked:
            qpos = i * tq + lax.broadcasted_iota(jnp.int32, (tq, tk), 0)
            kpos = j * tk + lax.broadcasted_iota(jnp.int32, (tq, tk), 1)
            s = jnp.where(kpos <= qpos, s, NEG)
        m_prev = m_ref[...]
        m_new = jnp.maximum(m_prev, jnp.max(s, axis=-1, keepdims=True))
        alpha = jnp.exp(m_prev - m_new)
        p = jnp.exp(s - m_new[:, :1])
        l_ref[...] = alpha * l_ref[...] + jnp.sum(p, axis=-1, keepdims=True)
        acc_ref[...] = alpha * acc_ref[...] + jnp.dot(p.astype(v.dtype), v,
                                                      preferred_element_type=jnp.float32)
        m_ref[...] = m_new

    nfull = i * (tq // tk)

    def body(j, carry):
        step(j, False)
        return carry
    lax.fori_loop(0, nfull, body, 0)
    for d in range(tq // tk):
        step(nfull + d, True)

    o_ref[...] = (acc_ref[...] / l_ref[...]).astype(o_ref.dtype)


def _mla_attn(q, k, v, *, batch, seq, tq=512, tk=512):
    nq = seq // tq
    return pl.pallas_call(
        functools.partial(_mla_attn_kernel, tq=tq, tk=tk),
        grid=(batch, N_HEADS, nq),
        in_specs=[
            pl.BlockSpec((tq, 2 * HEAD), lambda b, h, i: (b * nq + i, h)),
            pl.BlockSpec((seq, 2 * HEAD), lambda b, h, i: (b, h)),
            pl.BlockSpec((seq, HEAD), lambda b, h, i: (b, h)),
        ],
        out_specs=pl.BlockSpec((tq, HEAD), lambda b, h, i: (b * nq + i, h)),
        out_shape=jax.ShapeDtypeStruct((batch * seq, WIDTH), jnp.bfloat16),
        scratch_shapes=[
            pltpu.VMEM((tq, HEAD), jnp.float32),
            pltpu.VMEM((tq, HEAD), jnp.float32),
            pltpu.VMEM((tq, HEAD), jnp.float32),
        ],
        compiler_params=pltpu.CompilerParams(
            dimension_semantics=("parallel", "parallel", "arbitrary"),
            vmem_limit_bytes=VMEM_LIMIT),
        name="mla_attn",
    )(q, k, v)


def _dil_attn_kernel(q_ref, k_ref, v_ref, cos_ref, sin_ref, gq_ref, gk_ref, o_ref,
                     qn_ref, kn_ref, vf_ref, qd_ref, kd_ref, vd_ref, on_ref, ls_ref, *, seq):
    chunk = 512
    npat = len(DILATIONS)

    def prep(c, carry):
        r0 = pl.multiple_of(c * chunk, chunk)
        cos = cos_ref[pl.ds(r0, chunk), :]
        sin = sin_ref[pl.ds(r0, chunk), :]
        q = q_ref[pl.ds(r0, chunk), :].astype(jnp.float32)
        q = q * _rms_scale(q, HEAD) * gq_ref[...]
        qn_ref[pl.ds(r0, chunk), :] = q * cos + pltpu.roll(q, 64, 1) * sin
        k = k_ref[pl.ds(r0, chunk), :].astype(jnp.float32)
        k = k * _rms_scale(k, HEAD) * gk_ref[...]
        kn_ref[pl.ds(r0, chunk), :] = k * cos + pltpu.roll(k, 64, 1) * sin
        vf_ref[pl.ds(r0, chunk), :] = v_ref[pl.ds(r0, chunk), :].astype(jnp.float32)
        return carry
    lax.fori_loop(0, seq // chunk, prep, 0)

    for p, d in enumerate(DILATIONS):
        sub = seq // d
        kd_ref[p, 0:QBLK, :] = jnp.zeros((QBLK, HEAD), kd_ref.dtype)
        vd_ref[p, 0:QBLK, :] = jnp.zeros((QBLK, HEAD), vd_ref.dtype)
        for r in range(d):
            rows = pl.ds(r, sub, stride=d) if d > 1 else pl.ds(0, sub)
            qd_ref[p, r * sub:(r + 1) * sub, :] = qn_ref[rows, :].astype(qd_ref.dtype)
            kd_ref[p, QBLK + r * sub:QBLK + (r + 1) * sub, :] = kn_ref[rows, :].astype(kd_ref.dtype)
            vd_ref[p, QBLK + r * sub:QBLK + (r + 1) * sub, :] = vf_ref[rows, :].astype(vd_ref.dtype)

    qi = lax.broadcasted_iota(jnp.int32, (QBLK, 2 * QBLK), 0)
    kj = lax.broadcasted_iota(jnp.int32, (QBLK, 2 * QBLK), 1)
    band = (kj >= qi) & (kj <= qi + N_BACK)

    for p, d in enumerate(DILATIONS):
        nb = seq // d // QBLK
        shift = int(math.log2(nb))

        def block(g, carry, p=p, d=d, nb=nb, shift=shift):
            g0 = pl.multiple_of(g * QBLK, QBLK)
            q = qd_ref[p, pl.ds(g0, QBLK), :]
            k = kd_ref[p, pl.ds(g0, 2 * QBLK), :]
            v = vd_ref[p, pl.ds(g0, 2 * QBLK), :]
            s = lax.dot_general(q, k, (((1,), (1,)), ((), ())), preferred_element_type=jnp.float32)
            n = g & (nb - 1)
            lo = jnp.where(n == 0, QBLK, 0)
            s = jnp.where(band & (kj >= lo), s, NEG)
            m = jnp.max(s, axis=-1, keepdims=True)
            e = jnp.exp(s - m)
            den = jnp.sum(e, axis=-1, keepdims=True)
            o = jnp.dot(e.astype(v.dtype), v, preferred_element_type=jnp.float32) / den
            lse = jnp.broadcast_to(m + jnp.log(den), (QBLK, HEAD))
            r = g >> shift
            start = r + n * (QBLK * d)
            rows = pl.ds(start, QBLK, stride=d) if d > 1 else pl.ds(g0, QBLK)
            on_ref[p, rows, :] = o
            ls_ref[p, rows, :] = lse
            return carry
        lax.fori_loop(0, seq // QBLK, block, 0)

    def mix(c, carry):
        r0 = pl.multiple_of(c * chunk, chunk)
        ls = [ls_ref[p, pl.ds(r0, chunk), :] for p in range(npat)]
        mx = functools.reduce(jnp.maximum, ls)
        w = [jnp.exp(l - mx) for l in ls]
        num = sum(w[p] * on_ref[p, pl.ds(r0, chunk), :] for p in range(npat))
        o_ref[pl.ds(r0, chunk), :] = (num / sum(w)).astype(o_ref.dtype)
        return carry
    lax.fori_loop(0, seq // chunk, mix, 0)


def _dil_attn(proj, cos, sin, gq, gk, *, batch, seq):
    npat = len(DILATIONS)
    col = HEAD

    def qcol(h):
        return jnp.where(h < 4, 1536 // col + h, 4096 // col - 4 + h)

    const = lambda b, h: (0, 0)
    return pl.pallas_call(
        functools.partial(_dil_attn_kernel, seq=seq),
        grid=(batch, N_HEADS),
        in_specs=[
            pl.BlockSpec((seq, HEAD), lambda b, h: (b, qcol(h))),
            pl.BlockSpec((seq, HEAD), lambda b, h: (b, 4608 // col + h)),
            pl.BlockSpec((seq, HEAD), lambda b, h: (b, 5632 // col + h)),
            pl.BlockSpec((seq, HEAD), const),
            pl.BlockSpec((seq, HEAD), const),
            pl.BlockSpec((1, HEAD), const),
            pl.BlockSpec((1, HEAD), const),
        ],
        out_specs=pl.BlockSpec((seq, HEAD), lambda b, h: (b, h)),
        out_shape=jax.ShapeDtypeStruct((batch * seq, WIDTH), jnp.bfloat16),
        scratch_shapes=[
            pltpu.VMEM((seq, HEAD), jnp.float32),
            pltpu.VMEM((seq, HEAD), jnp.float32),
            pltpu.VMEM((seq, HEAD), jnp.float32),
            pltpu.VMEM((npat, seq, HEAD), jnp.bfloat16),
            pltpu.VMEM((npat, seq + QBLK, HEAD), jnp.bfloat16),
            pltpu.VMEM((npat, seq + QBLK, HEAD), jnp.bfloat16),
            pltpu.VMEM((npat, seq, HEAD), jnp.float32),
            pltpu.VMEM((npat, seq, HEAD), jnp.float32),
        ],
        compiler_params=pltpu.CompilerParams(
            dimension_semantics=("parallel", "parallel"), vmem_limit_bytes=VMEM_LIMIT),
        name="dil_attn",
    )(proj, proj, proj, cos, sin, gq, gk)


def _out_proj_kernel(oa_ref, ob_ref, g_ref, x_ref, ga_ref, gb_ref, w_ref, o_ref):
    def branch(o_ref_, gain_ref, gate):
        o = o_ref_[...].astype(jnp.float32)
        y = o * _rms_scale(o, WIDTH) * gain_ref[...]
        return (y * (gate / (1.0 + jnp.exp(-gate)))).astype(jnp.bfloat16)

    ya = branch(oa_ref, ga_ref, g_ref[:, :WIDTH].astype(jnp.float32))
    yb = branch(ob_ref, gb_ref, g_ref[:, WIDTH:].astype(jnp.float32))
    y = jnp.concatenate([ya, yb], axis=-1)
    o_ref[...] = x_ref[...] + jnp.dot(y, w_ref[...], preferred_element_type=jnp.float32)


def _out_proj(o_a, o_b, proj, x2, ga, gb, w, *, tm=512):
    m = x2.shape[0]
    const = lambda i: (0, 0)
    return pl.pallas_call(
        _out_proj_kernel,
        grid=(m // tm,),
        in_specs=[
            pl.BlockSpec((tm, WIDTH), lambda i: (i, 0)),
            pl.BlockSpec((tm, WIDTH), lambda i: (i, 0)),
            pl.BlockSpec((tm, 2 * WIDTH), lambda i: (i, 1)),
            pl.BlockSpec((tm, D_MODEL), lambda i: (i, 0)),
            pl.BlockSpec((1, WIDTH), const),
            pl.BlockSpec((1, WIDTH), const),
            pl.BlockSpec((2 * WIDTH, D_MODEL), const),
        ],
        out_specs=pl.BlockSpec((tm, D_MODEL), lambda i: (i, 0)),
        out_shape=jax.ShapeDtypeStruct((m, D_MODEL), jnp.float32),
        compiler_params=pltpu.CompilerParams(
            dimension_semantics=("parallel",), vmem_limit_bytes=VMEM_LIMIT),
        name="out_proj",
    )(o_a, o_b, proj, x2, ga, gb, w)


def _pad_rope_cols(a):
    z = jnp.zeros(a.shape[:-1] + (ROPE // 2,), a.dtype)
    return jnp.concatenate([a[..., :ROPE // 2], z, a[..., ROPE // 2:], z], axis=-1)


def _rope_tables(seq, d, padded):
    inv = ROPE_THETA ** (-jnp.arange(0, d, 2, dtype=jnp.float32) / d)
    ang = jnp.arange(seq, dtype=jnp.int32).astype(jnp.float32)[:, None] * inv[None, :]
    cos, sin = jnp.cos(ang), jnp.sin(ang)
    cos2 = jnp.concatenate([cos, cos], axis=-1)
    sin2 = jnp.concatenate([-sin, sin], axis=-1)
    if padded:
        cos2, sin2 = _pad_rope_cols(cos2), _pad_rope_cols(sin2)
    return cos2, sin2


def kernel(x, norm_gain, w_in, q_a_norm_gain, kv_a_norm_gain, w_uq, w_ukv, mla_q_norm_gain,
           mla_k_norm_gain, dil_q_norm_gain, dil_k_norm_gain, mla_out_norm_gain,
           dil_out_norm_gain, w_out):
    batch, seq, _ = x.shape
    depth = w_in.shape[0]
    bf = jnp.bfloat16
    cos_d, sin_d = _rope_tables(seq, HEAD, padded=False)
    cos_m, sin_m = _rope_tables(seq, ROPE, padded=True)
    h2 = x.reshape(batch * seq, D_MODEL)

    for l in range(depth):
        wi = w_in[l]
        o_cq, o_ckv, o_kr = 0, Q_RANK, Q_RANK + KV_RANK
        o_ga = o_kr + ROPE
        o_qb = o_ga + WIDTH
        o_kb, o_vb, o_gb = o_qb + WIDTH, o_qb + 2 * WIDTH, o_qb + 3 * WIDTH
        zeros = jnp.zeros((D_MODEL, HEAD), wi.dtype)
        w_r = jnp.concatenate([
            wi[:, o_ckv:o_kr], _pad_rope_cols(wi[:, o_kr:o_ga]), zeros, wi[:, o_cq:o_ckv],
            wi[:, o_qb:o_qb + 4 * HEAD], wi[:, o_ga:o_qb], wi[:, o_gb:o_gb + WIDTH],
            wi[:, o_qb + 4 * HEAD:o_kb], wi[:, o_kb:o_vb], wi[:, o_vb:o_gb]], axis=-1).astype(bf)

        wq = w_uq[l].reshape(Q_RANK, N_HEADS, MLA_QK)
        wq_r = jnp.concatenate([wq[..., :HEAD], _pad_rope_cols(wq[..., HEAD:])], axis=-1)
        wq_r = wq_r.reshape(Q_RANK, N_HEADS * 2 * HEAD).astype(bf)
        wkv = w_ukv[l].reshape(KV_RANK, N_HEADS, 2 * HEAD)
        wkv_r = jnp.concatenate([wkv[..., :HEAD].reshape(KV_RANK, WIDTH),
                                 wkv[..., HEAD:].reshape(KV_RANK, WIDTH)], axis=-1).astype(bf)

        gq_m = mla_q_norm_gain[l]
        gk_m = mla_k_norm_gain[l]
        gq_full = jnp.concatenate([gq_m[:HEAD], _pad_rope_cols(gq_m[HEAD:])])[None, :]
        gq_full = gq_full * (1.0 / math.sqrt(MLA_QK))
        gk_full = jnp.concatenate([gk_m[:HEAD], _pad_rope_cols(gk_m[HEAD:])])[None, :]
        gq_d = dil_q_norm_gain[l][None, :] * (1.0 / math.sqrt(HEAD))
        gk_d = dil_k_norm_gain[l][None, :]

        proj = _in_proj(h2, norm_gain[l][None, :], w_r)
        q_f, k_f, v_a = _mla_prep(proj, wq_r, wkv_r, q_a_norm_gain[l][None, :],
                                  kv_a_norm_gain[l][None, :], gq_full, gk_full, cos_m, sin_m, seq=seq)
        o_a = _mla_attn(q_f, k_f, v_a, batch=batch, seq=seq)
        o_b = _dil_attn(proj, cos_d, sin_d, gq_d, gk_d, batch=batch, seq=seq)
        h2 = _out_proj(o_a, o_b, proj, h2, mla_out_norm_gain[l][None, :],
                       dil_out_norm_gain[l][None, :], w_out[l].astype(bf))
    return h2.reshape(batch, seq, D_MODEL)
```

```python
import functools
import math

import jax
import jax.numpy as jnp
from jax import lax
from jax.experimental import pallas as pl
from jax.experimental.pallas import tpu as pltpu

EPS = 1e-6
ROPE_THETA = 10000.0
NEG = -1e30

D_MODEL = 2048
N_HEADS = 8
HEAD = 128
ROPE = 64
Q_RANK = 768
KV_RANK = 512
WIDTH = N_HEADS * HEAD
MLA_QK = HEAD + ROPE
QBLK = 128
N_BACK = 128
DILATIONS = (1, 4, 16)

PROJ_COLS = 6656
LAT_COLS = 1536
VMEM_LIMIT = 56 * 1024 * 1024


def _rms_scale(x, n):
    return lax.rsqrt(jnp.sum(x * x, axis=-1, keepdims=True) * (1.0 / n) + EPS)


def _rms_scale_mxu(x, n):
    ones = jnp.ones((x.shape[-1], x.shape[-1]), jnp.bfloat16)
    ss = jnp.dot((x * x).astype(jnp.bfloat16), ones, preferred_element_type=jnp.float32)
    return lax.rsqrt(ss * (1.0 / n) + EPS)


def _in_proj_kernel(x_ref, g_ref, w_ref, o_ref, hn_ref, *, row_chunk):
    @pl.when(pl.program_id(1) == 0)
    def _():
        def body(c, carry):
            r0 = pl.multiple_of(c * row_chunk, row_chunk)
            x = x_ref[pl.ds(r0, row_chunk), :]
            hn = x * _rms_scale(x, D_MODEL) * g_ref[...]
            hn_ref[pl.ds(r0, row_chunk), :] = hn.astype(hn_ref.dtype)
            return carry
        lax.fori_loop(0, x_ref.shape[0] // row_chunk, body, 0)

    o_ref[...] = jnp.dot(hn_ref[...], w_ref[...],
                         preferred_element_type=jnp.float32).astype(o_ref.dtype)


def _in_proj(x2, gain, w, *, tm=1024, tn=512):
    m = x2.shape[0]
    return pl.pallas_call(
        functools.partial(_in_proj_kernel, row_chunk=64),
        grid=(m // tm, PROJ_COLS // tn),
        in_specs=[
            pl.BlockSpec((tm, D_MODEL), lambda i, j: (i, 0)),
            pl.BlockSpec((1, D_MODEL), lambda i, j: (0, 0)),
            pl.BlockSpec((D_MODEL, tn), lambda i, j: (0, j)),
        ],
        out_specs=pl.BlockSpec((tm, tn), lambda i, j: (i, j)),
        out_shape=jax.ShapeDtypeStruct((m, PROJ_COLS), jnp.bfloat16),
        scratch_shapes=[pltpu.VMEM((tm, D_MODEL), jnp.bfloat16)],
        compiler_params=pltpu.CompilerParams(
            dimension_semantics=("parallel", "arbitrary"),
            vmem_limit_bytes=VMEM_LIMIT),
        name="in_proj",
    )(x2, gain, w)


def _rope_pad(x, cos, sin):
    return x * cos + pltpu.roll(x, 64, 1) * sin


def _mla_prep_kernel(lat_ref, wq_ref, wkv_ref, gqa_ref, gkva_ref, gq_ref, gk_ref,
                     cos_ref, sin_ref, q_ref, k_ref, v_ref):
    cos = cos_ref[...]
    sin = sin_ref[...]
    gq = gq_ref[...]
    gk = gk_ref[...]

    c_kv = lat_ref[:, 0:KV_RANK].astype(jnp.float32)
    cn = (c_kv * _rms_scale(c_kv, KV_RANK) * gkva_ref[...]).astype(jnp.bfloat16)
    kv = jnp.dot(cn, wkv_ref[...], preferred_element_type=jnp.float32)
    ones = jnp.ones((kv.shape[0], HEAD), v_ref.dtype)
    for h in range(N_HEADS):
        v_ref[:, 2 * h * HEAD:(2 * h + 1) * HEAD] = kv[:, WIDTH + h * HEAD:WIDTH + (h + 1) * HEAD].astype(v_ref.dtype)
        v_ref[:, (2 * h + 1) * HEAD:(2 * h + 2) * HEAD] = ones

    k_r = lat_ref[:, KV_RANK:KV_RANK + HEAD].astype(jnp.float32)
    k_r = k_r * _rms_scale(k_r, ROPE) * gk[:, HEAD:]
    k_rope = _rope_pad(k_r, cos, sin).astype(k_ref.dtype)
    for h in range(N_HEADS):
        kn = kv[:, h * HEAD:(h + 1) * HEAD]
        kn = kn * _rms_scale(kn, HEAD) * gk[:, :HEAD]
        k_ref[:, 2 * h * HEAD:(2 * h + 1) * HEAD] = kn.astype(k_ref.dtype)
        k_ref[:, (2 * h + 1) * HEAD:(2 * h + 2) * HEAD] = k_rope

    c_q = lat_ref[:, 2 * HEAD + KV_RANK:].astype(jnp.float32)
    cqn = (c_q * _rms_scale(c_q, Q_RANK) * gqa_ref[...]).astype(jnp.bfloat16)
    q = jnp.dot(cqn, wq_ref[...], preferred_element_type=jnp.float32)
    for h in range(N_HEADS):
        qn = q[:, 2 * h * HEAD:(2 * h + 1) * HEAD]
        qn = qn * _rms_scale(qn, HEAD) * gq[:, :HEAD]
        q_ref[:, 2 * h * HEAD:(2 * h + 1) * HEAD] = qn.astype(q_ref.dtype)
        qr = q[:, (2 * h + 1) * HEAD:(2 * h + 2) * HEAD]
        qr = qr * _rms_scale(qr, ROPE) * gq[:, HEAD:]
        q_ref[:, (2 * h + 1) * HEAD:(2 * h + 2) * HEAD] = _rope_pad(qr, cos, sin).astype(q_ref.dtype)


def _mla_prep(proj, wq, wkv, gqa, gkva, gq, gk, cos, sin, *, seq, tm=512):
    m = proj.shape[0]
    nseq = seq // tm
    const = lambda i: (0, 0)
    return pl.pallas_call(
        _mla_prep_kernel,
        grid=(m // tm,),
        in_specs=[
            pl.BlockSpec((tm, LAT_COLS), lambda i: (i, 0)),
            pl.BlockSpec(wq.shape, const),
            pl.BlockSpec(wkv.shape, const),
            pl.BlockSpec(gqa.shape, const),
            pl.BlockSpec(gkva.shape, const),
            pl.BlockSpec(gq.shape, const),
            pl.BlockSpec(gk.shape, const),
            pl.BlockSpec((tm, HEAD), lambda i: (i % nseq, 0)),
            pl.BlockSpec((tm, HEAD), lambda i: (i % nseq, 0)),
        ],
        out_specs=[
            pl.BlockSpec((tm, 2 * WIDTH), lambda i: (i, 0)),
            pl.BlockSpec((tm, 2 * WIDTH), lambda i: (i, 0)),
            pl.BlockSpec((tm, 2 * WIDTH), lambda i: (i, 0)),
        ],
        out_shape=[
            jax.ShapeDtypeStruct((m, 2 * WIDTH), jnp.bfloat16),
            jax.ShapeDtypeStruct((m, 2 * WIDTH), jnp.bfloat16),
            jax.ShapeDtypeStruct((m, 2 * WIDTH), jnp.bfloat16),
        ],
        compiler_params=pltpu.CompilerParams(
            dimension_semantics=("parallel",), vmem_limit_bytes=VMEM_LIMIT),
        name="mla_prep",
    )(proj, wq, wkv, gqa, gkva, gq, gk, cos, sin)


def _mla_attn_kernel(q_ref, k_ref, v_ref, o_ref, m_ref, acc_ref, *, tq, tk, heads):
    i = pl.program_id(2)
    m_ref[...] = jnp.full(m_ref.shape, NEG, jnp.float32)
    acc_ref[...] = jnp.zeros(acc_ref.shape, jnp.float32)

    def step(j, masked):
        k0 = pl.multiple_of(j * tk, tk)
        for h in range(heads):
            cols = slice(2 * h * HEAD, 2 * (h + 1) * HEAD)
            q = q_ref[:, cols]
            k = k_ref[pl.ds(k0, tk), cols]
            v = v_ref[pl.ds(k0, tk), cols]
            s = lax.dot_general(q, k, (((1,), (1,)), ((), ())), preferred_element_type=jnp.float32)
            if masked:
                qpos = i * tq + lax.broadcasted_iota(jnp.int32, (tq, tk), 0)
                kpos = j * tk + lax.broadcasted_iota(jnp.int32, (tq, tk), 1)
                s = jnp.where(kpos <= qpos, s, NEG)
            m_prev = m_ref[h]
            m_new = jnp.maximum(m_prev, jnp.max(s, axis=-1, keepdims=True))
            alpha = jnp.exp2(m_prev - m_new)
            p = jnp.exp2(s - jnp.tile(m_new, (1, tk // HEAD)))
            pv = jnp.dot(p.astype(v.dtype), v, preferred_element_type=jnp.float32)
            acc_ref[h] = jnp.tile(alpha, (1, 2)) * acc_ref[h] + pv
            m_ref[h] = m_new

    nfull = i * (tq // tk)

    def body(j, carry):
        step(j, False)
        return carry
    lax.fori_loop(0, nfull, body, 0)
    for d in range(tq // tk):
        step(nfull + d, True)

    for h in range(heads):
        acc = acc_ref[h]
        o_ref[:, h * HEAD:(h + 1) * HEAD] = (acc[:, :HEAD] / acc[:, HEAD:]).astype(o_ref.dtype)


def _mla_attn(q, k, v, *, batch, seq, tq=512, tk=512, heads=4):
    nq = seq // tq
    w = 2 * HEAD * heads
    return pl.pallas_call(
        functools.partial(_mla_attn_kernel, tq=tq, tk=tk, heads=heads),
        grid=(batch, N_HEADS // heads, nq),
        in_specs=[
            pl.BlockSpec((tq, w), lambda b, h, i: (b * nq + i, h)),
            pl.BlockSpec((seq, w), lambda b, h, i: (b, h)),
            pl.BlockSpec((seq, w), lambda b, h, i: (b, h)),
        ],
        out_specs=pl.BlockSpec((tq, HEAD * heads), lambda b, h, i: (b * nq + i, h)),
        out_shape=jax.ShapeDtypeStruct((batch * seq, WIDTH), jnp.bfloat16),
        scratch_shapes=[
            pltpu.VMEM((heads, tq, HEAD), jnp.float32),
            pltpu.VMEM((heads, tq, 2 * HEAD), jnp.float32),
        ],
        compiler_params=pltpu.CompilerParams(
            dimension_semantics=("parallel", "parallel", "arbitrary"),
            vmem_limit_bytes=VMEM_LIMIT),
        name="mla_attn",
    )(q, k, v)


def _dil_attn_kernel(q_ref, k_ref, v_ref, cos_ref, sin_ref, gq_ref, gk_ref, o_ref,
                     qd_ref, kd_ref, vd_ref, on_ref, ls_ref, bias_ref, *, seq, unroll):
    chunk = 512
    npat = len(DILATIONS)
    assert DILATIONS == (1, 4, 16)
    nat = [on_ref.at[a] for a in range(3)]
    by4 = [ls_ref.at[a] for a in range(3)]
    dst = [qd_ref, kd_ref, vd_ref]
    lead = [0, QBLK, QBLK]

    qi = lax.broadcasted_iota(jnp.int32, (QBLK, 2 * QBLK), 0)
    kj = lax.broadcasted_iota(jnp.int32, (QBLK, 2 * QBLK), 1)
    band = (kj >= qi) & (kj <= qi + N_BACK)
    bias_ref[0] = jnp.where(band, 0.0, NEG)
    bias_ref[1] = jnp.where(band & (kj >= QBLK), 0.0, NEG)
    ones = jnp.ones((chunk, HEAD), vd_ref.dtype)

    def prep(c, carry):
        r0 = pl.multiple_of(c * chunk, chunk)
        cos = cos_ref[pl.ds(r0, chunk), :]
        sin = sin_ref[pl.ds(r0, chunk), :]
        q = q_ref[pl.ds(r0, chunk), :].astype(jnp.float32)
        q = q * _rms_scale_mxu(q, HEAD) * gq_ref[...]
        q = q * cos + pltpu.roll(q, 64, 1) * sin
        k = k_ref[pl.ds(r0, chunk), :].astype(jnp.float32)
        k = k * _rms_scale_mxu(k, HEAD) * gk_ref[...]
        k = k * cos + pltpu.roll(k, 64, 1) * sin
        v = v_ref[pl.ds(r0, chunk), :]
        nat[0][pl.ds(r0, chunk), :] = q
        nat[1][pl.ds(r0, chunk), :] = k
        nat[2][pl.ds(r0, chunk), :] = v.astype(jnp.float32)
        qd_ref[0, pl.ds(r0, chunk), :] = q.astype(qd_ref.dtype)
        kd_ref[0, pl.ds(QBLK + r0, chunk), :] = k.astype(kd_ref.dtype)
        vd_ref[0, pl.ds(QBLK + r0, chunk), 0:HEAD] = v
        for p in range(npat):
            vd_ref[p, pl.ds(QBLK + r0, chunk), HEAD:] = ones
        return carry
    lax.fori_loop(0, seq // chunk, prep, 0)

    for p in range(npat):
        kd_ref[p, 0:QBLK, :] = jnp.zeros((QBLK, HEAD), kd_ref.dtype)
        vd_ref[p, 0:QBLK, :] = jnp.zeros((QBLK, 2 * HEAD), vd_ref.dtype)

    sub4, sub16 = seq // 4, seq // 16
    for a in range(3):
        for r4 in range(4):
            x = nat[a][pl.ds(r4, sub4, stride=4), :]
            by4[a][r4 * sub4:(r4 + 1) * sub4, :] = x
            dst[a][1, lead[a] + r4 * sub4:lead[a] + (r4 + 1) * sub4, 0:HEAD] = x.astype(dst[a].dtype)
    for a in range(3):
        for r4 in range(4):
            for j in range(4):
                r16 = r4 + 4 * j
                x = by4[a][pl.ds(r4 * sub4 + j, sub16, stride=4), :]
                dst[a][2, lead[a] + r16 * sub16:lead[a] + (r16 + 1) * sub16, 0:HEAD] = x.astype(dst[a].dtype)

    for p, d in enumerate(DILATIONS):
        nb = seq // d // QBLK
        shift = int(math.log2(nb))

        def block(g, carry, p=p, d=d, nb=nb, shift=shift):
            g0 = pl.multiple_of(g * QBLK, QBLK)
            q = qd_ref[p, pl.ds(g0, QBLK), :]
            k = kd_ref[p, pl.ds(g0, 2 * QBLK), :]
            v = vd_ref[p, pl.ds(g0, 2 * QBLK), :]
            n = g & (nb - 1)
            s = lax.dot_general(q, k, (((1,), (1,)), ((), ())), preferred_element_type=jnp.float32)
            s = s + bias_ref[jnp.where(n == 0, 1, 0)]
            m = jnp.max(s, axis=-1, keepdims=True)
            e = jnp.exp2(s - m)
            pv = jnp.dot(e.astype(v.dtype), v, preferred_element_type=jnp.float32)
            den = pv[:, HEAD:]
            r = g >> shift
            start = r + n * (QBLK * d)
            rows = pl.ds(start, QBLK, stride=d) if d > 1 else pl.ds(g0, QBLK)
            on_ref[p, rows, :] = pv[:, :HEAD] / den
            ls_ref[p, rows, :] = m + jnp.log2(den)
            return carry
        lax.fori_loop(0, seq // QBLK, block, 0, unroll=unroll)

    def mix(c, carry):
        r0 = pl.multiple_of(c * chunk, chunk)
        ls = [ls_ref[p, pl.ds(r0, chunk), :] for p in range(npat)]
        mx = functools.reduce(jnp.maximum, ls)
        w = [jnp.exp2(l - mx) for l in ls]
        num = sum(w[p] * on_ref[p, pl.ds(r0, chunk), :] for p in range(npat))
        o_ref[pl.ds(r0, chunk), :] = (num / sum(w)).astype(o_ref.dtype)
        return carry
    lax.fori_loop(0, seq // chunk, mix, 0)


def _dil_attn(proj, cos, sin, gq, gk, *, batch, seq):
    npat = len(DILATIONS)
    col = HEAD

    def qcol(h):
        return jnp.where(h < 4, 1536 // col + h, 4096 // col - 4 + h)

    const = lambda b, h: (0, 0)
    return pl.pallas_call(
        functools.partial(_dil_attn_kernel, seq=seq, unroll=16),
        grid=(batch, N_HEADS),
        in_specs=[
            pl.BlockSpec((seq, HEAD), lambda b, h: (b, qcol(h))),
            pl.BlockSpec((seq, HEAD), lambda b, h: (b, 4608 // col + h)),
            pl.BlockSpec((seq, HEAD), lambda b, h: (b, 5632 // col + h)),
            pl.BlockSpec((seq, HEAD), const),
            pl.BlockSpec((seq, HEAD), const),
            pl.BlockSpec((1, HEAD), const),
            pl.BlockSpec((1, HEAD), const),
        ],
        out_specs=pl.BlockSpec((seq, HEAD), lambda b, h: (b, h)),
        out_shape=jax.ShapeDtypeStruct((batch * seq, WIDTH), jnp.bfloat16),
        scratch_shapes=[
            pltpu.VMEM((npat, seq, HEAD), jnp.bfloat16),
            pltpu.VMEM((npat, seq + QBLK, HEAD), jnp.bfloat16),
            pltpu.VMEM((npat, seq + QBLK, 2 * HEAD), jnp.bfloat16),
            pltpu.VMEM((npat, seq, HEAD), jnp.float32),
            pltpu.VMEM((npat, seq, HEAD), jnp.float32),
            pltpu.VMEM((2, QBLK, 2 * QBLK), jnp.float32),
        ],
        compiler_params=pltpu.CompilerParams(
            dimension_semantics=("parallel", "parallel"), vmem_limit_bytes=VMEM_LIMIT),
        name="dil_attn",
    )(proj, proj, proj, cos, sin, gq, gk)


def _out_proj_kernel(oa_ref, ob_ref, g_ref, x_ref, ga_ref, gb_ref, w_ref, o_ref):
    def branch(o_ref_, gain_ref, gate):
        o = o_ref_[...].astype(jnp.float32)
        y = o * _rms_scale(o, WIDTH) * gain_ref[...]
        return (y * (gate / (1.0 + jnp.exp(-gate)))).astype(jnp.bfloat16)

    ya = branch(oa_ref, ga_ref, g_ref[:, :WIDTH].astype(jnp.float32))
    yb = branch(ob_ref, gb_ref, g_ref[:, WIDTH:].astype(jnp.float32))
    y = jnp.concatenate([ya, yb], axis=-1)
    o_ref[...] = x_ref[...] + jnp.dot(y, w_ref[...], preferred_element_type=jnp.float32)


def _out_proj(o_a, o_b, proj, x2, ga, gb, w, *, tm=512):
    m = x2.shape[0]
    const = lambda i: (0, 0)
    return pl.pallas_call(
        _out_proj_kernel,
        grid=(m // tm,),
        in_specs=[
            pl.BlockSpec((tm, WIDTH), lambda i: (i, 0)),
            pl.BlockSpec((tm, WIDTH), lambda i: (i, 0)),
            pl.BlockSpec((tm, 2 * WIDTH), lambda i: (i, 1)),
            pl.BlockSpec((tm, D_MODEL), lambda i: (i, 0)),
            pl.BlockSpec((1, WIDTH), const),
            pl.BlockSpec((1, WIDTH), const),
            pl.BlockSpec((2 * WIDTH, D_MODEL), const),
        ],
        out_specs=pl.BlockSpec((tm, D_MODEL), lambda i: (i, 0)),
        out_shape=jax.ShapeDtypeStruct((m, D_MODEL), jnp.float32),
        compiler_params=pltpu.CompilerParams(
            dimension_semantics=("parallel",), vmem_limit_bytes=VMEM_LIMIT),
        name="out_proj",
    )(o_a, o_b, proj, x2, ga, gb, w)


def _pad_rope_cols(a):
    z = jnp.zeros(a.shape[:-1] + (ROPE // 2,), a.dtype)
    return jnp.concatenate([a[..., :ROPE // 2], z, a[..., ROPE // 2:], z], axis=-1)


def _rope_tables(seq, d, padded):
    inv = ROPE_THETA ** (-jnp.arange(0, d, 2, dtype=jnp.float32) / d)
    ang = jnp.arange(seq, dtype=jnp.int32).astype(jnp.float32)[:, None] * inv[None, :]
    cos, sin = jnp.cos(ang), jnp.sin(ang)
    cos2 = jnp.concatenate([cos, cos], axis=-1)
    sin2 = jnp.concatenate([-sin, sin], axis=-1)
    if padded:
        cos2, sin2 = _pad_rope_cols(cos2), _pad_rope_cols(sin2)
    return cos2, sin2


def kernel(x, norm_gain, w_in, q_a_norm_gain, kv_a_norm_gain, w_uq, w_ukv, mla_q_norm_gain,
           mla_k_norm_gain, dil_q_norm_gain, dil_k_norm_gain, mla_out_norm_gain,
           dil_out_norm_gain, w_out):
    batch, seq, _ = x.shape
    depth = w_in.shape[0]
    bf = jnp.bfloat16
    cos_d, sin_d = _rope_tables(seq, HEAD, padded=False)
    cos_m, sin_m = _rope_tables(seq, ROPE, padded=True)
    h2 = x.reshape(batch * seq, D_MODEL)

    for l in range(depth):
        wi = w_in[l]
        o_cq, o_ckv, o_kr = 0, Q_RANK, Q_RANK + KV_RANK
        o_ga = o_kr + ROPE
        o_qb = o_ga + WIDTH
        o_kb, o_vb, o_gb = o_qb + WIDTH, o_qb + 2 * WIDTH, o_qb + 3 * WIDTH
        zeros = jnp.zeros((D_MODEL, HEAD), wi.dtype)
        w_r = jnp.concatenate([
            wi[:, o_ckv:o_kr], _pad_rope_cols(wi[:, o_kr:o_ga]), zeros, wi[:, o_cq:o_ckv],
            wi[:, o_qb:o_qb + 4 * HEAD], wi[:, o_ga:o_qb], wi[:, o_gb:o_gb + WIDTH],
            wi[:, o_qb + 4 * HEAD:o_kb], wi[:, o_kb:o_vb], wi[:, o_vb:o_gb]], axis=-1).astype(bf)

        wq = w_uq[l].reshape(Q_RANK, N_HEADS, MLA_QK)
        wq_r = jnp.concatenate([wq[..., :HEAD], _pad_rope_cols(wq[..., HEAD:])], axis=-1)
        wq_r = wq_r.reshape(Q_RANK, N_HEADS * 2 * HEAD).astype(bf)
        wkv = w_ukv[l].reshape(KV_RANK, N_HEADS, 2 * HEAD)
        wkv_r = jnp.concatenate([wkv[..., :HEAD].reshape(KV_RANK, WIDTH),
                                 wkv[..., HEAD:].reshape(KV_RANK, WIDTH)], axis=-1).astype(bf)

        gq_m = mla_q_norm_gain[l]
        gk_m = mla_k_norm_gain[l]
        gq_full = jnp.concatenate([gq_m[:HEAD], _pad_rope_cols(gq_m[HEAD:])])[None, :]
        gq_full = gq_full * (math.log2(math.e) / math.sqrt(MLA_QK))
        gk_full = jnp.concatenate([gk_m[:HEAD], _pad_rope_cols(gk_m[HEAD:])])[None, :]
        gq_d = dil_q_norm_gain[l][None, :] * (math.log2(math.e) / math.sqrt(HEAD))
        gk_d = dil_k_norm_gain[l][None, :]

        proj = _in_proj(h2, norm_gain[l][None, :], w_r)
        q_f, k_f, v_a = _mla_prep(proj, wq_r, wkv_r, q_a_norm_gain[l][None, :],
                                  kv_a_norm_gain[l][None, :], gq_full, gk_full, cos_m, sin_m, seq=seq)
        o_a = _mla_attn(q_f, k_f, v_a, batch=batch, seq=seq)
        o_b = _dil_attn(proj, cos_d, sin_d, gq_d, gk_d, batch=batch, seq=seq)
        h2 = _out_proj(o_a, o_b, proj, h2, mla_out_norm_gain[l][None, :],
                       dil_out_norm_gain[l][None, :], w_out[l].astype(bf))
    return h2.reshape(batch, seq, D_MODEL)
```

```python
import functools
import math

import jax
import jax.numpy as jnp
import numpy as np
from jax import lax
from jax.experimental import pallas as pl
from jax.experimental.pallas import tpu as pltpu

EPS = 1e-6
ROPE_THETA = 10000.0
NEG = -1e30

D_MODEL = 2048
N_HEADS = 8
HEAD = 128
ROPE = 64
Q_RANK = 768
KV_RANK = 512
WIDTH = N_HEADS * HEAD
MLA_QK = HEAD + ROPE
QBLK = 128
N_BACK = 128
DILATIONS = (1, 4, 16)

PROJ_COLS = 6656
LAT_COLS = 1536
VMEM_LIMIT = 56 * 1024 * 1024


def _rms_scale(x, n):
    return lax.rsqrt(jnp.sum(x * x, axis=-1, keepdims=True) * (1.0 / n) + EPS)


def _rms_scale_mxu(x, n):
    ones = jnp.ones((x.shape[-1], x.shape[-1]), jnp.bfloat16)
    ss = jnp.dot((x * x).astype(jnp.bfloat16), ones, preferred_element_type=jnp.float32)
    return lax.rsqrt(ss * (1.0 / n) + EPS)


_O_CQ, _O_CKV, _O_KR = 0, Q_RANK, Q_RANK + KV_RANK
_O_GA = _O_KR + ROPE
_O_QB = _O_GA + WIDTH
_O_KB, _O_VB, _O_GB = _O_QB + WIDTH, _O_QB + 2 * WIDTH, _O_QB + 3 * WIDTH
_IN_COLS = _O_GB + WIDTH


def _w_prep_kernel(w_ref, o_ref):
    half = HEAD // 2

    def put(dst, src, width):
        if src % HEAD == 0:
            v = w_ref[:, src:src + width]
        else:
            assert src % HEAD == half
            hi = min(src + width + half, w_ref.shape[1])
            v = w_ref[:, src - half:hi][:, half:half + width]
        o_ref[:, dst:dst + width] = v.astype(o_ref.dtype)

    put(0, _O_CKV, KV_RANK)
    t = w_ref[:, _O_KR:_O_KR + HEAD]
    lane = lax.broadcasted_iota(jnp.int32, t.shape, 1)
    x1 = jnp.where(lane < ROPE // 2, t, 0.0)
    x2 = jnp.where((lane >= ROPE // 2) & (lane < ROPE), t, 0.0)
    o_ref[:, KV_RANK:KV_RANK + HEAD] = (x1 + pltpu.roll(x2, ROPE // 2, 1)).astype(o_ref.dtype)
    o_ref[:, KV_RANK + HEAD:KV_RANK + 2 * HEAD] = jnp.zeros((w_ref.shape[0], HEAD), o_ref.dtype)
    put(KV_RANK + 2 * HEAD, _O_CQ, Q_RANK)
    put(1536, _O_QB, 4 * HEAD)
    put(2048, _O_GA, WIDTH)
    put(3072, _O_GB, WIDTH)
    put(4096, _O_QB + 4 * HEAD, 4 * HEAD)
    put(4608, _O_KB, WIDTH)
    put(5632, _O_VB, WIDTH)


def _w_prep(w, *, tr=256):
    return pl.pallas_call(
        _w_prep_kernel,
        grid=(D_MODEL // tr,),
        in_specs=[pl.BlockSpec((tr, _IN_COLS), lambda i: (i, 0))],
        out_specs=pl.BlockSpec((tr, PROJ_COLS), lambda i: (i, 0)),
        out_shape=jax.ShapeDtypeStruct((D_MODEL, PROJ_COLS), jnp.bfloat16),
        compiler_params=pltpu.CompilerParams(
            dimension_semantics=("parallel",), vmem_limit_bytes=VMEM_LIMIT),
        name="w_prep",
    )(w)


def _in_proj_kernel(x_ref, g_ref, w_ref, o_ref, hn_ref, *, row_chunk):
    @pl.when(pl.program_id(1) == 0)
    def _():
        def body(c, carry):
            r0 = pl.multiple_of(c * row_chunk, row_chunk)
            x = x_ref[pl.ds(r0, row_chunk), :]
            hn = x * _rms_scale(x, D_MODEL) * g_ref[...]
            hn_ref[pl.ds(r0, row_chunk), :] = hn.astype(hn_ref.dtype)
            return carry
        lax.fori_loop(0, x_ref.shape[0] // row_chunk, body, 0)

    o_ref[...] = jnp.dot(hn_ref[...], w_ref[...],
                         preferred_element_type=jnp.float32).astype(o_ref.dtype)


def _in_proj(x2, gain, w, *, tm=2048, tn=512):
    m = x2.shape[0]
    return pl.pallas_call(
        functools.partial(_in_proj_kernel, row_chunk=64),
        grid=(m // tm, PROJ_COLS // tn),
        in_specs=[
            pl.BlockSpec((tm, D_MODEL), lambda i, j: (i, 0)),
            pl.BlockSpec((1, D_MODEL), lambda i, j: (0, 0)),
            pl.BlockSpec((D_MODEL, tn), lambda i, j: (0, j)),
        ],
        out_specs=pl.BlockSpec((tm, tn), lambda i, j: (i, j)),
        out_shape=jax.ShapeDtypeStruct((m, PROJ_COLS), jnp.bfloat16),
        scratch_shapes=[pltpu.VMEM((tm, D_MODEL), jnp.bfloat16)],
        compiler_params=pltpu.CompilerParams(
            dimension_semantics=("parallel", "arbitrary"),
            vmem_limit_bytes=VMEM_LIMIT),
        name="in_proj",
    )(x2, gain, w)


def _rope_pad(x, cos, sin):
    return x * cos + pltpu.roll(x, 64, 1) * sin


def _mla_prep_kernel(lat_ref, wq_ref, wkv_ref, gqa_ref, gkva_ref, gq_ref, gk_ref,
                     cos_ref, sin_ref, q_ref, k_ref, v_ref):
    cos = cos_ref[...]
    sin = sin_ref[...]
    gq = gq_ref[...]
    gk = gk_ref[...]

    c_kv = lat_ref[:, 0:KV_RANK].astype(jnp.float32)
    cn = (c_kv * _rms_scale(c_kv, KV_RANK) * gkva_ref[...]).astype(jnp.bfloat16)
    kv = jnp.dot(cn, wkv_ref[...], preferred_element_type=jnp.float32)
    ones = jnp.ones((kv.shape[0], HEAD), v_ref.dtype)
    for h in range(N_HEADS):
        v_ref[:, 2 * h * HEAD:(2 * h + 1) * HEAD] = kv[:, WIDTH + h * HEAD:WIDTH + (h + 1) * HEAD].astype(v_ref.dtype)
        v_ref[:, (2 * h + 1) * HEAD:(2 * h + 2) * HEAD] = ones

    k_r = lat_ref[:, KV_RANK:KV_RANK + HEAD].astype(jnp.float32)
    k_r = k_r * _rms_scale(k_r, ROPE) * gk[:, HEAD:]
    k_rope = _rope_pad(k_r, cos, sin).astype(k_ref.dtype)
    for h in range(N_HEADS):
        kn = kv[:, h * HEAD:(h + 1) * HEAD]
        kn = kn * _rms_scale(kn, HEAD) * gk[:, :HEAD]
        k_ref[:, 2 * h * HEAD:(2 * h + 1) * HEAD] = kn.astype(k_ref.dtype)
        k_ref[:, (2 * h + 1) * HEAD:(2 * h + 2) * HEAD] = k_rope

    c_q = lat_ref[:, 2 * HEAD + KV_RANK:].astype(jnp.float32)
    cqn = (c_q * _rms_scale(c_q, Q_RANK) * gqa_ref[...]).astype(jnp.bfloat16)
    q = jnp.dot(cqn, wq_ref[...], preferred_element_type=jnp.float32)
    for h in range(N_HEADS):
        qn = q[:, 2 * h * HEAD:(2 * h + 1) * HEAD]
        qn = qn * _rms_scale(qn, HEAD) * gq[:, :HEAD]
        q_ref[:, 2 * h * HEAD:(2 * h + 1) * HEAD] = qn.astype(q_ref.dtype)
        qr = q[:, (2 * h + 1) * HEAD:(2 * h + 2) * HEAD]
        qr = qr * _rms_scale(qr, ROPE) * gq[:, HEAD:]
        q_ref[:, (2 * h + 1) * HEAD:(2 * h + 2) * HEAD] = _rope_pad(qr, cos, sin).astype(q_ref.dtype)


def _mla_prep(proj, wq, wkv, gqa, gkva, gq, gk, cos, sin, *, seq, tm=512):
    m = proj.shape[0]
    nseq = seq // tm
    const = lambda i: (0, 0)
    return pl.pallas_call(
        _mla_prep_kernel,
        grid=(m // tm,),
        in_specs=[
            pl.BlockSpec((tm, LAT_COLS), lambda i: (i, 0)),
            pl.BlockSpec(wq.shape, const),
            pl.BlockSpec(wkv.shape, const),
            pl.BlockSpec(gqa.shape, const),
            pl.BlockSpec(gkva.shape, const),
            pl.BlockSpec(gq.shape, const),
            pl.BlockSpec(gk.shape, const),
            pl.BlockSpec((tm, HEAD), lambda i: (i % nseq, 0)),
            pl.BlockSpec((tm, HEAD), lambda i: (i % nseq, 0)),
        ],
        out_specs=[
            pl.BlockSpec((tm, 2 * WIDTH), lambda i: (i, 0)),
            pl.BlockSpec((tm, 2 * WIDTH), lambda i: (i, 0)),
            pl.BlockSpec((tm, 2 * WIDTH), lambda i: (i, 0)),
        ],
        out_shape=[
            jax.ShapeDtypeStruct((m, 2 * WIDTH), jnp.bfloat16),
            jax.ShapeDtypeStruct((m, 2 * WIDTH), jnp.bfloat16),
            jax.ShapeDtypeStruct((m, 2 * WIDTH), jnp.bfloat16),
        ],
        compiler_params=pltpu.CompilerParams(
            dimension_semantics=("parallel",), vmem_limit_bytes=VMEM_LIMIT),
        name="mla_prep",
    )(proj, wq, wkv, gqa, gkva, gq, gk, cos, sin)


def _mla_attn_kernel(q_ref, k_ref, v_ref, o_ref, m_ref, acc_ref, sa_ref, sb_ref, *, tq, tk, heads):
    assert tq == tk
    i = pl.program_id(2)
    m_ref[...] = jnp.full(m_ref.shape, NEG, jnp.float32)
    acc_ref[...] = jnp.zeros(acc_ref.shape, jnp.float32)

    def scores(j, s_ref):
        k0 = pl.multiple_of(j * tk, tk)
        for h in range(heads):
            cols = slice(2 * h * HEAD, 2 * (h + 1) * HEAD)
            s_ref[h] = lax.dot_general(q_ref[:, cols], k_ref[pl.ds(k0, tk), cols],
                                       (((1,), (1,)), ((), ())),
                                       preferred_element_type=jnp.float32)

    def softmax_pv(j, s_ref, masked):
        k0 = pl.multiple_of(j * tk, tk)
        for h in range(heads):
            cols = slice(2 * h * HEAD, 2 * (h + 1) * HEAD)
            v = v_ref[pl.ds(k0, tk), cols]
            s = s_ref[h]
            if masked:
                row = lax.broadcasted_iota(jnp.int32, (tq, tk), 0)
                col = lax.broadcasted_iota(jnp.int32, (tq, tk), 1)
                s = jnp.where(col <= row, s, NEG)
            m_prev = m_ref[h]
            m_new = jnp.maximum(m_prev, jnp.max(s, axis=-1, keepdims=True))
            alpha = jnp.exp2(m_prev - m_new)
            p = jnp.exp2(s - jnp.tile(m_new, (1, tk // HEAD)))
            pv = jnp.dot(p.astype(v.dtype), v, preferred_element_type=jnp.float32)
            acc_ref[h] = jnp.tile(alpha, (1, 2)) * acc_ref[h] + pv
            m_ref[h] = m_new

    scores(0, sa_ref)

    def body(t, carry):
        j = 2 * t
        scores(j + 1, sb_ref)
        softmax_pv(j, sa_ref, False)
        scores(j + 2, sa_ref)
        softmax_pv(j + 1, sb_ref, False)
        return carry
    lax.fori_loop(0, i // 2, body, 0)

    @pl.when(i % 2 == 0)
    def _():
        softmax_pv(i, sa_ref, True)

    @pl.when(i % 2 == 1)
    def _():
        scores(i, sb_ref)
        softmax_pv(i - 1, sa_ref, False)
        softmax_pv(i, sb_ref, True)

    for h in range(heads):
        acc = acc_ref[h]
        o_ref[:, h * HEAD:(h + 1) * HEAD] = (acc[:, :HEAD] / acc[:, HEAD:]).astype(o_ref.dtype)


def _mla_attn(q, k, v, *, batch, seq, tq=512, tk=512, heads=4):
    nq = seq // tq
    w = 2 * HEAD * heads
    return pl.pallas_call(
        functools.partial(_mla_attn_kernel, tq=tq, tk=tk, heads=heads),
        grid=(batch, N_HEADS // heads, nq),
        in_specs=[
            pl.BlockSpec((tq, w), lambda b, h, i: (b * nq + i, h)),
            pl.BlockSpec((seq, w), lambda b, h, i: (b, h)),
            pl.BlockSpec((seq, w), lambda b, h, i: (b, h)),
        ],
        out_specs=pl.BlockSpec((tq, HEAD * heads), lambda b, h, i: (b * nq + i, h)),
        out_shape=jax.ShapeDtypeStruct((batch * seq, WIDTH), jnp.bfloat16),
        scratch_shapes=[
            pltpu.VMEM((heads, tq, HEAD), jnp.float32),
            pltpu.VMEM((heads, tq, 2 * HEAD), jnp.float32),
            pltpu.VMEM((heads, tq, tk), jnp.float32),
            pltpu.VMEM((heads, tq, tk), jnp.float32),
        ],
        compiler_params=pltpu.CompilerParams(
            dimension_semantics=("parallel", "parallel", "arbitrary"),
            vmem_limit_bytes=VMEM_LIMIT),
        name="mla_attn",
    )(q, k, v)


def _dil_attn_kernel(q_ref, k_ref, v_ref, cos_ref, sin_ref, gq_ref, gk_ref, o_ref,
                     qd_ref, kd_ref, vd_ref, on_ref, ls_ref, bias_ref, *, seq, unroll):
    chunk = 512
    npat = len(DILATIONS)
    assert DILATIONS == (1, 4, 16)
    nat = [on_ref.at[a] for a in range(3)]
    by4 = [ls_ref.at[a] for a in range(3)]
    dst = [qd_ref, kd_ref, vd_ref]
    lead = [0, QBLK, QBLK]

    qi = lax.broadcasted_iota(jnp.int32, (QBLK, 2 * QBLK), 0)
    kj = lax.broadcasted_iota(jnp.int32, (QBLK, 2 * QBLK), 1)
    band = (kj >= qi) & (kj <= qi + N_BACK)
    bias_ref[0] = jnp.where(band, 0.0, NEG)
    bias_ref[1] = jnp.where(band & (kj >= QBLK), 0.0, NEG)
    ones = jnp.ones((chunk, HEAD), vd_ref.dtype)

    def prep(c, carry):
        r0 = pl.multiple_of(c * chunk, chunk)
        cos = cos_ref[pl.ds(r0, chunk), :]
        sin = sin_ref[pl.ds(r0, chunk), :]
        q = q_ref[pl.ds(r0, chunk), :].astype(jnp.float32)
        q = q * _rms_scale_mxu(q, HEAD) * gq_ref[...]
        q = q * cos + pltpu.roll(q, 64, 1) * sin
        k = k_ref[pl.ds(r0, chunk), :].astype(jnp.float32)
        k = k * _rms_scale_mxu(k, HEAD) * gk_ref[...]
        k = k * cos + pltpu.roll(k, 64, 1) * sin
        v = v_ref[pl.ds(r0, chunk), :]
        nat[0][pl.ds(r0, chunk), :] = q
        nat[1][pl.ds(r0, chunk), :] = k
        nat[2][pl.ds(r0, chunk), :] = v.astype(jnp.float32)
        qd_ref[0, pl.ds(r0, chunk), :] = q.astype(qd_ref.dtype)
        kd_ref[0, pl.ds(QBLK + r0, chunk), :] = k.astype(kd_ref.dtype)
        vd_ref[0, pl.ds(QBLK + r0, chunk), 0:HEAD] = v
        for p in range(npat):
            vd_ref[p, pl.ds(QBLK + r0, chunk), HEAD:] = ones
        return carry
    lax.fori_loop(0, seq // chunk, prep, 0)

    for p in range(npat):
        kd_ref[p, 0:QBLK, :] = jnp.zeros((QBLK, HEAD), kd_ref.dtype)
        vd_ref[p, 0:QBLK, :] = jnp.zeros((QBLK, 2 * HEAD), vd_ref.dtype)

    sub4, sub16 = seq // 4, seq // 16
    for a in range(3):
        for r4 in range(4):
            x = nat[a][pl.ds(r4, sub4, stride=4), :]
            by4[a][r4 * sub4:(r4 + 1) * sub4, :] = x
            dst[a][1, lead[a] + r4 * sub4:lead[a] + (r4 + 1) * sub4, 0:HEAD] = x.astype(dst[a].dtype)
    for a in range(3):
        for r4 in range(4):
            for j in range(4):
                r16 = r4 + 4 * j
                x = by4[a][pl.ds(r4 * sub4 + j, sub16, stride=4), :]
                dst[a][2, lead[a] + r16 * sub16:lead[a] + (r16 + 1) * sub16, 0:HEAD] = x.astype(dst[a].dtype)

    for p, d in enumerate(DILATIONS):
        nb = seq // d // QBLK
        shift = int(math.log2(nb))

        def block(g, carry, p=p, d=d, nb=nb, shift=shift):
            g0 = pl.multiple_of(g * QBLK, QBLK)
            q = qd_ref[p, pl.ds(g0, QBLK), :]
            k = kd_ref[p, pl.ds(g0, 2 * QBLK), :]
            v = vd_ref[p, pl.ds(g0, 2 * QBLK), :]
            n = g & (nb - 1)
            s = lax.dot_general(q, k, (((1,), (1,)), ((), ())), preferred_element_type=jnp.float32)
            s = s + bias_ref[jnp.where(n == 0, 1, 0)]
            m = jnp.max(s, axis=-1, keepdims=True)
            e = jnp.exp2(s - m)
            pv = jnp.dot(e.astype(v.dtype), v, preferred_element_type=jnp.float32)
            den = pv[:, HEAD:]
            r = g >> shift
            start = r + n * (QBLK * d)
            rows = pl.ds(start, QBLK, stride=d) if d > 1 else pl.ds(g0, QBLK)
            on_ref[p, rows, :] = pv[:, :HEAD] / den
            ls_ref[p, rows, :] = m + jnp.log2(den)
            return carry
        lax.fori_loop(0, seq // QBLK, block, 0, unroll=unroll)

    def mix(c, carry):
        r0 = pl.multiple_of(c * chunk, chunk)
        ls = [ls_ref[p, pl.ds(r0, chunk), :] for p in range(npat)]
        mx = functools.reduce(jnp.maximum, ls)
        w = [jnp.exp2(l - mx) for l in ls]
        num = sum(w[p] * on_ref[p, pl.ds(r0, chunk), :] for p in range(npat))
        o_ref[pl.ds(r0, chunk), :] = (num / sum(w)).astype(o_ref.dtype)
        return carry
    lax.fori_loop(0, seq // chunk, mix, 0)


def _dil_attn(proj, cos, sin, gq, gk, *, batch, seq):
    npat = len(DILATIONS)
    col = HEAD

    def qcol(h):
        return jnp.where(h < 4, 1536 // col + h, 4096 // col - 4 + h)

    const = lambda b, h: (0, 0)
    return pl.pallas_call(
        functools.partial(_dil_attn_kernel, seq=seq, unroll=16),
        grid=(batch, N_HEADS),
        in_specs=[
            pl.BlockSpec((seq, HEAD), lambda b, h: (b, qcol(h))),
            pl.BlockSpec((seq, HEAD), lambda b, h: (b, 4608 // col + h)),
            pl.BlockSpec((seq, HEAD), lambda b, h: (b, 5632 // col + h)),
            pl.BlockSpec((seq, HEAD), const),
            pl.BlockSpec((seq, HEAD), const),
            pl.BlockSpec((1, HEAD), const),
            pl.BlockSpec((1, HEAD), const),
        ],
        out_specs=pl.BlockSpec((seq, HEAD), lambda b, h: (b, h)),
        out_shape=jax.ShapeDtypeStruct((batch * seq, WIDTH), jnp.bfloat16),
        scratch_shapes=[
            pltpu.VMEM((npat, seq, HEAD), jnp.bfloat16),
            pltpu.VMEM((npat, seq + QBLK, HEAD), jnp.bfloat16),
            pltpu.VMEM((npat, seq + QBLK, 2 * HEAD), jnp.bfloat16),
            pltpu.VMEM((npat, seq, HEAD), jnp.float32),
            pltpu.VMEM((npat, seq, HEAD), jnp.float32),
            pltpu.VMEM((2, QBLK, 2 * QBLK), jnp.float32),
        ],
        compiler_params=pltpu.CompilerParams(
            dimension_semantics=("parallel", "parallel"), vmem_limit_bytes=VMEM_LIMIT),
        name="dil_attn",
    )(proj, proj, proj, cos, sin, gq, gk)


def _out_proj_kernel(oa_ref, ob_ref, g_ref, x_ref, ga_ref, gb_ref, w_ref, o_ref):
    def branch(o_ref_, gain_ref, gate):
        o = o_ref_[...].astype(jnp.float32)
        y = o * _rms_scale(o, WIDTH) * gain_ref[...]
        return (y * (gate / (1.0 + jnp.exp(-gate)))).astype(jnp.bfloat16)

    ya = branch(oa_ref, ga_ref, g_ref[:, :WIDTH].astype(jnp.float32))
    yb = branch(ob_ref, gb_ref, g_ref[:, WIDTH:].astype(jnp.float32))
    y = jnp.concatenate([ya, yb], axis=-1)
    o_ref[...] = x_ref[...] + jnp.dot(y, w_ref[...], preferred_element_type=jnp.float32)


def _out_proj(o_a, o_b, proj, x2, ga, gb, w, *, tm=512):
    m = x2.shape[0]
    const = lambda i: (0, 0)
    return pl.pallas_call(
        _out_proj_kernel,
        grid=(m // tm,),
        in_specs=[
            pl.BlockSpec((tm, WIDTH), lambda i: (i, 0)),
            pl.BlockSpec((tm, WIDTH), lambda i: (i, 0)),
            pl.BlockSpec((tm, 2 * WIDTH), lambda i: (i, 1)),
            pl.BlockSpec((tm, D_MODEL), lambda i: (i, 0)),
            pl.BlockSpec((1, WIDTH), const),
            pl.BlockSpec((1, WIDTH), const),
            pl.BlockSpec((2 * WIDTH, D_MODEL), const),
        ],
        out_specs=pl.BlockSpec((tm, D_MODEL), lambda i: (i, 0)),
        out_shape=jax.ShapeDtypeStruct((m, D_MODEL), jnp.float32),
        compiler_params=pltpu.CompilerParams(
            dimension_semantics=("parallel",), vmem_limit_bytes=VMEM_LIMIT),
        name="out_proj",
    )(o_a, o_b, proj, x2, ga, gb, w)


def _pad_rope_cols(a):
    z = jnp.zeros(a.shape[:-1] + (ROPE // 2,), a.dtype)
    return jnp.concatenate([a[..., :ROPE // 2], z, a[..., ROPE // 2:], z], axis=-1)


def _rope_tables(seq, d, padded):
    inv = ROPE_THETA ** (-np.arange(0, d, 2, dtype=np.float64) / d)
    ang = np.arange(seq, dtype=np.float64)[:, None] * inv[None, :]
    cos, sin = np.cos(ang), np.sin(ang)
    cos2 = np.concatenate([cos, cos], axis=-1)
    sin2 = np.concatenate([-sin, sin], axis=-1)
    if padded:
        z = np.zeros((seq, d // 2))
        cos2 = np.concatenate([cos, z, cos, z], axis=-1)
        sin2 = np.concatenate([-sin, z, sin, z], axis=-1)
    return jnp.asarray(cos2, jnp.float32), jnp.asarray(sin2, jnp.float32)


def kernel(x, norm_gain, w_in, q_a_norm_gain, kv_a_norm_gain, w_uq, w_ukv, mla_q_norm_gain,
           mla_k_norm_gain, dil_q_norm_gain, dil_k_norm_gain, mla_out_norm_gain,
           dil_out_norm_gain, w_out):
    batch, seq, _ = x.shape
    depth = w_in.shape[0]
    bf = jnp.bfloat16
    cos_d, sin_d = _rope_tables(seq, HEAD, padded=False)
    cos_m, sin_m = _rope_tables(seq, ROPE, padded=True)
    h2 = x.reshape(batch * seq, D_MODEL)

    for l in range(depth):
        w_r = _w_prep(w_in[l])
        wq = w_uq[l].reshape(Q_RANK, N_HEADS, MLA_QK)
        wq_r = jnp.concatenate([wq[..., :HEAD], _pad_rope_cols(wq[..., HEAD:])], axis=-1)
        wq_r = wq_r.reshape(Q_RANK, N_HEADS * 2 * HEAD).astype(bf)
        wkv = w_ukv[l].reshape(KV_RANK, N_HEADS, 2 * HEAD)
        wkv_r = jnp.concatenate([wkv[..., :HEAD].reshape(KV_RANK, WIDTH),
                                 wkv[..., HEAD:].reshape(KV_RANK, WIDTH)], axis=-1).astype(bf)

        gq_m = mla_q_norm_gain[l]
        gk_m = mla_k_norm_gain[l]
        gq_full = jnp.concatenate([gq_m[:HEAD], _pad_rope_cols(gq_m[HEAD:])])[None, :]
        gq_full = gq_full * (math.log2(math.e) / math.sqrt(MLA_QK))
        gk_full = jnp.concatenate([gk_m[:HEAD], _pad_rope_cols(gk_m[HEAD:])])[None, :]
        gq_d = dil_q_norm_gain[l][None, :] * (math.log2(math.e) / math.sqrt(HEAD))
        gk_d = dil_k_norm_gain[l][None, :]

        proj = _in_proj(h2, norm_gain[l][None, :], w_r)
        q_f, k_f, v_a = _mla_prep(proj, wq_r, wkv_r, q_a_norm_gain[l][None, :],
                                  kv_a_norm_gain[l][None, :], gq_full, gk_full, cos_m, sin_m, seq=seq)
        o_a = _mla_attn(q_f, k_f, v_a, batch=batch, seq=seq)
        o_b = _dil_attn(proj, cos_d, sin_d, gq_d, gk_d, batch=batch, seq=seq)
        h2 = _out_proj(o_a, o_b, proj, h2, mla_out_norm_gain[l][None, :],
                       dil_out_norm_gain[l][None, :], w_out[l].astype(bf))
    return h2.reshape(batch, seq, D_MODEL)
```

```python
import functools
import math

import jax
import jax.numpy as jnp
import numpy as np
from jax import lax
from jax.experimental import pallas as pl
from jax.experimental.pallas import tpu as pltpu

EPS = 1e-6
ROPE_THETA = 10000.0
NEG = -1e30

D_MODEL = 2048
N_HEADS = 8
HEAD = 128
ROPE = 64
Q_RANK = 768
KV_RANK = 512
WIDTH = N_HEADS * HEAD
MLA_QK = HEAD + ROPE
QBLK = 128
N_BACK = 128
DILATIONS = (1, 4, 16)

PROJ_COLS = 6656
LAT_COLS = 1536
VMEM_LIMIT = 56 * 1024 * 1024


def _rms_scale(x, n):
    return lax.rsqrt(jnp.sum(x * x, axis=-1, keepdims=True) * (1.0 / n) + EPS)


def _rms_scale_mxu(x, n):
    ones = jnp.ones((x.shape[-1], x.shape[-1]), jnp.bfloat16)
    ss = jnp.dot((x * x).astype(jnp.bfloat16), ones, preferred_element_type=jnp.float32)
    return lax.rsqrt(ss * (1.0 / n) + EPS)


_O_CQ, _O_CKV, _O_KR = 0, Q_RANK, Q_RANK + KV_RANK
_O_GA = _O_KR + ROPE
_O_QB = _O_GA + WIDTH
_O_KB, _O_VB, _O_GB = _O_QB + WIDTH, _O_QB + 2 * WIDTH, _O_QB + 3 * WIDTH
_IN_COLS = _O_GB + WIDTH


W_TILE = 512
_W_TILE_SRC = (_O_CKV, _O_CQ, _O_CQ + 256, _O_QB, _O_GA, _O_GA + 512, _O_GB, _O_GB + 512,
               _O_QB + 512, _O_KB, _O_KB + 512, _O_VB, _O_VB + 512)


def _w_prep_kernel(src_ref, w_ref, kr_ref, o_ref):
    del src_ref
    o_ref[...] = w_ref[...].astype(o_ref.dtype)

    @pl.when(pl.program_id(0) == 1)
    def _():
        half = ROPE // 2
        zeros = jnp.zeros((half, D_MODEL), o_ref.dtype)
        o_ref[2 * HEAD:, :] = w_ref[:2 * HEAD, :].astype(o_ref.dtype)
        o_ref[0:half, :] = kr_ref[0:half, :].astype(o_ref.dtype)
        o_ref[half:2 * half, :] = zeros
        o_ref[2 * half:3 * half, :] = kr_ref[half:, :].astype(o_ref.dtype)
        o_ref[3 * half:4 * half, :] = zeros
        o_ref[HEAD:2 * HEAD, :] = jnp.zeros((HEAD, D_MODEL), o_ref.dtype)


def _w_prep(wt):
    assert all(s % ROPE == 0 for s in _W_TILE_SRC)
    src = jnp.asarray([s // ROPE for s in _W_TILE_SRC], jnp.int32)
    return pl.pallas_call(
        _w_prep_kernel,
        grid_spec=pltpu.PrefetchScalarGridSpec(
            num_scalar_prefetch=1,
            grid=(PROJ_COLS // W_TILE,),
            in_specs=[
                pl.BlockSpec((pl.Element(W_TILE), pl.Element(D_MODEL)), lambda t, src: (src[t] * ROPE, 0)),
                pl.BlockSpec((pl.Element(ROPE), pl.Element(D_MODEL)), lambda t, src: (_O_KR, 0)),
            ],
            out_specs=pl.BlockSpec((W_TILE, D_MODEL), lambda t, src: (t, 0)),
        ),
        out_shape=jax.ShapeDtypeStruct((PROJ_COLS, D_MODEL), jnp.bfloat16),
        compiler_params=pltpu.CompilerParams(
            dimension_semantics=("arbitrary",), vmem_limit_bytes=VMEM_LIMIT),
        name="w_prep",
    )(src, wt, wt)


def _in_proj_kernel(x_ref, g_ref, w_ref, o_ref, hn_ref, *, row_chunk):
    @pl.when(pl.program_id(1) == 0)
    def _():
        def body(c, carry):
            r0 = pl.multiple_of(c * row_chunk, row_chunk)
            x = x_ref[pl.ds(r0, row_chunk), :]
            hn = x * _rms_scale(x, D_MODEL) * g_ref[...]
            hn_ref[pl.ds(r0, row_chunk), :] = hn.astype(hn_ref.dtype)
            return carry
        lax.fori_loop(0, x_ref.shape[0] // row_chunk, body, 0)

    o_ref[...] = lax.dot_general(hn_ref[...], w_ref[...], (((1,), (1,)), ((), ())),
                                 preferred_element_type=jnp.float32).astype(o_ref.dtype)


def _in_proj(x2, gain, w, *, tm=2048, tn=512):
    m = x2.shape[0]
    return pl.pallas_call(
        functools.partial(_in_proj_kernel, row_chunk=64),
        grid=(m // tm, PROJ_COLS // tn),
        in_specs=[
            pl.BlockSpec((tm, D_MODEL), lambda i, j: (i, 0)),
            pl.BlockSpec((1, D_MODEL), lambda i, j: (0, 0)),
            pl.BlockSpec((tn, D_MODEL), lambda i, j: (j, 0)),
        ],
        out_specs=pl.BlockSpec((tm, tn), lambda i, j: (i, j)),
        out_shape=jax.ShapeDtypeStruct((m, PROJ_COLS), jnp.bfloat16),
        scratch_shapes=[pltpu.VMEM((tm, D_MODEL), jnp.bfloat16)],
        compiler_params=pltpu.CompilerParams(
            dimension_semantics=("parallel", "arbitrary"),
            vmem_limit_bytes=VMEM_LIMIT),
        name="in_proj",
    )(x2, gain, w)


def _rope_pad(x, cos, sin):
    return x * cos + pltpu.roll(x, 64, 1) * sin


def _mla_prep_kernel(lat_ref, wq_ref, wkv_ref, gqa_ref, gkva_ref, gq_ref, gk_ref,
                     cos_ref, sin_ref, q_ref, k_ref, v_ref):
    cos = cos_ref[...]
    sin = sin_ref[...]
    gq = gq_ref[...]
    gk = gk_ref[...]

    c_kv = lat_ref[:, 0:KV_RANK].astype(jnp.float32)
    cn = (c_kv * _rms_scale(c_kv, KV_RANK) * gkva_ref[...]).astype(jnp.bfloat16)
    kv = jnp.dot(cn, wkv_ref[...], preferred_element_type=jnp.float32)
    ones = jnp.ones((kv.shape[0], HEAD), v_ref.dtype)
    for h in range(N_HEADS):
        v_ref[:, 2 * h * HEAD:(2 * h + 1) * HEAD] = kv[:, WIDTH + h * HEAD:WIDTH + (h + 1) * HEAD].astype(v_ref.dtype)
        v_ref[:, (2 * h + 1) * HEAD:(2 * h + 2) * HEAD] = ones

    k_r = lat_ref[:, KV_RANK:KV_RANK + HEAD].astype(jnp.float32)
    k_r = k_r * _rms_scale(k_r, ROPE) * gk[:, HEAD:]
    k_rope = _rope_pad(k_r, cos, sin).astype(k_ref.dtype)
    for h in range(N_HEADS):
        kn = kv[:, h * HEAD:(h + 1) * HEAD]
        kn = kn * _rms_scale(kn, HEAD) * gk[:, :HEAD]
        k_ref[:, 2 * h * HEAD:(2 * h + 1) * HEAD] = kn.astype(k_ref.dtype)
        k_ref[:, (2 * h + 1) * HEAD:(2 * h + 2) * HEAD] = k_rope

    c_q = lat_ref[:, 2 * HEAD + KV_RANK:].astype(jnp.float32)
    cqn = (c_q * _rms_scale(c_q, Q_RANK) * gqa_ref[...]).astype(jnp.bfloat16)
    q = jnp.dot(cqn, wq_ref[...], preferred_element_type=jnp.float32)
    for h in range(N_HEADS):
        qn = q[:, 2 * h * HEAD:(2 * h + 1) * HEAD]
        qn = qn * _rms_scale(qn, HEAD) * gq[:, :HEAD]
        q_ref[:, 2 * h * HEAD:(2 * h + 1) * HEAD] = qn.astype(q_ref.dtype)
        qr = q[:, (2 * h + 1) * HEAD:(2 * h + 2) * HEAD]
        qr = qr * _rms_scale(qr, ROPE) * gq[:, HEAD:]
        q_ref[:, (2 * h + 1) * HEAD:(2 * h + 2) * HEAD] = _rope_pad(qr, cos, sin).astype(q_ref.dtype)


def _mla_prep(proj, wq, wkv, gqa, gkva, gq, gk, cos, sin, *, seq, tm=256):
    m = proj.shape[0]
    nseq = seq // tm
    const = lambda i: (0, 0)
    return pl.pallas_call(
        _mla_prep_kernel,
        grid=(m // tm,),
        in_specs=[
            pl.BlockSpec((tm, LAT_COLS), lambda i: (i, 0)),
            pl.BlockSpec(wq.shape, const),
            pl.BlockSpec(wkv.shape, const),
            pl.BlockSpec(gqa.shape, const),
            pl.BlockSpec(gkva.shape, const),
            pl.BlockSpec(gq.shape, const),
            pl.BlockSpec(gk.shape, const),
            pl.BlockSpec((tm, HEAD), lambda i: (i % nseq, 0)),
            pl.BlockSpec((tm, HEAD), lambda i: (i % nseq, 0)),
        ],
        out_specs=[
            pl.BlockSpec((tm, 2 * WIDTH), lambda i: (i, 0)),
            pl.BlockSpec((tm, 2 * WIDTH), lambda i: (i, 0)),
            pl.BlockSpec((tm, 2 * WIDTH), lambda i: (i, 0)),
        ],
        out_shape=[
            jax.ShapeDtypeStruct((m, 2 * WIDTH), jnp.bfloat16),
            jax.ShapeDtypeStruct((m, 2 * WIDTH), jnp.bfloat16),
            jax.ShapeDtypeStruct((m, 2 * WIDTH), jnp.bfloat16),
        ],
        compiler_params=pltpu.CompilerParams(
            dimension_semantics=("parallel",), vmem_limit_bytes=VMEM_LIMIT),
        name="mla_prep",
    )(proj, wq, wkv, gqa, gkva, gq, gk, cos, sin)


def _mla_attn_kernel(q_ref, k_ref, v_ref, o_ref, m_ref, acc_ref, sa_ref, sb_ref, *, tq, tk, heads):
    assert tq == tk
    i = pl.program_id(2)
    m_ref[...] = jnp.full(m_ref.shape, NEG, jnp.float32)
    acc_ref[...] = jnp.zeros(acc_ref.shape, jnp.float32)

    def scores(j, s_ref):
        k0 = pl.multiple_of(j * tk, tk)
        for h in range(heads):
            cols = slice(2 * h * HEAD, 2 * (h + 1) * HEAD)
            s_ref[h] = lax.dot_general(q_ref[:, cols], k_ref[pl.ds(k0, tk), cols],
                                       (((1,), (1,)), ((), ())),
                                       preferred_element_type=jnp.float32)

    def softmax_pv(j, s_ref, masked):
        k0 = pl.multiple_of(j * tk, tk)
        for h in range(heads):
            cols = slice(2 * h * HEAD, 2 * (h + 1) * HEAD)
            v = v_ref[pl.ds(k0, tk), cols]
            s = s_ref[h]
            if masked:
                row = lax.broadcasted_iota(jnp.int32, (tq, tk), 0)
                col = lax.broadcasted_iota(jnp.int32, (tq, tk), 1)
                s = jnp.where(col <= row, s, NEG)
            m_prev = m_ref[h]
            m_new = jnp.maximum(m_prev, jnp.max(s, axis=-1, keepdims=True))
            alpha = jnp.exp2(m_prev - m_new)
            p = jnp.exp2(s - jnp.tile(m_new, (1, tk // HEAD)))
            pv = jnp.dot(p.astype(v.dtype), v, preferred_element_type=jnp.float32)
            acc_ref[h] = jnp.tile(alpha, (1, 2)) * acc_ref[h] + pv
            m_ref[h] = m_new

    scores(0, sa_ref)

    def body(t, carry):
        j = 2 * t
        scores(j + 1, sb_ref)
        softmax_pv(j, sa_ref, False)
        scores(j + 2, sa_ref)
        softmax_pv(j + 1, sb_ref, False)
        return carry
    lax.fori_loop(0, i // 2, body, 0)

    @pl.when(i % 2 == 0)
    def _():
        softmax_pv(i, sa_ref, True)

    @pl.when(i % 2 == 1)
    def _():
        scores(i, sb_ref)
        softmax_pv(i - 1, sa_ref, False)
        softmax_pv(i, sb_ref, True)

    for h in range(heads):
        acc = acc_ref[h]
        o_ref[:, h * HEAD:(h + 1) * HEAD] = (acc[:, :HEAD] / acc[:, HEAD:]).astype(o_ref.dtype)


def _mla_attn(q, k, v, *, batch, seq, tq=512, tk=512, heads=4):
    nq = seq // tq
    w = 2 * HEAD * heads
    return pl.pallas_call(
        functools.partial(_mla_attn_kernel, tq=tq, tk=tk, heads=heads),
        grid=(batch, N_HEADS // heads, nq),
        in_specs=[
            pl.BlockSpec((tq, w), lambda b, h, i: (b * nq + i, h)),
            pl.BlockSpec((seq, w), lambda b, h, i: (b, h)),
            pl.BlockSpec((seq, w), lambda b, h, i: (b, h)),
        ],
        out_specs=pl.BlockSpec((tq, HEAD * heads), lambda b, h, i: (b * nq + i, h)),
        out_shape=jax.ShapeDtypeStruct((batch * seq, WIDTH), jnp.bfloat16),
        scratch_shapes=[
            pltpu.VMEM((heads, tq, HEAD), jnp.float32),
            pltpu.VMEM((heads, tq, 2 * HEAD), jnp.float32),
            pltpu.VMEM((heads, tq, tk), jnp.float32),
            pltpu.VMEM((heads, tq, tk), jnp.float32),
        ],
        compiler_params=pltpu.CompilerParams(
            dimension_semantics=("parallel", "parallel", "arbitrary"),
            vmem_limit_bytes=VMEM_LIMIT),
        name="mla_attn",
    )(q, k, v)


def _dil_attn_kernel(q_ref, k_ref, v_ref, cos_ref, sin_ref, gq_ref, gk_ref, o_ref,
                     qd_ref, kd_ref, vd_ref, on_ref, ls_ref, bias_ref, *, seq, unroll):
    chunk = 512
    npat = len(DILATIONS)
    assert DILATIONS == (1, 4, 16)
    nat = [on_ref.at[a] for a in range(3)]
    by4 = [ls_ref.at[a] for a in range(3)]
    dst = [qd_ref, kd_ref, vd_ref]
    lead = [0, QBLK, QBLK]

    qi = lax.broadcasted_iota(jnp.int32, (QBLK, 2 * QBLK), 0)
    kj = lax.broadcasted_iota(jnp.int32, (QBLK, 2 * QBLK), 1)
    band = (kj >= qi) & (kj <= qi + N_BACK)
    bias_ref[0] = jnp.where(band, 0.0, NEG)
    bias_ref[1] = jnp.where(band & (kj >= QBLK), 0.0, NEG)
    ones = jnp.ones((chunk, HEAD), vd_ref.dtype)

    def prep(c, carry):
        r0 = pl.multiple_of(c * chunk, chunk)
        cos = cos_ref[pl.ds(r0, chunk), :]
        sin = sin_ref[pl.ds(r0, chunk), :]
        q = q_ref[pl.ds(r0, chunk), :].astype(jnp.float32)
        q = q * _rms_scale_mxu(q, HEAD) * gq_ref[...]
        q = q * cos + pltpu.roll(q, 64, 1) * sin
        k = k_ref[pl.ds(r0, chunk), :].astype(jnp.float32)
        k = k * _rms_scale_mxu(k, HEAD) * gk_ref[...]
        k = k * cos + pltpu.roll(k, 64, 1) * sin
        v = v_ref[pl.ds(r0, chunk), :]
        nat[0][pl.ds(r0, chunk), :] = q
        nat[1][pl.ds(r0, chunk), :] = k
        nat[2][pl.ds(r0, chunk), :] = v.astype(jnp.float32)
        qd_ref[0, pl.ds(r0, chunk), :] = q.astype(qd_ref.dtype)
        kd_ref[0, pl.ds(QBLK + r0, chunk), :] = k.astype(kd_ref.dtype)
        vd_ref[0, pl.ds(QBLK + r0, chunk), 0:HEAD] = v
        for p in range(npat):
            vd_ref[p, pl.ds(QBLK + r0, chunk), HEAD:] = ones
        return carry
    lax.fori_loop(0, seq // chunk, prep, 0)

    for p in range(npat):
        kd_ref[p, 0:QBLK, :] = jnp.zeros((QBLK, HEAD), kd_ref.dtype)
        vd_ref[p, 0:QBLK, :] = jnp.zeros((QBLK, 2 * HEAD), vd_ref.dtype)

    sub4, sub16 = seq // 4, seq // 16
    for a in range(3):
        for r4 in range(4):
            x = nat[a][pl.ds(r4, sub4, stride=4), :]
            by4[a][r4 * sub4:(r4 + 1) * sub4, :] = x
            dst[a][1, lead[a] + r4 * sub4:lead[a] + (r4 + 1) * sub4, 0:HEAD] = x.astype(dst[a].dtype)
    for a in range(3):
        for r4 in range(4):
            for j in range(4):
                r16 = r4 + 4 * j
                x = by4[a][pl.ds(r4 * sub4 + j, sub16, stride=4), :]
                dst[a][2, lead[a] + r16 * sub16:lead[a] + (r16 + 1) * sub16, 0:HEAD] = x.astype(dst[a].dtype)

    for p, d in enumerate(DILATIONS):
        nb = seq // d // QBLK
        shift = int(math.log2(nb))

        def block(g, carry, p=p, d=d, nb=nb, shift=shift):
            g0 = pl.multiple_of(g * QBLK, QBLK)
            q = qd_ref[p, pl.ds(g0, QBLK), :]
            k = kd_ref[p, pl.ds(g0, 2 * QBLK), :]
            v = vd_ref[p, pl.ds(g0, 2 * QBLK), :]
            n = g & (nb - 1)
            s = lax.dot_general(q, k, (((1,), (1,)), ((), ())), preferred_element_type=jnp.float32)
            s = s + bias_ref[jnp.where(n == 0, 1, 0)]
            m = jnp.max(s, axis=-1, keepdims=True)
            e = jnp.exp2(s - m)
            pv = jnp.dot(e.astype(v.dtype), v, preferred_element_type=jnp.float32)
            den = pv[:, HEAD:]
            r = g >> shift
            start = r + n * (QBLK * d)
            rows = pl.ds(start, QBLK, stride=d) if d > 1 else pl.ds(g0, QBLK)
            on_ref[p, rows, :] = pv[:, :HEAD] / den
            ls_ref[p, rows, :] = m + jnp.log2(den)
            return carry
        lax.fori_loop(0, seq // QBLK, block, 0, unroll=unroll)

    def mix(c, carry):
        r0 = pl.multiple_of(c * chunk, chunk)
        ls = [ls_ref[p, pl.ds(r0, chunk), :] for p in range(npat)]
        mx = functools.reduce(jnp.maximum, ls)
        w = [jnp.exp2(l - mx) for l in ls]
        num = sum(w[p] * on_ref[p, pl.ds(r0, chunk), :] for p in range(npat))
        o_ref[pl.ds(r0, chunk), :] = (num / sum(w)).astype(o_ref.dtype)
        return carry
    lax.fori_loop(0, seq // chunk, mix, 0)


def _dil_attn(proj, cos, sin, gq, gk, *, batch, seq):
    npat = len(DILATIONS)
    col = HEAD

    def qcol(h):
        return jnp.where(h < 4, 1536 // col + h, 4096 // col - 4 + h)

    const = lambda b, h: (0, 0)
    return pl.pallas_call(
        functools.partial(_dil_attn_kernel, seq=seq, unroll=16),
        grid=(batch, N_HEADS),
        in_specs=[
            pl.BlockSpec((seq, HEAD), lambda b, h: (b, qcol(h))),
            pl.BlockSpec((seq, HEAD), lambda b, h: (b, 4608 // col + h)),
            pl.BlockSpec((seq, HEAD), lambda b, h: (b, 5632 // col + h)),
            pl.BlockSpec((seq, HEAD), const),
            pl.BlockSpec((seq, HEAD), const),
            pl.BlockSpec((1, HEAD), const),
            pl.BlockSpec((1, HEAD), const),
        ],
        out_specs=pl.BlockSpec((seq, HEAD), lambda b, h: (b, h)),
        out_shape=jax.ShapeDtypeStruct((batch * seq, WIDTH), jnp.bfloat16),
        scratch_shapes=[
            pltpu.VMEM((npat, seq, HEAD), jnp.bfloat16),
            pltpu.VMEM((npat, seq + QBLK, HEAD), jnp.bfloat16),
            pltpu.VMEM((npat, seq + QBLK, 2 * HEAD), jnp.bfloat16),
            pltpu.VMEM((npat, seq, HEAD), jnp.float32),
            pltpu.VMEM((npat, seq, HEAD), jnp.float32),
            pltpu.VMEM((2, QBLK, 2 * QBLK), jnp.float32),
        ],
        compiler_params=pltpu.CompilerParams(
            dimension_semantics=("parallel", "parallel"), vmem_limit_bytes=VMEM_LIMIT),
        name="dil_attn",
    )(proj, proj, proj, cos, sin, gq, gk)


def _out_proj_kernel(oa_ref, ob_ref, g_ref, x_ref, ga_ref, gb_ref, w_ref, o_ref):
    def branch(o_ref_, gain_ref, gate):
        o = o_ref_[...].astype(jnp.float32)
        y = o * _rms_scale(o, WIDTH) * gain_ref[...]
        return (y * (gate / (1.0 + jnp.exp(-gate)))).astype(jnp.bfloat16)

    ya = branch(oa_ref, ga_ref, g_ref[:, :WIDTH].astype(jnp.float32))
    yb = branch(ob_ref, gb_ref, g_ref[:, WIDTH:].astype(jnp.float32))
    y = jnp.concatenate([ya, yb], axis=-1)
    o_ref[...] = x_ref[...] + jnp.dot(y, w_ref[...], preferred_element_type=jnp.float32)


def _out_proj(o_a, o_b, proj, x2, ga, gb, w, *, tm=512):
    m = x2.shape[0]
    const = lambda i: (0, 0)
    return pl.pallas_call(
        _out_proj_kernel,
        grid=(m // tm,),
        in_specs=[
            pl.BlockSpec((tm, WIDTH), lambda i: (i, 0)),
            pl.BlockSpec((tm, WIDTH), lambda i: (i, 0)),
            pl.BlockSpec((tm, 2 * WIDTH), lambda i: (i, 1)),
            pl.BlockSpec((tm, D_MODEL), lambda i: (i, 0)),
            pl.BlockSpec((1, WIDTH), const),
            pl.BlockSpec((1, WIDTH), const),
            pl.BlockSpec((2 * WIDTH, D_MODEL), const),
        ],
        out_specs=pl.BlockSpec((tm, D_MODEL), lambda i: (i, 0)),
        out_shape=jax.ShapeDtypeStruct((m, D_MODEL), jnp.float32),
        compiler_params=pltpu.CompilerParams(
            dimension_semantics=("parallel",), vmem_limit_bytes=VMEM_LIMIT),
        name="out_proj",
    )(o_a, o_b, proj, x2, ga, gb, w)


def _pad_rope_cols(a):
    z = jnp.zeros(a.shape[:-1] + (ROPE // 2,), a.dtype)
    return jnp.concatenate([a[..., :ROPE // 2], z, a[..., ROPE // 2:], z], axis=-1)


def _rope_tables(seq, d, padded):
    inv = ROPE_THETA ** (-np.arange(0, d, 2, dtype=np.float64) / d)
    ang = np.arange(seq, dtype=np.float64)[:, None] * inv[None, :]
    cos, sin = np.cos(ang), np.sin(ang)
    cos2 = np.concatenate([cos, cos], axis=-1)
    sin2 = np.concatenate([-sin, sin], axis=-1)
    if padded:
        z = np.zeros((seq, d // 2))
        cos2 = np.concatenate([cos, z, cos, z], axis=-1)
        sin2 = np.concatenate([-sin, z, sin, z], axis=-1)
    return jnp.asarray(cos2, jnp.float32), jnp.asarray(sin2, jnp.float32)


def kernel(x, norm_gain, w_in, q_a_norm_gain, kv_a_norm_gain, w_uq, w_ukv, mla_q_norm_gain,
           mla_k_norm_gain, dil_q_norm_gain, dil_k_norm_gain, mla_out_norm_gain,
           dil_out_norm_gain, w_out):
    batch, seq, _ = x.shape
    depth = w_in.shape[0]
    bf = jnp.bfloat16
    cos_d, sin_d = _rope_tables(seq, HEAD, padded=False)
    cos_m, sin_m = _rope_tables(seq, ROPE, padded=True)
    h2 = x.reshape(batch * seq, D_MODEL)

    for l in range(depth):
        w_r = _w_prep(jnp.swapaxes(w_in[l], 0, 1))
        wq = w_uq[l].reshape(Q_RANK, N_HEADS, MLA_QK)
        wq_r = jnp.concatenate([wq[..., :HEAD], _pad_rope_cols(wq[..., HEAD:])], axis=-1)
        wq_r = wq_r.reshape(Q_RANK, N_HEADS * 2 * HEAD).astype(bf)
        wkv = w_ukv[l].reshape(KV_RANK, N_HEADS, 2 * HEAD)
        wkv_r = jnp.concatenate([wkv[..., :HEAD].reshape(KV_RANK, WIDTH),
                                 wkv[..., HEAD:].reshape(KV_RANK, WIDTH)], axis=-1).astype(bf)

        gq_m = mla_q_norm_gain[l]
        gk_m = mla_k_norm_gain[l]
        gq_full = jnp.concatenate([gq_m[:HEAD], _pad_rope_cols(gq_m[HEAD:])])[None, :]
        gq_full = gq_full * (math.log2(math.e) / math.sqrt(MLA_QK))
        gk_full = jnp.concatenate([gk_m[:HEAD], _pad_rope_cols(gk_m[HEAD:])])[None, :]
        gq_d = dil_q_norm_gain[l][None, :] * (math.log2(math.e) / math.sqrt(HEAD))
        gk_d = dil_k_norm_gain[l][None, :]

        proj = _in_proj(h2, norm_gain[l][None, :], w_r)
        q_f, k_f, v_a = _mla_prep(proj, wq_r, wkv_r, q_a_norm_gain[l][None, :],
                                  kv_a_norm_gain[l][None, :], gq_full, gk_full, cos_m, sin_m, seq=seq)
        o_a = _mla_attn(q_f, k_f, v_a, batch=batch, seq=seq)
        o_b = _dil_attn(proj, cos_d, sin_d, gq_d, gk_d, batch=batch, seq=seq)
        h2 = _out_proj(o_a, o_b, proj, h2, mla_out_norm_gain[l][None, :],
                       dil_out_norm_gain[l][None, :], w_out[l].astype(bf))
    return h2.reshape(batch, seq, D_MODEL)
```

```python
import functools
import math

import jax
import jax.numpy as jnp
import numpy as np
from jax import lax
from jax.experimental import pallas as pl
from jax.experimental.pallas import tpu as pltpu

EPS = 1e-6
ROPE_THETA = 10000.0
NEG = -1e30

D_MODEL = 2048
N_HEADS = 8
HEAD = 128
ROPE = 64
Q_RANK = 768
KV_RANK = 512
WIDTH = N_HEADS * HEAD
MLA_QK = HEAD + ROPE
QBLK = 128
N_BACK = 128
DILATIONS = (1, 4, 16)

PROJ_COLS = 6656
LAT_COLS = 1536
VMEM_LIMIT = 56 * 1024 * 1024


def _rms_scale(x, n):
    return lax.rsqrt(jnp.sum(x * x, axis=-1, keepdims=True) * (1.0 / n) + EPS)


def _rms_scale_mxu(x, n):
    ones = jnp.ones((x.shape[-1], x.shape[-1]), jnp.bfloat16)
    ss = jnp.dot((x * x).astype(jnp.bfloat16), ones, preferred_element_type=jnp.float32)
    return lax.rsqrt(ss * (1.0 / n) + EPS)


_O_CQ, _O_CKV, _O_KR = 0, Q_RANK, Q_RANK + KV_RANK
_O_GA = _O_KR + ROPE
_O_QB = _O_GA + WIDTH
_O_KB, _O_VB, _O_GB = _O_QB + WIDTH, _O_QB + 2 * WIDTH, _O_QB + 3 * WIDTH
_IN_COLS = _O_GB + WIDTH


W_TILE = 512
_W_TILE_SRC = (_O_CKV, _O_CQ, _O_CQ + 256, _O_QB, _O_GA, _O_GA + 512, _O_GB, _O_GB + 512,
               _O_QB + 512, _O_KB, _O_KB + 512, _O_VB, _O_VB + 512)


def _w_prep_kernel(src_ref, w_ref, kr_ref, o_ref):
    del src_ref
    o_ref[...] = w_ref[...].astype(o_ref.dtype)

    @pl.when(pl.program_id(0) == 1)
    def _():
        half = ROPE // 2
        zeros = jnp.zeros((half, D_MODEL), o_ref.dtype)
        o_ref[2 * HEAD:, :] = w_ref[:2 * HEAD, :].astype(o_ref.dtype)
        o_ref[0:half, :] = kr_ref[0:half, :].astype(o_ref.dtype)
        o_ref[half:2 * half, :] = zeros
        o_ref[2 * half:3 * half, :] = kr_ref[half:, :].astype(o_ref.dtype)
        o_ref[3 * half:4 * half, :] = zeros
        o_ref[HEAD:2 * HEAD, :] = jnp.zeros((HEAD, D_MODEL), o_ref.dtype)


def _w_prep(wt):
    assert all(s % ROPE == 0 for s in _W_TILE_SRC)
    src = jnp.asarray([s // ROPE for s in _W_TILE_SRC], jnp.int32)
    return pl.pallas_call(
        _w_prep_kernel,
        grid_spec=pltpu.PrefetchScalarGridSpec(
            num_scalar_prefetch=1,
            grid=(PROJ_COLS // W_TILE,),
            in_specs=[
                pl.BlockSpec((pl.Element(W_TILE), pl.Element(D_MODEL)), lambda t, src: (src[t] * ROPE, 0)),
                pl.BlockSpec((pl.Element(ROPE), pl.Element(D_MODEL)), lambda t, src: (_O_KR, 0)),
            ],
            out_specs=pl.BlockSpec((W_TILE, D_MODEL), lambda t, src: (t, 0)),
        ),
        out_shape=jax.ShapeDtypeStruct((PROJ_COLS, D_MODEL), jnp.bfloat16),
        compiler_params=pltpu.CompilerParams(
            dimension_semantics=("arbitrary",), vmem_limit_bytes=VMEM_LIMIT),
        name="w_prep",
    )(src, wt, wt)


def _in_proj_kernel(x_ref, g_ref, w_ref, o_ref, hn_ref, *, row_chunk):
    @pl.when(pl.program_id(1) == 0)
    def _():
        def body(c, carry):
            r0 = pl.multiple_of(c * row_chunk, row_chunk)
            x = x_ref[pl.ds(r0, row_chunk), :]
            hn = x * _rms_scale(x, D_MODEL) * g_ref[...]
            hn_ref[pl.ds(r0, row_chunk), :] = hn.astype(hn_ref.dtype)
            return carry
        lax.fori_loop(0, x_ref.shape[0] // row_chunk, body, 0, unroll=4)

    o_ref[...] = lax.dot_general(hn_ref[...], w_ref[...], (((1,), (1,)), ((), ())),
                                 preferred_element_type=jnp.float32).astype(o_ref.dtype)


def _in_proj(x2, gain, w, *, tm=2048, tn=512):
    m = x2.shape[0]
    return pl.pallas_call(
        functools.partial(_in_proj_kernel, row_chunk=64),
        grid=(m // tm, PROJ_COLS // tn),
        in_specs=[
            pl.BlockSpec((tm, D_MODEL), lambda i, j: (i, 0)),
            pl.BlockSpec((1, D_MODEL), lambda i, j: (0, 0)),
            pl.BlockSpec((tn, D_MODEL), lambda i, j: (j, 0)),
        ],
        out_specs=pl.BlockSpec((tm, tn), lambda i, j: (i, j)),
        out_shape=jax.ShapeDtypeStruct((m, PROJ_COLS), jnp.bfloat16),
        scratch_shapes=[pltpu.VMEM((tm, D_MODEL), jnp.bfloat16)],
        compiler_params=pltpu.CompilerParams(
            dimension_semantics=("parallel", "arbitrary"),
            vmem_limit_bytes=VMEM_LIMIT),
        name="in_proj",
    )(x2, gain, w)


def _rope_pad(x, cos, sin):
    return x * cos + pltpu.roll(x, 64, 1) * sin


def _mla_prep_kernel(lat_ref, wq_ref, wkv_ref, gqa_ref, gkva_ref, gq_ref, gk_ref,
                     cos_ref, sin_ref, q_ref, k_ref, v_ref):
    cos = cos_ref[...]
    sin = sin_ref[...]
    gq = gq_ref[...]
    gk = gk_ref[...]

    c_kv = lat_ref[:, 0:KV_RANK].astype(jnp.float32)
    cn = (c_kv * _rms_scale(c_kv, KV_RANK) * gkva_ref[...]).astype(jnp.bfloat16)
    kv = jnp.dot(cn, wkv_ref[...], preferred_element_type=jnp.float32)
    ones = jnp.ones((kv.shape[0], HEAD), v_ref.dtype)
    for h in range(N_HEADS):
        v_ref[:, 2 * h * HEAD:(2 * h + 1) * HEAD] = kv[:, WIDTH + h * HEAD:WIDTH + (h + 1) * HEAD].astype(v_ref.dtype)
        v_ref[:, (2 * h + 1) * HEAD:(2 * h + 2) * HEAD] = ones

    k_r = lat_ref[:, KV_RANK:KV_RANK + HEAD].astype(jnp.float32)
    k_r = k_r * _rms_scale(k_r, ROPE) * gk[:, HEAD:]
    k_rope = _rope_pad(k_r, cos, sin).astype(k_ref.dtype)
    for h in range(N_HEADS):
        kn = kv[:, h * HEAD:(h + 1) * HEAD]
        kn = kn * _rms_scale(kn, HEAD) * gk[:, :HEAD]
        k_ref[:, 2 * h * HEAD:(2 * h + 1) * HEAD] = kn.astype(k_ref.dtype)
        k_ref[:, (2 * h + 1) * HEAD:(2 * h + 2) * HEAD] = k_rope

    c_q = lat_ref[:, 2 * HEAD + KV_RANK:].astype(jnp.float32)
    cqn = (c_q * _rms_scale(c_q, Q_RANK) * gqa_ref[...]).astype(jnp.bfloat16)
    q = jnp.dot(cqn, wq_ref[...], preferred_element_type=jnp.float32)
    for h in range(N_HEADS):
        qn = q[:, 2 * h * HEAD:(2 * h + 1) * HEAD]
        qn = qn * _rms_scale(qn, HEAD) * gq[:, :HEAD]
        q_ref[:, 2 * h * HEAD:(2 * h + 1) * HEAD] = qn.astype(q_ref.dtype)
        qr = q[:, (2 * h + 1) * HEAD:(2 * h + 2) * HEAD]
        qr = qr * _rms_scale(qr, ROPE) * gq[:, HEAD:]
        q_ref[:, (2 * h + 1) * HEAD:(2 * h + 2) * HEAD] = _rope_pad(qr, cos, sin).astype(q_ref.dtype)


def _mla_prep(proj, wq, wkv, gqa, gkva, gq, gk, cos, sin, *, seq, tm=256):
    m = proj.shape[0]
    nseq = seq // tm
    const = lambda i: (0, 0)
    return pl.pallas_call(
        _mla_prep_kernel,
        grid=(m // tm,),
        in_specs=[
            pl.BlockSpec((tm, LAT_COLS), lambda i: (i, 0)),
            pl.BlockSpec(wq.shape, const),
            pl.BlockSpec(wkv.shape, const),
            pl.BlockSpec(gqa.shape, const),
            pl.BlockSpec(gkva.shape, const),
            pl.BlockSpec(gq.shape, const),
            pl.BlockSpec(gk.shape, const),
            pl.BlockSpec((tm, HEAD), lambda i: (i % nseq, 0)),
            pl.BlockSpec((tm, HEAD), lambda i: (i % nseq, 0)),
        ],
        out_specs=[
            pl.BlockSpec((tm, 2 * WIDTH), lambda i: (i, 0)),
            pl.BlockSpec((tm, 2 * WIDTH), lambda i: (i, 0)),
            pl.BlockSpec((tm, 2 * WIDTH), lambda i: (i, 0)),
        ],
        out_shape=[
            jax.ShapeDtypeStruct((m, 2 * WIDTH), jnp.bfloat16),
            jax.ShapeDtypeStruct((m, 2 * WIDTH), jnp.bfloat16),
            jax.ShapeDtypeStruct((m, 2 * WIDTH), jnp.bfloat16),
        ],
        compiler_params=pltpu.CompilerParams(
            dimension_semantics=("parallel",), vmem_limit_bytes=VMEM_LIMIT),
        name="mla_prep",
    )(proj, wq, wkv, gqa, gkva, gq, gk, cos, sin)


def _mla_attn_kernel(q_ref, k_ref, v_ref, o_ref, m_ref, acc_ref, sa_ref, sb_ref, *, tq, tk, heads):
    assert tq == tk
    i = pl.program_id(2)
    m_ref[...] = jnp.full(m_ref.shape, NEG, jnp.float32)
    acc_ref[...] = jnp.zeros(acc_ref.shape, jnp.float32)

    def scores(j, s_ref):
        k0 = pl.multiple_of(j * tk, tk)
        for h in range(heads):
            cols = slice(2 * h * HEAD, 2 * (h + 1) * HEAD)
            s_ref[h] = lax.dot_general(q_ref[:, cols], k_ref[pl.ds(k0, tk), cols],
                                       (((1,), (1,)), ((), ())),
                                       preferred_element_type=jnp.float32)

    def softmax_pv(j, s_ref, masked):
        k0 = pl.multiple_of(j * tk, tk)
        for h in range(heads):
            cols = slice(2 * h * HEAD, 2 * (h + 1) * HEAD)
            v = v_ref[pl.ds(k0, tk), cols]
            s = s_ref[h]
            if masked:
                row = lax.broadcasted_iota(jnp.int32, (tq, tk), 0)
                col = lax.broadcasted_iota(jnp.int32, (tq, tk), 1)
                s = jnp.where(col <= row, s, NEG)
            m_prev = m_ref[h]
            m_new = jnp.maximum(m_prev, jnp.max(s, axis=-1, keepdims=True))
            alpha = jnp.exp2(m_prev - m_new)
            p = jnp.exp2(s - jnp.tile(m_new, (1, tk // HEAD)))
            pv = jnp.dot(p.astype(v.dtype), v, preferred_element_type=jnp.float32)
            acc_ref[h] = jnp.tile(alpha, (1, 2)) * acc_ref[h] + pv
            m_ref[h] = m_new

    scores(0, sa_ref)

    def body(t, carry):
        j = 2 * t
        scores(j + 1, sb_ref)
        softmax_pv(j, sa_ref, False)
        scores(j + 2, sa_ref)
        softmax_pv(j + 1, sb_ref, False)
        return carry
    lax.fori_loop(0, i // 2, body, 0)

    @pl.when(i % 2 == 0)
    def _():
        softmax_pv(i, sa_ref, True)

    @pl.when(i % 2 == 1)
    def _():
        scores(i, sb_ref)
        softmax_pv(i - 1, sa_ref, False)
        softmax_pv(i, sb_ref, True)

    for h in range(heads):
        acc = acc_ref[h]
        o_ref[:, h * HEAD:(h + 1) * HEAD] = (acc[:, :HEAD] / acc[:, HEAD:]).astype(o_ref.dtype)


def _mla_attn(q, k, v, *, batch, seq, tq=512, tk=512, heads=4):
    nq = seq // tq
    w = 2 * HEAD * heads
    return pl.pallas_call(
        functools.partial(_mla_attn_kernel, tq=tq, tk=tk, heads=heads),
        grid=(batch, N_HEADS // heads, nq),
        in_specs=[
            pl.BlockSpec((tq, w), lambda b, h, i: (b * nq + i, h)),
            pl.BlockSpec((seq, w), lambda b, h, i: (b, h)),
            pl.BlockSpec((seq, w), lambda b, h, i: (b, h)),
        ],
        out_specs=pl.BlockSpec((tq, HEAD * heads), lambda b, h, i: (b * nq + i, h)),
        out_shape=jax.ShapeDtypeStruct((batch * seq, WIDTH), jnp.bfloat16),
        scratch_shapes=[
            pltpu.VMEM((heads, tq, HEAD), jnp.float32),
            pltpu.VMEM((heads, tq, 2 * HEAD), jnp.float32),
            pltpu.VMEM((heads, tq, tk), jnp.float32),
            pltpu.VMEM((heads, tq, tk), jnp.float32),
        ],
        compiler_params=pltpu.CompilerParams(
            dimension_semantics=("parallel", "parallel", "arbitrary"),
            vmem_limit_bytes=VMEM_LIMIT),
        name="mla_attn",
    )(q, k, v)


def _dil_attn_kernel(q_ref, k_ref, v_ref, cos_ref, sin_ref, gq_ref, gk_ref, o_ref,
                     qd_ref, kd_ref, vd_ref, on_ref, ls_ref, bias_ref, *, seq, unroll):
    chunk = 1024
    npat = len(DILATIONS)
    assert DILATIONS == (1, 4, 16)
    nat = [on_ref.at[a] for a in range(3)]
    by4 = [ls_ref.at[a] for a in range(3)]
    dst = [qd_ref, kd_ref, vd_ref]
    lead = [0, QBLK, QBLK]

    qi = lax.broadcasted_iota(jnp.int32, (QBLK, 2 * QBLK), 0)
    kj = lax.broadcasted_iota(jnp.int32, (QBLK, 2 * QBLK), 1)
    band = (kj >= qi) & (kj <= qi + N_BACK)
    bias_ref[0] = jnp.where(band, 0.0, NEG)
    bias_ref[1] = jnp.where(band & (kj >= QBLK), 0.0, NEG)
    ones = jnp.ones((chunk, HEAD), vd_ref.dtype)

    def prep(c, carry):
        r0 = pl.multiple_of(c * chunk, chunk)
        cos = cos_ref[pl.ds(r0, chunk), :]
        sin = sin_ref[pl.ds(r0, chunk), :]
        q = q_ref[pl.ds(r0, chunk), :].astype(jnp.float32)
        q = q * _rms_scale_mxu(q, HEAD) * gq_ref[...]
        q = q * cos + pltpu.roll(q, 64, 1) * sin
        k = k_ref[pl.ds(r0, chunk), :].astype(jnp.float32)
        k = k * _rms_scale_mxu(k, HEAD) * gk_ref[...]
        k = k * cos + pltpu.roll(k, 64, 1) * sin
        v = v_ref[pl.ds(r0, chunk), :]
        nat[0][pl.ds(r0, chunk), :] = q
        nat[1][pl.ds(r0, chunk), :] = k
        nat[2][pl.ds(r0, chunk), :] = v.astype(jnp.float32)
        qd_ref[0, pl.ds(r0, chunk), :] = q.astype(qd_ref.dtype)
        kd_ref[0, pl.ds(QBLK + r0, chunk), :] = k.astype(kd_ref.dtype)
        vd_ref[0, pl.ds(QBLK + r0, chunk), 0:HEAD] = v
        for p in range(npat):
            vd_ref[p, pl.ds(QBLK + r0, chunk), HEAD:] = ones
        return carry
    lax.fori_loop(0, seq // chunk, prep, 0)

    for p in range(npat):
        kd_ref[p, 0:QBLK, :] = jnp.zeros((QBLK, HEAD), kd_ref.dtype)
        vd_ref[p, 0:QBLK, :] = jnp.zeros((QBLK, 2 * HEAD), vd_ref.dtype)

    sub4, sub16 = seq // 4, seq // 16
    for a in range(3):
        for r4 in range(4):
            x = nat[a][pl.ds(r4, sub4, stride=4), :]
            by4[a][r4 * sub4:(r4 + 1) * sub4, :] = x
            dst[a][1, lead[a] + r4 * sub4:lead[a] + (r4 + 1) * sub4, 0:HEAD] = x.astype(dst[a].dtype)
    for a in range(3):
        for r4 in range(4):
            for j in range(4):
                r16 = r4 + 4 * j
                x = by4[a][pl.ds(r4 * sub4 + j, sub16, stride=4), :]
                dst[a][2, lead[a] + r16 * sub16:lead[a] + (r16 + 1) * sub16, 0:HEAD] = x.astype(dst[a].dtype)

    for p, d in enumerate(DILATIONS):
        nb = seq // d // QBLK
        shift = int(math.log2(nb))

        def block(g, carry, p=p, d=d, nb=nb, shift=shift):
            g0 = pl.multiple_of(g * QBLK, QBLK)
            q = qd_ref[p, pl.ds(g0, QBLK), :]
            k = kd_ref[p, pl.ds(g0, 2 * QBLK), :]
            v = vd_ref[p, pl.ds(g0, 2 * QBLK), :]
            n = g & (nb - 1)
            s = lax.dot_general(q, k, (((1,), (1,)), ((), ())), preferred_element_type=jnp.float32)
            s = s + bias_ref[jnp.where(n == 0, 1, 0)]
            m = jnp.max(s, axis=-1, keepdims=True)
            e = jnp.exp2(s - m)
            pv = jnp.dot(e.astype(v.dtype), v, preferred_element_type=jnp.float32)
            den = pv[:, HEAD:]
            r = g >> shift
            start = r + n * (QBLK * d)
            rows = pl.ds(start, QBLK, stride=d) if d > 1 else pl.ds(g0, QBLK)
            on_ref[p, rows, :] = pv[:, :HEAD] / den
            ls_ref[p, rows, :] = m + jnp.log2(den)
            return carry
        lax.fori_loop(0, seq // QBLK, block, 0, unroll=unroll)

    def mix(c, carry):
        r0 = pl.multiple_of(c * chunk, chunk)
        ls = [ls_ref[p, pl.ds(r0, chunk), :] for p in range(npat)]
        mx = functools.reduce(jnp.maximum, ls)
        w = [jnp.exp2(l - mx) for l in ls]
        num = sum(w[p] * on_ref[p, pl.ds(r0, chunk), :] for p in range(npat))
        o_ref[pl.ds(r0, chunk), :] = (num / sum(w)).astype(o_ref.dtype)
        return carry
    lax.fori_loop(0, seq // chunk, mix, 0)


def _dil_attn(proj, cos, sin, gq, gk, *, batch, seq):
    npat = len(DILATIONS)
    col = HEAD

    def qcol(h):
        return jnp.where(h < 4, 1536 // col + h, 4096 // col - 4 + h)

    const = lambda b, h: (0, 0)
    return pl.pallas_call(
        functools.partial(_dil_attn_kernel, seq=seq, unroll=16),
        grid=(batch, N_HEADS),
        in_specs=[
            pl.BlockSpec((seq, HEAD), lambda b, h: (b, qcol(h))),
            pl.BlockSpec((seq, HEAD), lambda b, h: (b, 4608 // col + h)),
            pl.BlockSpec((seq, HEAD), lambda b, h: (b, 5632 // col + h)),
            pl.BlockSpec((seq, HEAD), const),
            pl.BlockSpec((seq, HEAD), const),
            pl.BlockSpec((1, HEAD), const),
            pl.BlockSpec((1, HEAD), const),
        ],
        out_specs=pl.BlockSpec((seq, HEAD), lambda b, h: (b, h)),
        out_shape=jax.ShapeDtypeStruct((batch * seq, WIDTH), jnp.bfloat16),
        scratch_shapes=[
            pltpu.VMEM((npat, seq, HEAD), jnp.bfloat16),
            pltpu.VMEM((npat, seq + QBLK, HEAD), jnp.bfloat16),
            pltpu.VMEM((npat, seq + QBLK, 2 * HEAD), jnp.bfloat16),
            pltpu.VMEM((npat, seq, HEAD), jnp.float32),
            pltpu.VMEM((npat, seq, HEAD), jnp.float32),
            pltpu.VMEM((2, QBLK, 2 * QBLK), jnp.float32),
        ],
        compiler_params=pltpu.CompilerParams(
            dimension_semantics=("parallel", "parallel"), vmem_limit_bytes=VMEM_LIMIT),
        name="dil_attn",
    )(proj, proj, proj, cos, sin, gq, gk)


def _out_proj_kernel(oa_ref, ob_ref, g_ref, x_ref, ga_ref, gb_ref, w_ref, o_ref):
    def branch(o_ref_, gain_ref, gate):
        o = o_ref_[...].astype(jnp.float32)
        y = o * _rms_scale(o, WIDTH) * gain_ref[...]
        return (y * (gate / (1.0 + jnp.exp(-gate)))).astype(jnp.bfloat16)

    ya = branch(oa_ref, ga_ref, g_ref[:, :WIDTH].astype(jnp.float32))
    yb = branch(ob_ref, gb_ref, g_ref[:, WIDTH:].astype(jnp.float32))
    y = jnp.concatenate([ya, yb], axis=-1)
    o_ref[...] = x_ref[...] + jnp.dot(y, w_ref[...], preferred_element_type=jnp.float32)


def _out_proj(o_a, o_b, proj, x2, ga, gb, w, *, tm=512):
    m = x2.shape[0]
    const = lambda i: (0, 0)
    return pl.pallas_call(
        _out_proj_kernel,
        grid=(m // tm,),
        in_specs=[
            pl.BlockSpec((tm, WIDTH), lambda i: (i, 0)),
            pl.BlockSpec((tm, WIDTH), lambda i: (i, 0)),
            pl.BlockSpec((tm, 2 * WIDTH), lambda i: (i, 1)),
            pl.BlockSpec((tm, D_MODEL), lambda i: (i, 0)),
            pl.BlockSpec((1, WIDTH), const),
            pl.BlockSpec((1, WIDTH), const),
            pl.BlockSpec((2 * WIDTH, D_MODEL), const),
        ],
        out_specs=pl.BlockSpec((tm, D_MODEL), lambda i: (i, 0)),
        out_shape=jax.ShapeDtypeStruct((m, D_MODEL), jnp.float32),
        compiler_params=pltpu.CompilerParams(
            dimension_semantics=("parallel",), vmem_limit_bytes=VMEM_LIMIT),
        name="out_proj",
    )(o_a, o_b, proj, x2, ga, gb, w)


def _pad_rope_cols(a):
    z = jnp.zeros(a.shape[:-1] + (ROPE // 2,), a.dtype)
    return jnp.concatenate([a[..., :ROPE // 2], z, a[..., ROPE // 2:], z], axis=-1)


def _rope_tables(seq, d, padded):
    inv = ROPE_THETA ** (-np.arange(0, d, 2, dtype=np.float64) / d)
    ang = np.arange(seq, dtype=np.float64)[:, None] * inv[None, :]
    cos, sin = np.cos(ang), np.sin(ang)
    cos2 = np.concatenate([cos, cos], axis=-1)
    sin2 = np.concatenate([-sin, sin], axis=-1)
    if padded:
        z = np.zeros((seq, d // 2))
        cos2 = np.concatenate([cos, z, cos, z], axis=-1)
        sin2 = np.concatenate([-sin, z, sin, z], axis=-1)
    return jnp.asarray(cos2, jnp.float32), jnp.asarray(sin2, jnp.float32)


def kernel(x, norm_gain, w_in, q_a_norm_gain, kv_a_norm_gain, w_uq, w_ukv, mla_q_norm_gain,
           mla_k_norm_gain, dil_q_norm_gain, dil_k_norm_gain, mla_out_norm_gain,
           dil_out_norm_gain, w_out):
    batch, seq, _ = x.shape
    depth = w_in.shape[0]
    bf = jnp.bfloat16
    cos_d, sin_d = _rope_tables(seq, HEAD, padded=False)
    cos_m, sin_m = _rope_tables(seq, ROPE, padded=True)
    h2 = x.reshape(batch * seq, D_MODEL)

    for l in range(depth):
        w_r = _w_prep(jnp.swapaxes(w_in[l], 0, 1))
        wq = w_uq[l].reshape(Q_RANK, N_HEADS, MLA_QK)
        wq_r = jnp.concatenate([wq[..., :HEAD], _pad_rope_cols(wq[..., HEAD:])], axis=-1)
        wq_r = wq_r.reshape(Q_RANK, N_HEADS * 2 * HEAD).astype(bf)
        wkv = w_ukv[l].reshape(KV_RANK, N_HEADS, 2 * HEAD)
        wkv_r = jnp.concatenate([wkv[..., :HEAD].reshape(KV_RANK, WIDTH),
                                 wkv[..., HEAD:].reshape(KV_RANK, WIDTH)], axis=-1).astype(bf)

        gq_m = mla_q_norm_gain[l]
        gk_m = mla_k_norm_gain[l]
        gq_full = jnp.concatenate([gq_m[:HEAD], _pad_rope_cols(gq_m[HEAD:])])[None, :]
        gq_full = gq_full * (math.log2(math.e) / math.sqrt(MLA_QK))
        gk_full = jnp.concatenate([gk_m[:HEAD], _pad_rope_cols(gk_m[HEAD:])])[None, :]
        gq_d = dil_q_norm_gain[l][None, :] * (math.log2(math.e) / math.sqrt(HEAD))
        gk_d = dil_k_norm_gain[l][None, :]

        proj = _in_proj(h2, norm_gain[l][None, :], w_r)
        q_f, k_f, v_a = _mla_prep(proj, wq_r, wkv_r, q_a_norm_gain[l][None, :],
                                  kv_a_norm_gain[l][None, :], gq_full, gk_full, cos_m, sin_m, seq=seq)
        o_a = _mla_attn(q_f, k_f, v_a, batch=batch, seq=seq)
        o_b = _dil_attn(proj, cos_d, sin_d, gq_d, gk_d, batch=batch, seq=seq)
        h2 = _out_proj(o_a, o_b, proj, h2, mla_out_norm_gain[l][None, :],
                       dil_out_norm_gain[l][None, :], w_out[l].astype(bf))
    return h2.reshape(batch, seq, D_MODEL)
```

```python
import functools
import math

import jax
import jax.numpy as jnp
import numpy as np
from jax import lax
from jax.experimental import pallas as pl
from jax.experimental.pallas import tpu as pltpu

EPS = 1e-6
ROPE_THETA = 10000.0
NEG = -1e30

D_MODEL = 2048
N_HEADS = 8
HEAD = 128
ROPE = 64
Q_RANK = 768
KV_RANK = 512
WIDTH = N_HEADS * HEAD
MLA_QK = HEAD + ROPE
QBLK = 128
N_BACK = 128
DILATIONS = (1, 4, 16)

PROJ_COLS = 6656
LAT_COLS = 1536
VMEM_LIMIT = 56 * 1024 * 1024


def _rms_scale(x, n):
    return lax.rsqrt(jnp.sum(x * x, axis=-1, keepdims=True) * (1.0 / n) + EPS)


def _rms_scale_mxu(x, n):
    ones = jnp.ones((x.shape[-1], x.shape[-1]), jnp.bfloat16)
    ss = jnp.dot((x * x).astype(jnp.bfloat16), ones, preferred_element_type=jnp.float32)
    return lax.rsqrt(ss * (1.0 / n) + EPS)


_O_CQ, _O_CKV, _O_KR = 0, Q_RANK, Q_RANK + KV_RANK
_O_GA = _O_KR + ROPE
_O_QB = _O_GA + WIDTH
_O_KB, _O_VB, _O_GB = _O_QB + WIDTH, _O_QB + 2 * WIDTH, _O_QB + 3 * WIDTH
_IN_COLS = _O_GB + WIDTH


W_TILE = 512
_W_TILE_SRC = (_O_CKV, _O_CQ, _O_CQ + 256, _O_QB, _O_GA, _O_GA + 512, _O_GB, _O_GB + 512,
               _O_QB + 512, _O_KB, _O_KB + 512, _O_VB, _O_VB + 512)


def _w_prep_kernel(src_ref, w_ref, kr_ref, o_ref):
    del src_ref
    o_ref[...] = w_ref[...].astype(o_ref.dtype)

    @pl.when(pl.program_id(0) == 1)
    def _():
        half = ROPE // 2
        zeros = jnp.zeros((half, D_MODEL), o_ref.dtype)
        o_ref[2 * HEAD:, :] = w_ref[:2 * HEAD, :].astype(o_ref.dtype)
        o_ref[0:half, :] = kr_ref[0:half, :].astype(o_ref.dtype)
        o_ref[half:2 * half, :] = zeros
        o_ref[2 * half:3 * half, :] = kr_ref[half:, :].astype(o_ref.dtype)
        o_ref[3 * half:4 * half, :] = zeros
        o_ref[HEAD:2 * HEAD, :] = jnp.zeros((HEAD, D_MODEL), o_ref.dtype)


def _w_prep(wt):
    assert all(s % ROPE == 0 for s in _W_TILE_SRC)
    src = jnp.asarray([s // ROPE for s in _W_TILE_SRC], jnp.int32)
    return pl.pallas_call(
        _w_prep_kernel,
        grid_spec=pltpu.PrefetchScalarGridSpec(
            num_scalar_prefetch=1,
            grid=(PROJ_COLS // W_TILE,),
            in_specs=[
                pl.BlockSpec((pl.Element(W_TILE), pl.Element(D_MODEL)), lambda t, src: (src[t] * ROPE, 0)),
                pl.BlockSpec((pl.Element(ROPE), pl.Element(D_MODEL)), lambda t, src: (_O_KR, 0)),
            ],
            out_specs=pl.BlockSpec((W_TILE, D_MODEL), lambda t, src: (t, 0)),
        ),
        out_shape=jax.ShapeDtypeStruct((PROJ_COLS, D_MODEL), jnp.bfloat16),
        compiler_params=pltpu.CompilerParams(
            dimension_semantics=("arbitrary",), vmem_limit_bytes=VMEM_LIMIT),
        name="w_prep",
    )(src, wt, wt)


def _in_proj_kernel(x_ref, g_ref, w_ref, o_ref, hn_ref, *, row_chunk):
    @pl.when(pl.program_id(1) == 0)
    def _():
        def body(c, carry):
            r0 = pl.multiple_of(c * row_chunk, row_chunk)
            x = x_ref[pl.ds(r0, row_chunk), :]
            hn = x * _rms_scale(x, D_MODEL) * g_ref[...]
            hn_ref[pl.ds(r0, row_chunk), :] = hn.astype(hn_ref.dtype)
            return carry
        lax.fori_loop(0, x_ref.shape[0] // row_chunk, body, 0, unroll=4)

    half = o_ref.shape[0] // 2
    for r in (0, half):
        o_ref[r:r + half, :] = lax.dot_general(hn_ref[r:r + half, :], w_ref[...], (((1,), (1,)), ((), ())),
                                               preferred_element_type=jnp.float32).astype(o_ref.dtype)


def _in_proj(x2, gain, w, *, tm=2048, tn=512):
    m = x2.shape[0]
    return pl.pallas_call(
        functools.partial(_in_proj_kernel, row_chunk=64),
        grid=(m // tm, PROJ_COLS // tn),
        in_specs=[
            pl.BlockSpec((tm, D_MODEL), lambda i, j: (i, 0)),
            pl.BlockSpec((1, D_MODEL), lambda i, j: (0, 0)),
            pl.BlockSpec((tn, D_MODEL), lambda i, j: (j, 0)),
        ],
        out_specs=pl.BlockSpec((tm, tn), lambda i, j: (i, j)),
        out_shape=jax.ShapeDtypeStruct((m, PROJ_COLS), jnp.bfloat16),
        scratch_shapes=[pltpu.VMEM((tm, D_MODEL), jnp.bfloat16)],
        compiler_params=pltpu.CompilerParams(
            dimension_semantics=("parallel", "arbitrary"),
            vmem_limit_bytes=VMEM_LIMIT),
        name="in_proj",
    )(x2, gain, w)


def _rope_pad(x, cos, sin):
    return x * cos + pltpu.roll(x, 64, 1) * sin


def _mla_prep_kernel(lat_ref, wq_ref, wkv_ref, gqa_ref, gkva_ref, gq_ref, gk_ref,
                     cos_ref, sin_ref, q_ref, k_ref, v_ref):
    cos = cos_ref[...]
    sin = sin_ref[...]
    gq = gq_ref[...]
    gk = gk_ref[...]

    c_kv = lat_ref[:, 0:KV_RANK].astype(jnp.float32)
    cn = (c_kv * _rms_scale(c_kv, KV_RANK) * gkva_ref[...]).astype(jnp.bfloat16)
    kv = jnp.dot(cn, wkv_ref[...], preferred_element_type=jnp.float32)
    ones = jnp.ones((kv.shape[0], HEAD), v_ref.dtype)
    for h in range(N_HEADS):
        v_ref[:, 2 * h * HEAD:(2 * h + 1) * HEAD] = kv[:, WIDTH + h * HEAD:WIDTH + (h + 1) * HEAD].astype(v_ref.dtype)
        v_ref[:, (2 * h + 1) * HEAD:(2 * h + 2) * HEAD] = ones

    k_r = lat_ref[:, KV_RANK:KV_RANK + HEAD].astype(jnp.float32)
    k_r = k_r * _rms_scale(k_r, ROPE) * gk[:, HEAD:]
    k_rope = _rope_pad(k_r, cos, sin).astype(k_ref.dtype)
    for h in range(N_HEADS):
        kn = kv[:, h * HEAD:(h + 1) * HEAD]
        kn = kn * _rms_scale(kn, HEAD) * gk[:, :HEAD]
        k_ref[:, 2 * h * HEAD:(2 * h + 1) * HEAD] = kn.astype(k_ref.dtype)
        k_ref[:, (2 * h + 1) * HEAD:(2 * h + 2) * HEAD] = k_rope

    c_q = lat_ref[:, 2 * HEAD + KV_RANK:].astype(jnp.float32)
    cqn = (c_q * _rms_scale(c_q, Q_RANK) * gqa_ref[...]).astype(jnp.bfloat16)
    q = jnp.dot(cqn, wq_ref[...], preferred_element_type=jnp.float32)
    for h in range(N_HEADS):
        qn = q[:, 2 * h * HEAD:(2 * h + 1) * HEAD]
        qn = qn * _rms_scale(qn, HEAD) * gq[:, :HEAD]
        q_ref[:, 2 * h * HEAD:(2 * h + 1) * HEAD] = qn.astype(q_ref.dtype)
        qr = q[:, (2 * h + 1) * HEAD:(2 * h + 2) * HEAD]
        qr = qr * _rms_scale(qr, ROPE) * gq[:, HEAD:]
        q_ref[:, (2 * h + 1) * HEAD:(2 * h + 2) * HEAD] = _rope_pad(qr, cos, sin).astype(q_ref.dtype)


def _mla_prep(proj, wq, wkv, gqa, gkva, gq, gk, cos, sin, *, seq, tm=256):
    m = proj.shape[0]
    nseq = seq // tm
    const = lambda i: (0, 0)
    return pl.pallas_call(
        _mla_prep_kernel,
        grid=(m // tm,),
        in_specs=[
            pl.BlockSpec((tm, LAT_COLS), lambda i: (i, 0)),
            pl.BlockSpec(wq.shape, const),
            pl.BlockSpec(wkv.shape, const),
            pl.BlockSpec(gqa.shape, const),
            pl.BlockSpec(gkva.shape, const),
            pl.BlockSpec(gq.shape, const),
            pl.BlockSpec(gk.shape, const),
            pl.BlockSpec((tm, HEAD), lambda i: (i % nseq, 0)),
            pl.BlockSpec((tm, HEAD), lambda i: (i % nseq, 0)),
        ],
        out_specs=[
            pl.BlockSpec((tm, 2 * WIDTH), lambda i: (i, 0)),
            pl.BlockSpec((tm, 2 * WIDTH), lambda i: (i, 0)),
            pl.BlockSpec((tm, 2 * WIDTH), lambda i: (i, 0)),
        ],
        out_shape=[
            jax.ShapeDtypeStruct((m, 2 * WIDTH), jnp.bfloat16),
            jax.ShapeDtypeStruct((m, 2 * WIDTH), jnp.bfloat16),
            jax.ShapeDtypeStruct((m, 2 * WIDTH), jnp.bfloat16),
        ],
        compiler_params=pltpu.CompilerParams(
            dimension_semantics=("parallel",), vmem_limit_bytes=VMEM_LIMIT),
        name="mla_prep",
    )(proj, wq, wkv, gqa, gkva, gq, gk, cos, sin)


def _mla_attn_kernel(q_ref, k_ref, v_ref, o_ref, m_ref, acc_ref, sa_ref, sb_ref, *, tq, tk, heads):
    assert tq == tk
    i = pl.program_id(2)
    m_ref[...] = jnp.full(m_ref.shape, NEG, jnp.float32)
    acc_ref[...] = jnp.zeros(acc_ref.shape, jnp.float32)

    def scores(j, s_ref):
        k0 = pl.multiple_of(j * tk, tk)
        for h in range(heads):
            cols = slice(2 * h * HEAD, 2 * (h + 1) * HEAD)
            s_ref[h] = lax.dot_general(q_ref[:, cols], k_ref[pl.ds(k0, tk), cols],
                                       (((1,), (1,)), ((), ())),
                                       preferred_element_type=jnp.float32)

    def update(h, rows, s, v):
        m_prev = m_ref[h, rows, :]
        m_new = jnp.maximum(m_prev, jnp.max(s, axis=-1, keepdims=True))
        alpha = jnp.exp2(m_prev - m_new)
        p = jnp.exp2(s - jnp.tile(m_new, (1, s.shape[1] // HEAD)))
        pv = jnp.dot(p.astype(v.dtype), v, preferred_element_type=jnp.float32)
        acc_ref[h, rows, :] = jnp.tile(alpha, (1, 2)) * acc_ref[h, rows, :] + pv
        m_ref[h, rows, :] = m_new

    def softmax_pv(j, s_ref, masked):
        k0 = pl.multiple_of(j * tk, tk)
        hq = tq // 2
        for h in range(heads):
            cols = slice(2 * h * HEAD, 2 * (h + 1) * HEAD)
            if not masked:
                update(h, slice(None), s_ref[h], v_ref[pl.ds(k0, tk), cols])
                continue
            row = lax.broadcasted_iota(jnp.int32, (hq, hq), 0)
            col = lax.broadcasted_iota(jnp.int32, (hq, hq), 1)
            tri = col <= row
            s0 = jnp.where(tri, s_ref[h, 0:hq, 0:hq], NEG)
            update(h, slice(0, hq), s0, v_ref[pl.ds(k0, hq), cols])
            s1 = jnp.concatenate([s_ref[h, hq:, 0:hq], jnp.where(tri, s_ref[h, hq:, hq:], NEG)], axis=1)
            update(h, slice(hq, tq), s1, v_ref[pl.ds(k0, tk), cols])

    scores(0, sa_ref)

    def body(t, carry):
        j = 2 * t
        scores(j + 1, sb_ref)
        softmax_pv(j, sa_ref, False)
        scores(j + 2, sa_ref)
        softmax_pv(j + 1, sb_ref, False)
        return carry
    lax.fori_loop(0, i // 2, body, 0)

    @pl.when(i % 2 == 0)
    def _():
        softmax_pv(i, sa_ref, True)

    @pl.when(i % 2 == 1)
    def _():
        scores(i, sb_ref)
        softmax_pv(i - 1, sa_ref, False)
        softmax_pv(i, sb_ref, True)

    for h in range(heads):
        acc = acc_ref[h]
        o_ref[:, h * HEAD:(h + 1) * HEAD] = (acc[:, :HEAD] / acc[:, HEAD:]).astype(o_ref.dtype)


def _mla_attn(q, k, v, *, batch, seq, tq=512, tk=512, heads=4):
    nq = seq // tq
    w = 2 * HEAD * heads
    return pl.pallas_call(
        functools.partial(_mla_attn_kernel, tq=tq, tk=tk, heads=heads),
        grid=(batch, N_HEADS // heads, nq),
        in_specs=[
            pl.BlockSpec((tq, w), lambda b, h, i: (b * nq + i, h)),
            pl.BlockSpec((seq, w), lambda b, h, i: (b, h)),
            pl.BlockSpec((seq, w), lambda b, h, i: (b, h)),
        ],
        out_specs=pl.BlockSpec((tq, HEAD * heads), lambda b, h, i: (b * nq + i, h)),
        out_shape=jax.ShapeDtypeStruct((batch * seq, WIDTH), jnp.bfloat16),
        scratch_shapes=[
            pltpu.VMEM((heads, tq, HEAD), jnp.float32),
            pltpu.VMEM((heads, tq, 2 * HEAD), jnp.float32),
            pltpu.VMEM((heads, tq, tk), jnp.float32),
            pltpu.VMEM((heads, tq, tk), jnp.float32),
        ],
        compiler_params=pltpu.CompilerParams(
            dimension_semantics=("parallel", "parallel", "arbitrary"),
            vmem_limit_bytes=VMEM_LIMIT),
        name="mla_attn",
    )(q, k, v)


def _dil_attn_kernel(q_ref, k_ref, v_ref, cos_ref, sin_ref, gq_ref, gk_ref, o_ref,
                     qd_ref, kd_ref, vd_ref, on_ref, ls_ref, bias_ref, *, seq, unroll):
    chunk = 1024
    npat = len(DILATIONS)
    assert DILATIONS == (1, 4, 16)
    nat = [on_ref.at[a] for a in range(3)]
    by4 = [ls_ref.at[a] for a in range(3)]
    dst = [qd_ref, kd_ref, vd_ref]
    lead = [0, QBLK, QBLK]

    @pl.when((pl.program_id(0) == 0) & (pl.program_id(1) == 0))
    def _():
        qi = lax.broadcasted_iota(jnp.int32, (QBLK, 2 * QBLK), 0)
        kj = lax.broadcasted_iota(jnp.int32, (QBLK, 2 * QBLK), 1)
        band = (kj >= qi) & (kj <= qi + N_BACK)
        bias_ref[0] = jnp.where(band, 0.0, NEG)
        bias_ref[1] = jnp.where(band & (kj >= QBLK), 0.0, NEG)
        for p in range(npat):
            vd_ref[p, QBLK:, HEAD:] = jnp.ones((seq, HEAD), vd_ref.dtype)
            kd_ref[p, 0:QBLK, :] = jnp.zeros((QBLK, HEAD), kd_ref.dtype)
            vd_ref[p, 0:QBLK, :] = jnp.zeros((QBLK, 2 * HEAD), vd_ref.dtype)

    def prep(c, carry):
        r0 = pl.multiple_of(c * chunk, chunk)
        cos = cos_ref[pl.ds(r0, chunk), :]
        sin = sin_ref[pl.ds(r0, chunk), :]
        q = q_ref[pl.ds(r0, chunk), :].astype(jnp.float32)
        q = q * _rms_scale_mxu(q, HEAD) * gq_ref[...]
        q = q * cos + pltpu.roll(q, 64, 1) * sin
        k = k_ref[pl.ds(r0, chunk), :].astype(jnp.float32)
        k = k * _rms_scale_mxu(k, HEAD) * gk_ref[...]
        k = k * cos + pltpu.roll(k, 64, 1) * sin
        v = v_ref[pl.ds(r0, chunk), :]
        nat[0][pl.ds(r0, chunk), :] = q
        nat[1][pl.ds(r0, chunk), :] = k
        nat[2][pl.ds(r0, chunk), :] = v.astype(jnp.float32)
        qd_ref[0, pl.ds(r0, chunk), :] = q.astype(qd_ref.dtype)
        kd_ref[0, pl.ds(QBLK + r0, chunk), :] = k.astype(kd_ref.dtype)
        vd_ref[0, pl.ds(QBLK + r0, chunk), 0:HEAD] = v
        return carry
    lax.fori_loop(0, seq // chunk, prep, 0)

    sub4, sub16 = seq // 4, seq // 16
    for a in range(3):
        for r4 in range(4):
            x = nat[a][pl.ds(r4, sub4, stride=4), :]
            by4[a][r4 * sub4:(r4 + 1) * sub4, :] = x
            dst[a][1, lead[a] + r4 * sub4:lead[a] + (r4 + 1) * sub4, 0:HEAD] = x.astype(dst[a].dtype)
    for a in range(3):
        for r4 in range(4):
            for j in range(4):
                r16 = r4 + 4 * j
                x = by4[a][pl.ds(r4 * sub4 + j, sub16, stride=4), :]
                dst[a][2, lead[a] + r16 * sub16:lead[a] + (r16 + 1) * sub16, 0:HEAD] = x.astype(dst[a].dtype)

    for p, d in enumerate(DILATIONS):
        nb = seq // d // QBLK
        shift = int(math.log2(nb))

        def block(g, carry, p=p, d=d, nb=nb, shift=shift):
            g0 = pl.multiple_of(g * QBLK, QBLK)
            q = qd_ref[p, pl.ds(g0, QBLK), :]
            k = kd_ref[p, pl.ds(g0, 2 * QBLK), :]
            v = vd_ref[p, pl.ds(g0, 2 * QBLK), :]
            n = g & (nb - 1)
            s = lax.dot_general(q, k, (((1,), (1,)), ((), ())), preferred_element_type=jnp.float32)
            s = s + bias_ref[jnp.where(n == 0, 1, 0)]
            m = jnp.max(s, axis=-1, keepdims=True)
            e = jnp.exp2(s - m)
            pv = jnp.dot(e.astype(v.dtype), v, preferred_element_type=jnp.float32)
            den = pv[:, HEAD:]
            r = g >> shift
            start = r + n * (QBLK * d)
            rows = pl.ds(start, QBLK, stride=d) if d > 1 else pl.ds(g0, QBLK)
            on_ref[p, rows, :] = pv[:, :HEAD] / den
            ls_ref[p, rows, :] = m + jnp.log2(den)
            return carry
        lax.fori_loop(0, seq // QBLK, block, 0, unroll=unroll)

    def mix(c, carry):
        r0 = pl.multiple_of(c * chunk, chunk)
        ls = [ls_ref[p, pl.ds(r0, chunk), :] for p in range(npat)]
        mx = functools.reduce(jnp.maximum, ls)
        w = [jnp.exp2(l - mx) for l in ls]
        num = sum(w[p] * on_ref[p, pl.ds(r0, chunk), :] for p in range(npat))
        o_ref[pl.ds(r0, chunk), :] = (num / sum(w)).astype(o_ref.dtype)
        return carry
    lax.fori_loop(0, seq // chunk, mix, 0)


def _dil_attn(proj, cos, sin, gq, gk, *, batch, seq):
    npat = len(DILATIONS)
    col = HEAD

    def qcol(h):
        return jnp.where(h < 4, 1536 // col + h, 4096 // col - 4 + h)

    const = lambda b, h: (0, 0)
    return pl.pallas_call(
        functools.partial(_dil_attn_kernel, seq=seq, unroll=16),
        grid=(batch, N_HEADS),
        in_specs=[
            pl.BlockSpec((seq, HEAD), lambda b, h: (b, qcol(h))),
            pl.BlockSpec((seq, HEAD), lambda b, h: (b, 4608 // col + h)),
            pl.BlockSpec((seq, HEAD), lambda b, h: (b, 5632 // col + h)),
            pl.BlockSpec((seq, HEAD), const),
            pl.BlockSpec((seq, HEAD), const),
            pl.BlockSpec((1, HEAD), const),
            pl.BlockSpec((1, HEAD), const),
        ],
        out_specs=pl.BlockSpec((seq, HEAD), lambda b, h: (b, h)),
        out_shape=jax.ShapeDtypeStruct((batch * seq, WIDTH), jnp.bfloat16),
        scratch_shapes=[
            pltpu.VMEM((npat, seq, HEAD), jnp.bfloat16),
            pltpu.VMEM((npat, seq + QBLK, HEAD), jnp.bfloat16),
            pltpu.VMEM((npat, seq + QBLK, 2 * HEAD), jnp.bfloat16),
            pltpu.VMEM((npat, seq, HEAD), jnp.float32),
            pltpu.VMEM((npat, seq, HEAD), jnp.float32),
            pltpu.VMEM((2, QBLK, 2 * QBLK), jnp.float32),
        ],
        compiler_params=pltpu.CompilerParams(
            dimension_semantics=("arbitrary", "arbitrary"), vmem_limit_bytes=VMEM_LIMIT),
        name="dil_attn",
    )(proj, proj, proj, cos, sin, gq, gk)


def _out_proj_kernel(oa_ref, ob_ref, g_ref, x_ref, ga_ref, gb_ref, w_ref, o_ref):
    def branch(o_ref_, gain_ref, gate):
        o = o_ref_[...].astype(jnp.float32)
        y = o * _rms_scale(o, WIDTH) * gain_ref[...]
        return (y * (gate / (1.0 + jnp.exp(-gate)))).astype(jnp.bfloat16)

    ya = branch(oa_ref, ga_ref, g_ref[:, :WIDTH].astype(jnp.float32))
    yb = branch(ob_ref, gb_ref, g_ref[:, WIDTH:].astype(jnp.float32))
    y = jnp.concatenate([ya, yb], axis=-1)
    o_ref[...] = x_ref[...] + jnp.dot(y, w_ref[...], preferred_element_type=jnp.float32)


def _out_proj(o_a, o_b, proj, x2, ga, gb, w, *, tm=512):
    m = x2.shape[0]
    const = lambda i: (0, 0)
    return pl.pallas_call(
        _out_proj_kernel,
        grid=(m // tm,),
        in_specs=[
            pl.BlockSpec((tm, WIDTH), lambda i: (i, 0)),
            pl.BlockSpec((tm, WIDTH), lambda i: (i, 0)),
            pl.BlockSpec((tm, 2 * WIDTH), lambda i: (i, 1)),
            pl.BlockSpec((tm, D_MODEL), lambda i: (i, 0)),
            pl.BlockSpec((1, WIDTH), const),
            pl.BlockSpec((1, WIDTH), const),
            pl.BlockSpec((2 * WIDTH, D_MODEL), const),
        ],
        out_specs=pl.BlockSpec((tm, D_MODEL), lambda i: (i, 0)),
        out_shape=jax.ShapeDtypeStruct((m, D_MODEL), jnp.float32),
        compiler_params=pltpu.CompilerParams(
            dimension_semantics=("parallel",), vmem_limit_bytes=VMEM_LIMIT),
        name="out_proj",
    )(o_a, o_b, proj, x2, ga, gb, w)


def _pad_rope_cols(a):
    z = jnp.zeros(a.shape[:-1] + (ROPE // 2,), a.dtype)
    return jnp.concatenate([a[..., :ROPE // 2], z, a[..., ROPE // 2:], z], axis=-1)


def _rope_tables(seq, d, padded):
    inv = ROPE_THETA ** (-np.arange(0, d, 2, dtype=np.float64) / d)
    ang = np.arange(seq, dtype=np.float64)[:, None] * inv[None, :]
    cos, sin = np.cos(ang), np.sin(ang)
    cos2 = np.concatenate([cos, cos], axis=-1)
    sin2 = np.concatenate([-sin, sin], axis=-1)
    if padded:
        z = np.zeros((seq, d // 2))
        cos2 = np.concatenate([cos, z, cos, z], axis=-1)
        sin2 = np.concatenate([-sin, z, sin, z], axis=-1)
    return jnp.asarray(cos2, jnp.float32), jnp.asarray(sin2, jnp.float32)


def kernel(x, norm_gain, w_in, q_a_norm_gain, kv_a_norm_gain, w_uq, w_ukv, mla_q_norm_gain,
           mla_k_norm_gain, dil_q_norm_gain, dil_k_norm_gain, mla_out_norm_gain,
           dil_out_norm_gain, w_out):
    batch, seq, _ = x.shape
    depth = w_in.shape[0]
    bf = jnp.bfloat16
    cos_d, sin_d = _rope_tables(seq, HEAD, padded=False)
    cos_m, sin_m = _rope_tables(seq, ROPE, padded=True)
    h2 = x.reshape(batch * seq, D_MODEL)

    for l in range(depth):
        w_r = _w_prep(jnp.swapaxes(w_in[l], 0, 1))
        wq = w_uq[l].reshape(Q_RANK, N_HEADS, MLA_QK)
        wq_r = jnp.concatenate([wq[..., :HEAD], _pad_rope_cols(wq[..., HEAD:])], axis=-1)
        wq_r = wq_r.reshape(Q_RANK, N_HEADS * 2 * HEAD).astype(bf)
        wkv = w_ukv[l].reshape(KV_RANK, N_HEADS, 2 * HEAD)
        wkv_r = jnp.concatenate([wkv[..., :HEAD].reshape(KV_RANK, WIDTH),
                                 wkv[..., HEAD:].reshape(KV_RANK, WIDTH)], axis=-1).astype(bf)

        gq_m = mla_q_norm_gain[l]
        gk_m = mla_k_norm_gain[l]
        gq_full = jnp.concatenate([gq_m[:HEAD], _pad_rope_cols(gq_m[HEAD:])])[None, :]
        gq_full = gq_full * (math.log2(math.e) / math.sqrt(MLA_QK))
        gk_full = jnp.concatenate([gk_m[:HEAD], _pad_rope_cols(gk_m[HEAD:])])[None, :]
        gq_d = dil_q_norm_gain[l][None, :] * (math.log2(math.e) / math.sqrt(HEAD))
        gk_d = dil_k_norm_gain[l][None, :]

        proj = _in_proj(h2, norm_gain[l][None, :], w_r)
        q_f, k_f, v_a = _mla_prep(proj, wq_r, wkv_r, q_a_norm_gain[l][None, :],
                                  kv_a_norm_gain[l][None, :], gq_full, gk_full, cos_m, sin_m, seq=seq)
        o_a = _mla_attn(q_f, k_f, v_a, batch=batch, seq=seq)
        o_b = _dil_attn(proj, cos_d, sin_d, gq_d, gk_d, batch=batch, seq=seq)
        h2 = _out_proj(o_a, o_b, proj, h2, mla_out_norm_gain[l][None, :],
                       dil_out_norm_gain[l][None, :], w_out[l].astype(bf))
    return h2.reshape(batch, seq, D_MODEL)
```

```python
import functools
import math

import jax
import jax.numpy as jnp
import numpy as np
from jax import lax
from jax.experimental import pallas as pl
from jax.experimental.pallas import tpu as pltpu

EPS = 1e-6
ROPE_THETA = 10000.0
NEG = -1e30

D_MODEL = 2048
N_HEADS = 8
HEAD = 128
ROPE = 64
Q_RANK = 768
KV_RANK = 512
WIDTH = N_HEADS * HEAD
MLA_QK = HEAD + ROPE
QBLK = 128
N_BACK = 128
DILATIONS = (1, 4, 16)

PROJ_COLS = 6656
LAT_COLS = 1536
VMEM_LIMIT = 56 * 1024 * 1024


def _rms_scale(x, n):
    return lax.rsqrt(jnp.sum(x * x, axis=-1, keepdims=True) * (1.0 / n) + EPS)


def _rms_scale_mxu(x, n):
    ones = jnp.ones((x.shape[-1], x.shape[-1]), jnp.bfloat16)
    ss = jnp.dot((x * x).astype(jnp.bfloat16), ones, preferred_element_type=jnp.float32)
    return lax.rsqrt(ss * (1.0 / n) + EPS)


_O_CQ, _O_CKV, _O_KR = 0, Q_RANK, Q_RANK + KV_RANK
_O_GA = _O_KR + ROPE
_O_QB = _O_GA + WIDTH
_O_KB, _O_VB, _O_GB = _O_QB + WIDTH, _O_QB + 2 * WIDTH, _O_QB + 3 * WIDTH
_IN_COLS = _O_GB + WIDTH


W_TILE = 512
_W_TILE_SRC = (_O_CKV, _O_CQ, _O_CQ + 256, _O_QB, _O_GA, _O_GA + 512, _O_GB, _O_GB + 512,
               _O_QB + 512, _O_KB, _O_KB + 512, _O_VB, _O_VB + 512)


def _w_prep_kernel(src_ref, w_ref, kr_ref, o_ref):
    del src_ref
    o_ref[...] = w_ref[...].astype(o_ref.dtype)

    @pl.when(pl.program_id(0) == 1)
    def _():
        half = ROPE // 2
        zeros = jnp.zeros((half, D_MODEL), o_ref.dtype)
        o_ref[2 * HEAD:, :] = w_ref[:2 * HEAD, :].astype(o_ref.dtype)
        o_ref[0:half, :] = kr_ref[0:half, :].astype(o_ref.dtype)
        o_ref[half:2 * half, :] = zeros
        o_ref[2 * half:3 * half, :] = kr_ref[half:, :].astype(o_ref.dtype)
        o_ref[3 * half:4 * half, :] = zeros
        o_ref[HEAD:2 * HEAD, :] = jnp.zeros((HEAD, D_MODEL), o_ref.dtype)


def _w_prep(wt):
    assert all(s % ROPE == 0 for s in _W_TILE_SRC)
    src = jnp.asarray([s // ROPE for s in _W_TILE_SRC], jnp.int32)
    return pl.pallas_call(
        _w_prep_kernel,
        grid_spec=pltpu.PrefetchScalarGridSpec(
            num_scalar_prefetch=1,
            grid=(PROJ_COLS // W_TILE,),
            in_specs=[
                pl.BlockSpec((pl.Element(W_TILE), pl.Element(D_MODEL)), lambda t, src: (src[t] * ROPE, 0)),
                pl.BlockSpec((pl.Element(ROPE), pl.Element(D_MODEL)), lambda t, src: (_O_KR, 0)),
            ],
            out_specs=pl.BlockSpec((W_TILE, D_MODEL), lambda t, src: (t, 0)),
        ),
        out_shape=jax.ShapeDtypeStruct((PROJ_COLS, D_MODEL), jnp.bfloat16),
        compiler_params=pltpu.CompilerParams(
            dimension_semantics=("arbitrary",), vmem_limit_bytes=VMEM_LIMIT),
        name="w_prep",
    )(src, wt, wt)


def _in_proj_kernel(x_ref, g_ref, w_ref, o_ref, hn_ref, *, row_chunk):
    @pl.when(pl.program_id(1) == 0)
    def _():
        def body(c, carry):
            r0 = pl.multiple_of(c * row_chunk, row_chunk)
            x = x_ref[pl.ds(r0, row_chunk), :]
            hn = x * _rms_scale(x, D_MODEL) * g_ref[...]
            hn_ref[pl.ds(r0, row_chunk), :] = hn.astype(hn_ref.dtype)
            return carry
        lax.fori_loop(0, x_ref.shape[0] // row_chunk, body, 0, unroll=4)

    half = o_ref.shape[0] // 2
    for r in (0, half):
        o_ref[r:r + half, :] = lax.dot_general(hn_ref[r:r + half, :], w_ref[...], (((1,), (1,)), ((), ())),
                                               preferred_element_type=jnp.float32).astype(o_ref.dtype)


def _in_proj(x2, gain, w, *, tm=2048, tn=512):
    m = x2.shape[0]
    return pl.pallas_call(
        functools.partial(_in_proj_kernel, row_chunk=64),
        grid=(m // tm, PROJ_COLS // tn),
        in_specs=[
            pl.BlockSpec((tm, D_MODEL), lambda i, j: (i, 0)),
            pl.BlockSpec((1, D_MODEL), lambda i, j: (0, 0)),
            pl.BlockSpec((tn, D_MODEL), lambda i, j: (j, 0)),
        ],
        out_specs=pl.BlockSpec((tm, tn), lambda i, j: (i, j)),
        out_shape=jax.ShapeDtypeStruct((m, PROJ_COLS), jnp.bfloat16),
        scratch_shapes=[pltpu.VMEM((tm, D_MODEL), jnp.bfloat16)],
        compiler_params=pltpu.CompilerParams(
            dimension_semantics=("parallel", "arbitrary"),
            vmem_limit_bytes=VMEM_LIMIT),
        name="in_proj",
    )(x2, gain, w)


def _rope_pad(x, cos, sin):
    return x * cos + pltpu.roll(x, 64, 1) * sin


def _mla_prep_kernel(lat_ref, wq_ref, wkv_ref, gqa_ref, gkva_ref, gq_ref, gk_ref,
                     cos_ref, sin_ref, q_ref, k_ref, v_ref):
    cos = cos_ref[...]
    sin = sin_ref[...]
    gq = gq_ref[...]
    gk = gk_ref[...]

    c_kv = lat_ref[:, 0:KV_RANK].astype(jnp.float32)
    cn = (c_kv * _rms_scale(c_kv, KV_RANK) * gkva_ref[...]).astype(jnp.bfloat16)
    kv = jnp.dot(cn, wkv_ref[...], preferred_element_type=jnp.float32)
    ones = jnp.ones((kv.shape[0], HEAD), v_ref.dtype)
    for h in range(N_HEADS):
        v_ref[:, 2 * h * HEAD:(2 * h + 1) * HEAD] = kv[:, WIDTH + h * HEAD:WIDTH + (h + 1) * HEAD].astype(v_ref.dtype)
        v_ref[:, (2 * h + 1) * HEAD:(2 * h + 2) * HEAD] = ones

    k_r = lat_ref[:, KV_RANK:KV_RANK + HEAD].astype(jnp.float32)
    k_r = k_r * _rms_scale(k_r, ROPE) * gk[:, HEAD:]
    k_rope = _rope_pad(k_r, cos, sin).astype(k_ref.dtype)
    for h in range(N_HEADS):
        kn = kv[:, h * HEAD:(h + 1) * HEAD]
        kn = kn * _rms_scale(kn, HEAD) * gk[:, :HEAD]
        k_ref[:, 2 * h * HEAD:(2 * h + 1) * HEAD] = kn.astype(k_ref.dtype)
        k_ref[:, (2 * h + 1) * HEAD:(2 * h + 2) * HEAD] = k_rope

    c_q = lat_ref[:, 2 * HEAD + KV_RANK:].astype(jnp.float32)
    cqn = (c_q * _rms_scale(c_q, Q_RANK) * gqa_ref[...]).astype(jnp.bfloat16)
    q = jnp.dot(cqn, wq_ref[...], preferred_element_type=jnp.float32)
    for h in range(N_HEADS):
        qn = q[:, 2 * h * HEAD:(2 * h + 1) * HEAD]
        qn = qn * _rms_scale(qn, HEAD) * gq[:, :HEAD]
        q_ref[:, 2 * h * HEAD:(2 * h + 1) * HEAD] = qn.astype(q_ref.dtype)
        qr = q[:, (2 * h + 1) * HEAD:(2 * h + 2) * HEAD]
        qr = qr * _rms_scale(qr, ROPE) * gq[:, HEAD:]
        q_ref[:, (2 * h + 1) * HEAD:(2 * h + 2) * HEAD] = _rope_pad(qr, cos, sin).astype(q_ref.dtype)


def _mla_prep(proj, wq, wkv, gqa, gkva, gq, gk, cos, sin, *, seq, tm=256):
    m = proj.shape[0]
    nseq = seq // tm
    const = lambda i: (0, 0)
    return pl.pallas_call(
        _mla_prep_kernel,
        grid=(m // tm,),
        in_specs=[
            pl.BlockSpec((tm, LAT_COLS), lambda i: (i, 0)),
            pl.BlockSpec(wq.shape, const),
            pl.BlockSpec(wkv.shape, const),
            pl.BlockSpec(gqa.shape, const),
            pl.BlockSpec(gkva.shape, const),
            pl.BlockSpec(gq.shape, const),
            pl.BlockSpec(gk.shape, const),
            pl.BlockSpec((tm, HEAD), lambda i: (i % nseq, 0)),
            pl.BlockSpec((tm, HEAD), lambda i: (i % nseq, 0)),
        ],
        out_specs=[
            pl.BlockSpec((tm, 2 * WIDTH), lambda i: (i, 0)),
            pl.BlockSpec((tm, 2 * WIDTH), lambda i: (i, 0)),
            pl.BlockSpec((tm, 2 * WIDTH), lambda i: (i, 0)),
        ],
        out_shape=[
            jax.ShapeDtypeStruct((m, 2 * WIDTH), jnp.bfloat16),
            jax.ShapeDtypeStruct((m, 2 * WIDTH), jnp.bfloat16),
            jax.ShapeDtypeStruct((m, 2 * WIDTH), jnp.bfloat16),
        ],
        compiler_params=pltpu.CompilerParams(
            dimension_semantics=("parallel",), vmem_limit_bytes=VMEM_LIMIT),
        name="mla_prep",
    )(proj, wq, wkv, gqa, gkva, gq, gk, cos, sin)


def _mla_attn_kernel(q_ref, k_ref, v_ref, o_ref, m_ref, acc_ref, sa_ref, sb_ref, *, tq, tk, heads):
    assert tq == tk
    i = pl.program_id(2)
    m_ref[...] = jnp.full(m_ref.shape, NEG, jnp.float32)
    acc_ref[...] = jnp.zeros(acc_ref.shape, jnp.float32)

    def scores(j, s_ref):
        k0 = pl.multiple_of(j * tk, tk)
        for h in range(heads):
            cols = slice(2 * h * HEAD, 2 * (h + 1) * HEAD)
            s_ref[h] = lax.dot_general(q_ref[:, cols], k_ref[pl.ds(k0, tk), cols],
                                       (((1,), (1,)), ((), ())),
                                       preferred_element_type=jnp.float32)

    def update(h, rows, s, v):
        m_prev = m_ref[h, rows, :]
        m_new = jnp.maximum(m_prev, jnp.max(s, axis=-1, keepdims=True))
        alpha = jnp.exp2(m_prev - m_new)
        p = jnp.exp2(s - jnp.tile(m_new, (1, s.shape[1] // HEAD)))
        pv = jnp.dot(p.astype(v.dtype), v, preferred_element_type=jnp.float32)
        acc_ref[h, rows, :] = jnp.tile(alpha, (1, 2)) * acc_ref[h, rows, :] + pv
        m_ref[h, rows, :] = m_new

    def softmax_pv(j, s_ref, masked):
        k0 = pl.multiple_of(j * tk, tk)
        hq = tq // 2
        for h in range(heads):
            cols = slice(2 * h * HEAD, 2 * (h + 1) * HEAD)
            if not masked:
                update(h, slice(None), s_ref[h], v_ref[pl.ds(k0, tk), cols])
                continue
            row = lax.broadcasted_iota(jnp.int32, (hq, hq), 0)
            col = lax.broadcasted_iota(jnp.int32, (hq, hq), 1)
            tri = col <= row
            s0 = jnp.where(tri, s_ref[h, 0:hq, 0:hq], NEG)
            update(h, slice(0, hq), s0, v_ref[pl.ds(k0, hq), cols])
            s1 = jnp.concatenate([s_ref[h, hq:, 0:hq], jnp.where(tri, s_ref[h, hq:, hq:], NEG)], axis=1)
            update(h, slice(hq, tq), s1, v_ref[pl.ds(k0, tk), cols])

    scores(0, sa_ref)

    def body(t, carry):
        j = 2 * t
        scores(j + 1, sb_ref)
        softmax_pv(j, sa_ref, False)
        scores(j + 2, sa_ref)
        softmax_pv(j + 1, sb_ref, False)
        return carry
    lax.fori_loop(0, i // 2, body, 0)

    @pl.when(i % 2 == 0)
    def _():
        softmax_pv(i, sa_ref, True)

    @pl.when(i % 2 == 1)
    def _():
        scores(i, sb_ref)
        softmax_pv(i - 1, sa_ref, False)
        softmax_pv(i, sb_ref, True)

    for h in range(heads):
        acc = acc_ref[h]
        o_ref[:, h * HEAD:(h + 1) * HEAD] = (acc[:, :HEAD] / acc[:, HEAD:]).astype(o_ref.dtype)


def _mla_attn(q, k, v, *, batch, seq, tq=512, tk=512, heads=4):
    nq = seq // tq
    w = 2 * HEAD * heads
    return pl.pallas_call(
        functools.partial(_mla_attn_kernel, tq=tq, tk=tk, heads=heads),
        grid=(batch, N_HEADS // heads, nq),
        in_specs=[
            pl.BlockSpec((tq, w), lambda b, h, i: (b * nq + i, h)),
            pl.BlockSpec((seq, w), lambda b, h, i: (b, h)),
            pl.BlockSpec((seq, w), lambda b, h, i: (b, h)),
        ],
        out_specs=pl.BlockSpec((tq, HEAD * heads), lambda b, h, i: (b * nq + i, h)),
        out_shape=jax.ShapeDtypeStruct((batch * seq, WIDTH), jnp.bfloat16),
        scratch_shapes=[
            pltpu.VMEM((heads, tq, HEAD), jnp.float32),
            pltpu.VMEM((heads, tq, 2 * HEAD), jnp.float32),
            pltpu.VMEM((heads, tq, tk), jnp.float32),
            pltpu.VMEM((heads, tq, tk), jnp.float32),
        ],
        compiler_params=pltpu.CompilerParams(
            dimension_semantics=("parallel", "parallel", "arbitrary"),
            vmem_limit_bytes=VMEM_LIMIT),
        name="mla_attn",
    )(q, k, v)


def _dil_attn_kernel(q_ref, k_ref, v_ref, cos_ref, sin_ref, gq_ref, gk_ref, o_ref,
                     qd_ref, kd_ref, vd_ref, on_ref, ls_ref, bias_ref, st_ref, *, seq, unroll):
    chunk = 1024
    npat = len(DILATIONS)
    assert DILATIONS == (1, 4, 16)
    nat = [on_ref.at[a] for a in range(3)]
    by4 = [ls_ref.at[a] for a in range(3)]
    dst = [qd_ref, kd_ref, vd_ref]
    lead = [0, QBLK, QBLK]

    @pl.when((pl.program_id(0) == 0) & (pl.program_id(1) == 0))
    def _():
        qi = lax.broadcasted_iota(jnp.int32, (QBLK, 2 * QBLK), 0)
        kj = lax.broadcasted_iota(jnp.int32, (QBLK, 2 * QBLK), 1)
        band = (kj >= qi) & (kj <= qi + N_BACK)
        bias_ref[0] = jnp.where(band, 0.0, NEG)
        bias_ref[1] = jnp.where(band & (kj >= QBLK), 0.0, NEG)
        for p in range(npat):
            vd_ref[p, QBLK:, HEAD:] = jnp.ones((seq, HEAD), vd_ref.dtype)
            kd_ref[p, 0:QBLK, :] = jnp.zeros((QBLK, HEAD), kd_ref.dtype)
            vd_ref[p, 0:QBLK, :] = jnp.zeros((QBLK, 2 * HEAD), vd_ref.dtype)

    def prep(c, carry):
        r0 = pl.multiple_of(c * chunk, chunk)
        cos = cos_ref[pl.ds(r0, chunk), :]
        sin = sin_ref[pl.ds(r0, chunk), :]
        q = q_ref[pl.ds(r0, chunk), :].astype(jnp.float32)
        q = q * _rms_scale_mxu(q, HEAD) * gq_ref[...]
        q = q * cos + pltpu.roll(q, 64, 1) * sin
        k = k_ref[pl.ds(r0, chunk), :].astype(jnp.float32)
        k = k * _rms_scale_mxu(k, HEAD) * gk_ref[...]
        k = k * cos + pltpu.roll(k, 64, 1) * sin
        v = v_ref[pl.ds(r0, chunk), :]
        nat[0][pl.ds(r0, chunk), :] = q
        nat[1][pl.ds(r0, chunk), :] = k
        nat[2][pl.ds(r0, chunk), :] = v.astype(jnp.float32)
        qd_ref[0, pl.ds(r0, chunk), :] = q.astype(qd_ref.dtype)
        kd_ref[0, pl.ds(QBLK + r0, chunk), :] = k.astype(kd_ref.dtype)
        vd_ref[0, pl.ds(QBLK + r0, chunk), 0:HEAD] = v
        return carry
    lax.fori_loop(0, seq // chunk, prep, 0)

    sub4, sub16 = seq // 4, seq // 16
    for a in range(3):
        for r4 in range(4):
            x = nat[a][pl.ds(r4, sub4, stride=4), :]
            by4[a][r4 * sub4:(r4 + 1) * sub4, :] = x
            dst[a][1, lead[a] + r4 * sub4:lead[a] + (r4 + 1) * sub4, 0:HEAD] = x.astype(dst[a].dtype)
    for a in range(3):
        for r4 in range(4):
            for j in range(4):
                r16 = r4 + 4 * j
                x = by4[a][pl.ds(r4 * sub4 + j, sub16, stride=4), :]
                dst[a][2, lead[a] + r16 * sub16:lead[a] + (r16 + 1) * sub16, 0:HEAD] = x.astype(dst[a].dtype)

    for p, d in enumerate(DILATIONS):
        nb = seq // d // QBLK
        shift = int(math.log2(nb))

        def block(g, carry, p=p, d=d, nb=nb, shift=shift):
            g0 = pl.multiple_of(g * QBLK, QBLK)
            q = qd_ref[p, pl.ds(g0, QBLK), :]
            k = kd_ref[p, pl.ds(g0, 2 * QBLK), :]
            v = vd_ref[p, pl.ds(g0, 2 * QBLK), :]
            n = g & (nb - 1)
            s = lax.dot_general(q, k, (((1,), (1,)), ((), ())), preferred_element_type=jnp.float32)
            s = s + bias_ref[jnp.where(n == 0, 1, 0)]
            m = jnp.max(s, axis=-1, keepdims=True)
            e = jnp.exp2(s - m)
            pv = jnp.dot(e.astype(v.dtype), v, preferred_element_type=jnp.float32)
            den = pv[:, HEAD:]
            o = pv[:, :HEAD] / den
            lse = m + jnp.log2(den)
            r = g >> shift
            if d == 16:
                rows = pl.ds((r & 3) * sub4 + (r >> 2) + n * (QBLK * 4), QBLK, stride=4)
                st_ref[0, rows, :] = o
                st_ref[1, rows, :] = lse
            else:
                rows = pl.ds(r + n * (QBLK * d), QBLK, stride=d) if d > 1 else pl.ds(g0, QBLK)
                on_ref[p, rows, :] = o
                ls_ref[p, rows, :] = lse
            return carry
        lax.fori_loop(0, seq // QBLK, block, 0, unroll=unroll)

    for a, out in enumerate((on_ref, ls_ref)):
        for r4 in range(4):
            out[2, pl.ds(r4, sub4, stride=4), :] = st_ref[a, r4 * sub4:(r4 + 1) * sub4, :]

    def mix(c, carry):
        r0 = pl.multiple_of(c * chunk, chunk)
        ls = [ls_ref[p, pl.ds(r0, chunk), :] for p in range(npat)]
        mx = functools.reduce(jnp.maximum, ls)
        w = [jnp.exp2(l - mx) for l in ls]
        num = sum(w[p] * on_ref[p, pl.ds(r0, chunk), :] for p in range(npat))
        o_ref[pl.ds(r0, chunk), :] = (num / sum(w)).astype(o_ref.dtype)
        return carry
    lax.fori_loop(0, seq // chunk, mix, 0)


def _dil_attn(proj, cos, sin, gq, gk, *, batch, seq):
    npat = len(DILATIONS)
    col = HEAD

    def qcol(h):
        return jnp.where(h < 4, 1536 // col + h, 4096 // col - 4 + h)

    const = lambda b, h: (0, 0)
    return pl.pallas_call(
        functools.partial(_dil_attn_kernel, seq=seq, unroll=16),
        grid=(batch, N_HEADS),
        in_specs=[
            pl.BlockSpec((seq, HEAD), lambda b, h: (b, qcol(h))),
            pl.BlockSpec((seq, HEAD), lambda b, h: (b, 4608 // col + h)),
            pl.BlockSpec((seq, HEAD), lambda b, h: (b, 5632 // col + h)),
            pl.BlockSpec((seq, HEAD), const),
            pl.BlockSpec((seq, HEAD), const),
            pl.BlockSpec((1, HEAD), const),
            pl.BlockSpec((1, HEAD), const),
        ],
        out_specs=pl.BlockSpec((seq, HEAD), lambda b, h: (b, h)),
        out_shape=jax.ShapeDtypeStruct((batch * seq, WIDTH), jnp.bfloat16),
        scratch_shapes=[
            pltpu.VMEM((npat, seq, HEAD), jnp.bfloat16),
            pltpu.VMEM((npat, seq + QBLK, HEAD), jnp.bfloat16),
            pltpu.VMEM((npat, seq + QBLK, 2 * HEAD), jnp.bfloat16),
            pltpu.VMEM((npat, seq, HEAD), jnp.float32),
            pltpu.VMEM((npat, seq, HEAD), jnp.float32),
            pltpu.VMEM((2, QBLK, 2 * QBLK), jnp.float32),
            pltpu.VMEM((2, seq, HEAD), jnp.float32),
        ],
        compiler_params=pltpu.CompilerParams(
            dimension_semantics=("arbitrary", "arbitrary"), vmem_limit_bytes=VMEM_LIMIT),
        name="dil_attn",
    )(proj, proj, proj, cos, sin, gq, gk)


def _out_proj_kernel(oa_ref, ob_ref, g_ref, x_ref, ga_ref, gb_ref, w_ref, o_ref):
    def branch(o_ref_, gain_ref, gate):
        o = o_ref_[...].astype(jnp.float32)
        y = o * _rms_scale(o, WIDTH) * gain_ref[...]
        return (y * (gate / (1.0 + jnp.exp(-gate)))).astype(jnp.bfloat16)

    ya = branch(oa_ref, ga_ref, g_ref[:, :WIDTH].astype(jnp.float32))
    yb = branch(ob_ref, gb_ref, g_ref[:, WIDTH:].astype(jnp.float32))
    y = jnp.concatenate([ya, yb], axis=-1)
    o_ref[...] = x_ref[...] + jnp.dot(y, w_ref[...], preferred_element_type=jnp.float32)


def _out_proj(o_a, o_b, proj, x2, ga, gb, w, *, tm=512):
    m = x2.shape[0]
    const = lambda i: (0, 0)
    return pl.pallas_call(
        _out_proj_kernel,
        grid=(m // tm,),
        in_specs=[
            pl.BlockSpec((tm, WIDTH), lambda i: (i, 0)),
            pl.BlockSpec((tm, WIDTH), lambda i: (i, 0)),
            pl.BlockSpec((tm, 2 * WIDTH), lambda i: (i, 1)),
            pl.BlockSpec((tm, D_MODEL), lambda i: (i, 0)),
            pl.BlockSpec((1, WIDTH), const),
            pl.BlockSpec((1, WIDTH), const),
            pl.BlockSpec((2 * WIDTH, D_MODEL), const),
        ],
        out_specs=pl.BlockSpec((tm, D_MODEL), lambda i: (i, 0)),
        out_shape=jax.ShapeDtypeStruct((m, D_MODEL), jnp.float32),
        compiler_params=pltpu.CompilerParams(
            dimension_semantics=("parallel",), vmem_limit_bytes=VMEM_LIMIT),
        name="out_proj",
    )(o_a, o_b, proj, x2, ga, gb, w)


def _pad_rope_cols(a):
    z = jnp.zeros(a.shape[:-1] + (ROPE // 2,), a.dtype)
    return jnp.concatenate([a[..., :ROPE // 2], z, a[..., ROPE // 2:], z], axis=-1)


def _rope_tables(seq, d, padded):
    inv = ROPE_THETA ** (-np.arange(0, d, 2, dtype=np.float64) / d)
    ang = np.arange(seq, dtype=np.float64)[:, None] * inv[None, :]
    cos, sin = np.cos(ang), np.sin(ang)
    cos2 = np.concatenate([cos, cos], axis=-1)
    sin2 = np.concatenate([-sin, sin], axis=-1)
    if padded:
        z = np.zeros((seq, d // 2))
        cos2 = np.concatenate([cos, z, cos, z], axis=-1)
        sin2 = np.concatenate([-sin, z, sin, z], axis=-1)
    return jnp.asarray(cos2, jnp.float32), jnp.asarray(sin2, jnp.float32)


def kernel(x, norm_gain, w_in, q_a_norm_gain, kv_a_norm_gain, w_uq, w_ukv, mla_q_norm_gain,
           mla_k_norm_gain, dil_q_norm_gain, dil_k_norm_gain, mla_out_norm_gain,
           dil_out_norm_gain, w_out):
    batch, seq, _ = x.shape
    depth = w_in.shape[0]
    bf = jnp.bfloat16
    cos_d, sin_d = _rope_tables(seq, HEAD, padded=False)
    cos_m, sin_m = _rope_tables(seq, ROPE, padded=True)
    h2 = x.reshape(batch * seq, D_MODEL)

    for l in range(depth):
        w_r = _w_prep(jnp.swapaxes(w_in[l], 0, 1))
        wq = w_uq[l].reshape(Q_RANK, N_HEADS, MLA_QK)
        wq_r = jnp.concatenate([wq[..., :HEAD], _pad_rope_cols(wq[..., HEAD:])], axis=-1)
        wq_r = wq_r.reshape(Q_RANK, N_HEADS * 2 * HEAD).astype(bf)
        wkv = w_ukv[l].reshape(KV_RANK, N_HEADS, 2 * HEAD)
        wkv_r = jnp.concatenate([wkv[..., :HEAD].reshape(KV_RANK, WIDTH),
                                 wkv[..., HEAD:].reshape(KV_RANK, WIDTH)], axis=-1).astype(bf)

        gq_m = mla_q_norm_gain[l]
        gk_m = mla_k_norm_gain[l]
        gq_full = jnp.concatenate([gq_m[:HEAD], _pad_rope_cols(gq_m[HEAD:])])[None, :]
        gq_full = gq_full * (math.log2(math.e) / math.sqrt(MLA_QK))
        gk_full = jnp.concatenate([gk_m[:HEAD], _pad_rope_cols(gk_m[HEAD:])])[None, :]
        gq_d = dil_q_norm_gain[l][None, :] * (math.log2(math.e) / math.sqrt(HEAD))
        gk_d = dil_k_norm_gain[l][None, :]

        proj = _in_proj(h2, norm_gain[l][None, :], w_r)
        q_f, k_f, v_a = _mla_prep(proj, wq_r, wkv_r, q_a_norm_gain[l][None, :],
                                  kv_a_norm_gain[l][None, :], gq_full, gk_full, cos_m, sin_m, seq=seq)
        o_a = _mla_attn(q_f, k_f, v_a, batch=batch, seq=seq)
        o_b = _dil_attn(proj, cos_d, sin_d, gq_d, gk_d, batch=batch, seq=seq)
        h2 = _out_proj(o_a, o_b, proj, h2, mla_out_norm_gain[l][None, :],
                       dil_out_norm_gain[l][None, :], w_out[l].astype(bf))
    return h2.reshape(batch, seq, D_MODEL)
```

```python
import functools
import math

import jax
import jax.numpy as jnp
import numpy as np
from jax import lax
from jax.experimental import pallas as pl
from jax.experimental.pallas import tpu as pltpu

EPS = 1e-6
ROPE_THETA = 10000.0
NEG = -1e30

D_MODEL = 2048
N_HEADS = 8
HEAD = 128
ROPE = 64
Q_RANK = 768
KV_RANK = 512
WIDTH = N_HEADS * HEAD
MLA_QK = HEAD + ROPE
QBLK = 128
N_BACK = 128
DILATIONS = (1, 4, 16)

PROJ_COLS = 6656
LAT_COLS = 1536
VMEM_LIMIT = 56 * 1024 * 1024


def _rms_scale(x, n):
    return lax.rsqrt(jnp.sum(x * x, axis=-1, keepdims=True) * (1.0 / n) + EPS)


def _rms_scale_mxu(x, n):
    ones = jnp.ones((x.shape[-1], x.shape[-1]), jnp.bfloat16)
    ss = jnp.dot((x * x).astype(jnp.bfloat16), ones, preferred_element_type=jnp.float32)
    return lax.rsqrt(ss * (1.0 / n) + EPS)


_O_CQ, _O_CKV, _O_KR = 0, Q_RANK, Q_RANK + KV_RANK
_O_GA = _O_KR + ROPE
_O_QB = _O_GA + WIDTH
_O_KB, _O_VB, _O_GB = _O_QB + WIDTH, _O_QB + 2 * WIDTH, _O_QB + 3 * WIDTH
_IN_COLS = _O_GB + WIDTH


W_TILE = 512
_W_TILE_SRC = (_O_CKV, _O_CQ, _O_CQ + 256, _O_QB, _O_GA, _O_GA + 512, _O_GB, _O_GB + 512,
               _O_QB + 512, _O_KB, _O_KB + 512, _O_VB, _O_VB + 512)


def _w_prep_kernel(src_ref, w_ref, kr_ref, o_ref):
    del src_ref
    o_ref[...] = w_ref[...].astype(o_ref.dtype)

    @pl.when(pl.program_id(0) == 1)
    def _():
        half = ROPE // 2
        zeros = jnp.zeros((half, D_MODEL), o_ref.dtype)
        o_ref[2 * HEAD:, :] = w_ref[:2 * HEAD, :].astype(o_ref.dtype)
        o_ref[0:half, :] = kr_ref[0:half, :].astype(o_ref.dtype)
        o_ref[half:2 * half, :] = zeros
        o_ref[2 * half:3 * half, :] = kr_ref[half:, :].astype(o_ref.dtype)
        o_ref[3 * half:4 * half, :] = zeros
        o_ref[HEAD:2 * HEAD, :] = jnp.zeros((HEAD, D_MODEL), o_ref.dtype)


def _w_prep(wt):
    assert all(s % ROPE == 0 for s in _W_TILE_SRC)
    src = jnp.asarray([s // ROPE for s in _W_TILE_SRC], jnp.int32)
    return pl.pallas_call(
        _w_prep_kernel,
        grid_spec=pltpu.PrefetchScalarGridSpec(
            num_scalar_prefetch=1,
            grid=(PROJ_COLS // W_TILE,),
            in_specs=[
                pl.BlockSpec((pl.Element(W_TILE), pl.Element(D_MODEL)), lambda t, src: (src[t] * ROPE, 0)),
                pl.BlockSpec((pl.Element(ROPE), pl.Element(D_MODEL)), lambda t, src: (_O_KR, 0)),
            ],
            out_specs=pl.BlockSpec((W_TILE, D_MODEL), lambda t, src: (t, 0)),
        ),
        out_shape=jax.ShapeDtypeStruct((PROJ_COLS, D_MODEL), jnp.bfloat16),
        compiler_params=pltpu.CompilerParams(
            dimension_semantics=("arbitrary",), vmem_limit_bytes=VMEM_LIMIT),
        name="w_prep",
    )(src, wt, wt)


def _in_proj_kernel(src_ref, x_ref, g_ref, w_ref, kr_ref, o_ref, hn_ref, *, row_chunk):
    del src_ref
    j = pl.program_id(1)

    @pl.when(j == 0)
    def _():
        def body(c, carry):
            r0 = pl.multiple_of(c * row_chunk, row_chunk)
            x = x_ref[pl.ds(r0, row_chunk), :]
            hn = x * _rms_scale(x, D_MODEL) * g_ref[...]
            hn_ref[pl.ds(r0, row_chunk), :] = hn.astype(hn_ref.dtype)
            return carry
        lax.fori_loop(0, x_ref.shape[0] // row_chunk, body, 0, unroll=4)

    def project(w):
        half = o_ref.shape[0] // 2
        for r in (0, half):
            o_ref[r:r + half, :] = lax.dot_general(hn_ref[r:r + half, :], w, (((1,), (1,)), ((), ())),
                                                   preferred_element_type=jnp.float32).astype(o_ref.dtype)

    @pl.when(j != 1)
    def _():
        project(w_ref[...].astype(jnp.bfloat16))

    @pl.when(j == 1)
    def _():
        half = ROPE // 2
        kr = kr_ref[...].astype(jnp.bfloat16)
        z = jnp.zeros((half, D_MODEL), jnp.bfloat16)
        project(jnp.concatenate([kr[:half], z, kr[half:], z, jnp.zeros((HEAD, D_MODEL), jnp.bfloat16),
                                 w_ref[:2 * HEAD, :].astype(jnp.bfloat16)], axis=0))


def _in_proj(x2, gain, wt, *, tm=2048):
    m = x2.shape[0]
    assert all(s % ROPE == 0 for s in _W_TILE_SRC)
    src = jnp.asarray([s // ROPE for s in _W_TILE_SRC], jnp.int32)
    return pl.pallas_call(
        functools.partial(_in_proj_kernel, row_chunk=64),
        grid_spec=pltpu.PrefetchScalarGridSpec(
            num_scalar_prefetch=1,
            grid=(m // tm, PROJ_COLS // W_TILE),
            in_specs=[
                pl.BlockSpec((tm, D_MODEL), lambda i, j, src: (i, 0)),
                pl.BlockSpec((1, D_MODEL), lambda i, j, src: (0, 0)),
                pl.BlockSpec((pl.Element(W_TILE), pl.Element(D_MODEL)), lambda i, j, src: (src[j] * ROPE, 0)),
                pl.BlockSpec((pl.Element(ROPE), pl.Element(D_MODEL)), lambda i, j, src: (_O_KR, 0)),
            ],
            out_specs=pl.BlockSpec((tm, W_TILE), lambda i, j, src: (i, j)),
            scratch_shapes=[pltpu.VMEM((tm, D_MODEL), jnp.bfloat16)],
        ),
        out_shape=jax.ShapeDtypeStruct((m, PROJ_COLS), jnp.bfloat16),
        compiler_params=pltpu.CompilerParams(
            dimension_semantics=("arbitrary", "arbitrary"),
            vmem_limit_bytes=60 * 1024 * 1024),
        name="in_proj",
    )(src, x2, gain, wt, wt)


def _rope_pad(x, cos, sin):
    return x * cos + pltpu.roll(x, 64, 1) * sin


def _mla_prep_kernel(lat_ref, wq_ref, wkv_ref, gqa_ref, gkva_ref, gq_ref, gk_ref,
                     cos_ref, sin_ref, q_ref, k_ref, v_ref):
    cos = cos_ref[...]
    sin = sin_ref[...]
    gq = gq_ref[...]
    gk = gk_ref[...]

    c_kv = lat_ref[:, 0:KV_RANK].astype(jnp.float32)
    cn = (c_kv * _rms_scale(c_kv, KV_RANK) * gkva_ref[...]).astype(jnp.bfloat16)
    kv = jnp.dot(cn, wkv_ref[...], preferred_element_type=jnp.float32)
    ones = jnp.ones((kv.shape[0], HEAD), v_ref.dtype)
    for h in range(N_HEADS):
        v_ref[:, 2 * h * HEAD:(2 * h + 1) * HEAD] = kv[:, WIDTH + h * HEAD:WIDTH + (h + 1) * HEAD].astype(v_ref.dtype)
        v_ref[:, (2 * h + 1) * HEAD:(2 * h + 2) * HEAD] = ones

    k_r = lat_ref[:, KV_RANK:KV_RANK + HEAD].astype(jnp.float32)
    k_r = k_r * _rms_scale(k_r, ROPE) * gk[:, HEAD:]
    k_rope = _rope_pad(k_r, cos, sin).astype(k_ref.dtype)
    for h in range(N_HEADS):
        kn = kv[:, h * HEAD:(h + 1) * HEAD]
        kn = kn * _rms_scale(kn, HEAD) * gk[:, :HEAD]
        k_ref[:, 2 * h * HEAD:(2 * h + 1) * HEAD] = kn.astype(k_ref.dtype)
        k_ref[:, (2 * h + 1) * HEAD:(2 * h + 2) * HEAD] = k_rope

    c_q = lat_ref[:, 2 * HEAD + KV_RANK:].astype(jnp.float32)
    cqn = (c_q * _rms_scale(c_q, Q_RANK) * gqa_ref[...]).astype(jnp.bfloat16)
    q = jnp.dot(cqn, wq_ref[...], preferred_element_type=jnp.float32)
    for h in range(N_HEADS):
        qn = q[:, 2 * h * HEAD:(2 * h + 1) * HEAD]
        qn = qn * _rms_scale(qn, HEAD) * gq[:, :HEAD]
        q_ref[:, 2 * h * HEAD:(2 * h + 1) * HEAD] = qn.astype(q_ref.dtype)
        qr = q[:, (2 * h + 1) * HEAD:(2 * h + 2) * HEAD]
        qr = qr * _rms_scale(qr, ROPE) * gq[:, HEAD:]
        q_ref[:, (2 * h + 1) * HEAD:(2 * h + 2) * HEAD] = _rope_pad(qr, cos, sin).astype(q_ref.dtype)


def _mla_prep(proj, wq, wkv, gqa, gkva, gq, gk, cos, sin, *, seq, tm=256):
    m = proj.shape[0]
    nseq = seq // tm
    const = lambda i: (0, 0)
    return pl.pallas_call(
        _mla_prep_kernel,
        grid=(m // tm,),
        in_specs=[
            pl.BlockSpec((tm, LAT_COLS), lambda i: (i, 0)),
            pl.BlockSpec(wq.shape, const),
            pl.BlockSpec(wkv.shape, const),
            pl.BlockSpec(gqa.shape, const),
            pl.BlockSpec(gkva.shape, const),
            pl.BlockSpec(gq.shape, const),
            pl.BlockSpec(gk.shape, const),
            pl.BlockSpec((tm, HEAD), lambda i: (i % nseq, 0)),
            pl.BlockSpec((tm, HEAD), lambda i: (i % nseq, 0)),
        ],
        out_specs=[
            pl.BlockSpec((tm, 2 * WIDTH), lambda i: (i, 0)),
            pl.BlockSpec((tm, 2 * WIDTH), lambda i: (i, 0)),
            pl.BlockSpec((tm, 2 * WIDTH), lambda i: (i, 0)),
        ],
        out_shape=[
            jax.ShapeDtypeStruct((m, 2 * WIDTH), jnp.bfloat16),
            jax.ShapeDtypeStruct((m, 2 * WIDTH), jnp.bfloat16),
            jax.ShapeDtypeStruct((m, 2 * WIDTH), jnp.bfloat16),
        ],
        compiler_params=pltpu.CompilerParams(
            dimension_semantics=("parallel",), vmem_limit_bytes=VMEM_LIMIT),
        name="mla_prep",
    )(proj, wq, wkv, gqa, gkva, gq, gk, cos, sin)


def _mla_attn_kernel(q_ref, k_ref, v_ref, o_ref, m_ref, acc_ref, sa_ref, sb_ref, *, tq, tk, heads):
    assert tq == tk
    i = pl.program_id(2)
    m_ref[...] = jnp.full(m_ref.shape, NEG, jnp.float32)
    acc_ref[...] = jnp.zeros(acc_ref.shape, jnp.float32)

    def scores(j, s_ref):
        k0 = pl.multiple_of(j * tk, tk)
        for h in range(heads):
            cols = slice(2 * h * HEAD, 2 * (h + 1) * HEAD)
            s_ref[h] = lax.dot_general(q_ref[:, cols], k_ref[pl.ds(k0, tk), cols],
                                       (((1,), (1,)), ((), ())),
                                       preferred_element_type=jnp.float32)

    def update(h, rows, s, v):
        m_prev = m_ref[h, rows, :]
        m_new = jnp.maximum(m_prev, jnp.max(s, axis=-1, keepdims=True))
        alpha = jnp.exp2(m_prev - m_new)
        p = jnp.exp2(s - jnp.tile(m_new, (1, s.shape[1] // HEAD)))
        pv = jnp.dot(p.astype(v.dtype), v, preferred_element_type=jnp.float32)
        acc_ref[h, rows, :] = jnp.tile(alpha, (1, 2)) * acc_ref[h, rows, :] + pv
        m_ref[h, rows, :] = m_new

    def softmax_pv(j, s_ref, masked):
        k0 = pl.multiple_of(j * tk, tk)
        hq = tq // 2
        for h in range(heads):
            cols = slice(2 * h * HEAD, 2 * (h + 1) * HEAD)
            if not masked:
                update(h, slice(None), s_ref[h], v_ref[pl.ds(k0, tk), cols])
                continue
            row = lax.broadcasted_iota(jnp.int32, (hq, hq), 0)
            col = lax.broadcasted_iota(jnp.int32, (hq, hq), 1)
            tri = col <= row
            s0 = jnp.where(tri, s_ref[h, 0:hq, 0:hq], NEG)
            update(h, slice(0, hq), s0, v_ref[pl.ds(k0, hq), cols])
            s1 = jnp.concatenate([s_ref[h, hq:, 0:hq], jnp.where(tri, s_ref[h, hq:, hq:], NEG)], axis=1)
            update(h, slice(hq, tq), s1, v_ref[pl.ds(k0, tk), cols])

    scores(0, sa_ref)

    def body(t, carry):
        j = 2 * t
        scores(j + 1, sb_ref)
        softmax_pv(j, sa_ref, False)
        scores(j + 2, sa_ref)
        softmax_pv(j + 1, sb_ref, False)
        return carry
    lax.fori_loop(0, i // 2, body, 0)

    @pl.when(i % 2 == 0)
    def _():
        softmax_pv(i, sa_ref, True)

    @pl.when(i % 2 == 1)
    def _():
        scores(i, sb_ref)
        softmax_pv(i - 1, sa_ref, False)
        softmax_pv(i, sb_ref, True)

    for h in range(heads):
        acc = acc_ref[h]
        o_ref[:, h * HEAD:(h + 1) * HEAD] = (acc[:, :HEAD] / acc[:, HEAD:]).astype(o_ref.dtype)


def _mla_attn(q, k, v, *, batch, seq, tq=512, tk=512, heads=4):
    nq = seq // tq
    w = 2 * HEAD * heads
    return pl.pallas_call(
        functools.partial(_mla_attn_kernel, tq=tq, tk=tk, heads=heads),
        grid=(batch, N_HEADS // heads, nq),
        in_specs=[
            pl.BlockSpec((tq, w), lambda b, h, i: (b * nq + i, h)),
            pl.BlockSpec((seq, w), lambda b, h, i: (b, h)),
            pl.BlockSpec((seq, w), lambda b, h, i: (b, h)),
        ],
        out_specs=pl.BlockSpec((tq, HEAD * heads), lambda b, h, i: (b * nq + i, h)),
        out_shape=jax.ShapeDtypeStruct((batch * seq, WIDTH), jnp.bfloat16),
        scratch_shapes=[
            pltpu.VMEM((heads, tq, HEAD), jnp.float32),
            pltpu.VMEM((heads, tq, 2 * HEAD), jnp.float32),
            pltpu.VMEM((heads, tq, tk), jnp.float32),
            pltpu.VMEM((heads, tq, tk), jnp.float32),
        ],
        compiler_params=pltpu.CompilerParams(
            dimension_semantics=("parallel", "parallel", "arbitrary"),
            vmem_limit_bytes=VMEM_LIMIT),
        name="mla_attn",
    )(q, k, v)


def _dil_attn_kernel(q_ref, k_ref, v_ref, cos_ref, sin_ref, gq_ref, gk_ref, o_ref,
                     qd_ref, kd_ref, vd_ref, on_ref, ls_ref, bias_ref, st_ref, *, seq, unroll):
    chunk = 1024
    npat = len(DILATIONS)
    assert DILATIONS == (1, 4, 16)
    nat = [on_ref.at[a] for a in range(3)]
    by4 = [ls_ref.at[a] for a in range(3)]
    dst = [qd_ref, kd_ref, vd_ref]
    lead = [0, QBLK, QBLK]

    @pl.when((pl.program_id(0) == 0) & (pl.program_id(1) == 0))
    def _():
        qi = lax.broadcasted_iota(jnp.int32, (QBLK, 2 * QBLK), 0)
        kj = lax.broadcasted_iota(jnp.int32, (QBLK, 2 * QBLK), 1)
        band = (kj >= qi) & (kj <= qi + N_BACK)
        bias_ref[0] = jnp.where(band, 0.0, NEG)
        bias_ref[1] = jnp.where(band & (kj >= QBLK), 0.0, NEG)
        for p in range(npat):
            vd_ref[p, QBLK:, HEAD:] = jnp.ones((seq, HEAD), vd_ref.dtype)
            kd_ref[p, 0:QBLK, :] = jnp.zeros((QBLK, HEAD), kd_ref.dtype)
            vd_ref[p, 0:QBLK, :] = jnp.zeros((QBLK, 2 * HEAD), vd_ref.dtype)

    def prep(c, carry):
        r0 = pl.multiple_of(c * chunk, chunk)
        cos = cos_ref[pl.ds(r0, chunk), :]
        sin = sin_ref[pl.ds(r0, chunk), :]
        q = q_ref[pl.ds(r0, chunk), :].astype(jnp.float32)
        q = q * _rms_scale_mxu(q, HEAD) * gq_ref[...]
        q = q * cos + pltpu.roll(q, 64, 1) * sin
        k = k_ref[pl.ds(r0, chunk), :].astype(jnp.float32)
        k = k * _rms_scale_mxu(k, HEAD) * gk_ref[...]
        k = k * cos + pltpu.roll(k, 64, 1) * sin
        v = v_ref[pl.ds(r0, chunk), :]
        nat[0][pl.ds(r0, chunk), :] = q
        nat[1][pl.ds(r0, chunk), :] = k
        nat[2][pl.ds(r0, chunk), :] = v.astype(jnp.float32)
        qd_ref[0, pl.ds(r0, chunk), :] = q.astype(qd_ref.dtype)
        kd_ref[0, pl.ds(QBLK + r0, chunk), :] = k.astype(kd_ref.dtype)
        vd_ref[0, pl.ds(QBLK + r0, chunk), 0:HEAD] = v
        return carry
    lax.fori_loop(0, seq // chunk, prep, 0)

    sub4, sub16 = seq // 4, seq // 16
    for a in range(3):
        for r4 in range(4):
            x = nat[a][pl.ds(r4, sub4, stride=4), :]
            by4[a][r4 * sub4:(r4 + 1) * sub4, :] = x
            dst[a][1, lead[a] + r4 * sub4:lead[a] + (r4 + 1) * sub4, 0:HEAD] = x.astype(dst[a].dtype)
    for a in range(3):
        for r4 in range(4):
            for j in range(4):
                r16 = r4 + 4 * j
                x = by4[a][pl.ds(r4 * sub4 + j, sub16, stride=4), :]
                dst[a][2, lead[a] + r16 * sub16:lead[a] + (r16 + 1) * sub16, 0:HEAD] = x.astype(dst[a].dtype)

    for p, d in enumerate(DILATIONS):
        nb = seq // d // QBLK
        shift = int(math.log2(nb))

        def block(g, carry, p=p, d=d, nb=nb, shift=shift):
            g0 = pl.multiple_of(g * QBLK, QBLK)
            q = qd_ref[p, pl.ds(g0, QBLK), :]
            k = kd_ref[p, pl.ds(g0, 2 * QBLK), :]
            v = vd_ref[p, pl.ds(g0, 2 * QBLK), :]
            n = g & (nb - 1)
            s = lax.dot_general(q, k, (((1,), (1,)), ((), ())), preferred_element_type=jnp.float32)
            s = s + bias_ref[jnp.where(n == 0, 1, 0)]
            m = jnp.max(s, axis=-1, keepdims=True)
            e = jnp.exp2(s - m)
            pv = jnp.dot(e.astype(v.dtype), v, preferred_element_type=jnp.float32)
            den = pv[:, HEAD:]
            o = pv[:, :HEAD] / den
            lse = m + jnp.log2(den)
            r = g >> shift
            if d == 16:
                rows = pl.ds((r & 3) * sub4 + (r >> 2) + n * (QBLK * 4), QBLK, stride=4)
                st_ref[0, rows, :] = o
                st_ref[1, rows, :] = lse
            else:
                rows = pl.ds(r + n * (QBLK * d), QBLK, stride=d) if d > 1 else pl.ds(g0, QBLK)
                on_ref[p, rows, :] = o
                ls_ref[p, rows, :] = lse
            return carry
        lax.fori_loop(0, seq // QBLK, block, 0, unroll=unroll)

    for a, out in enumerate((on_ref, ls_ref)):
        for r4 in range(4):
            out[2, pl.ds(r4, sub4, stride=4), :] = st_ref[a, r4 * sub4:(r4 + 1) * sub4, :]

    def mix(c, carry):
        r0 = pl.multiple_of(c * chunk, chunk)
        ls = [ls_ref[p, pl.ds(r0, chunk), :] for p in range(npat)]
        mx = functools.reduce(jnp.maximum, ls)
        w = [jnp.exp2(l - mx) for l in ls]
        num = sum(w[p] * on_ref[p, pl.ds(r0, chunk), :] for p in range(npat))
        o_ref[pl.ds(r0, chunk), :] = (num / sum(w)).astype(o_ref.dtype)
        return carry
    lax.fori_loop(0, seq // chunk, mix, 0)


def _dil_attn(proj, cos, sin, gq, gk, *, batch, seq):
    npat = len(DILATIONS)
    col = HEAD

    def qcol(h):
        return jnp.where(h < 4, 1536 // col + h, 4096 // col - 4 + h)

    const = lambda b, h: (0, 0)
    return pl.pallas_call(
        functools.partial(_dil_attn_kernel, seq=seq, unroll=16),
        grid=(batch, N_HEADS),
        in_specs=[
            pl.BlockSpec((seq, HEAD), lambda b, h: (b, qcol(h))),
            pl.BlockSpec((seq, HEAD), lambda b, h: (b, 4608 // col + h)),
            pl.BlockSpec((seq, HEAD), lambda b, h: (b, 5632 // col + h)),
            pl.BlockSpec((seq, HEAD), const),
            pl.BlockSpec((seq, HEAD), const),
            pl.BlockSpec((1, HEAD), const),
            pl.BlockSpec((1, HEAD), const),
        ],
        out_specs=pl.BlockSpec((seq, HEAD), lambda b, h: (b, h)),
        out_shape=jax.ShapeDtypeStruct((batch * seq, WIDTH), jnp.bfloat16),
        scratch_shapes=[
            pltpu.VMEM((npat, seq, HEAD), jnp.bfloat16),
            pltpu.VMEM((npat, seq + QBLK, HEAD), jnp.bfloat16),
            pltpu.VMEM((npat, seq + QBLK, 2 * HEAD), jnp.bfloat16),
            pltpu.VMEM((npat, seq, HEAD), jnp.float32),
            pltpu.VMEM((npat, seq, HEAD), jnp.float32),
            pltpu.VMEM((2, QBLK, 2 * QBLK), jnp.float32),
            pltpu.VMEM((2, seq, HEAD), jnp.float32),
        ],
        compiler_params=pltpu.CompilerParams(
            dimension_semantics=("arbitrary", "arbitrary"), vmem_limit_bytes=VMEM_LIMIT),
        name="dil_attn",
    )(proj, proj, proj, cos, sin, gq, gk)


def _out_proj_kernel(oa_ref, ob_ref, g_ref, x_ref, ga_ref, gb_ref, w_ref, o_ref, *, nsplit):
    def branch(o, gain_ref, gate):
        o = o.astype(jnp.float32)
        gate = gate.astype(jnp.float32)
        y = o * _rms_scale(o, WIDTH) * gain_ref[...]
        return (y * (gate / (1.0 + jnp.exp(-gate)))).astype(jnp.bfloat16)

    step = o_ref.shape[0] // nsplit
    for r in range(0, o_ref.shape[0], step):
        rows = slice(r, r + step)
        ya = branch(oa_ref[rows, :], ga_ref, g_ref[rows, :WIDTH])
        yb = branch(ob_ref[rows, :], gb_ref, g_ref[rows, WIDTH:])
        y = jnp.concatenate([ya, yb], axis=-1)
        o_ref[rows, :] = x_ref[rows, :] + jnp.dot(y, w_ref[...], preferred_element_type=jnp.float32)


def _out_proj(o_a, o_b, proj, x2, ga, gb, w, *, tm=512):
    m = x2.shape[0]
    const = lambda i: (0, 0)
    return pl.pallas_call(
        functools.partial(_out_proj_kernel, nsplit=2),
        grid=(m // tm,),
        in_specs=[
            pl.BlockSpec((tm, WIDTH), lambda i: (i, 0)),
            pl.BlockSpec((tm, WIDTH), lambda i: (i, 0)),
            pl.BlockSpec((tm, 2 * WIDTH), lambda i: (i, 1)),
            pl.BlockSpec((tm, D_MODEL), lambda i: (i, 0)),
            pl.BlockSpec((1, WIDTH), const),
            pl.BlockSpec((1, WIDTH), const),
            pl.BlockSpec((2 * WIDTH, D_MODEL), const),
        ],
        out_specs=pl.BlockSpec((tm, D_MODEL), lambda i: (i, 0)),
        out_shape=jax.ShapeDtypeStruct((m, D_MODEL), jnp.float32),
        compiler_params=pltpu.CompilerParams(
            dimension_semantics=("parallel",), vmem_limit_bytes=VMEM_LIMIT),
        name="out_proj",
    )(o_a, o_b, proj, x2, ga, gb, w)


def _pad_rope_cols(a):
    z = jnp.zeros(a.shape[:-1] + (ROPE // 2,), a.dtype)
    return jnp.concatenate([a[..., :ROPE // 2], z, a[..., ROPE // 2:], z], axis=-1)


def _rope_tables(seq, d, padded):
    inv = ROPE_THETA ** (-np.arange(0, d, 2, dtype=np.float64) / d)
    ang = np.arange(seq, dtype=np.float64)[:, None] * inv[None, :]
    cos, sin = np.cos(ang), np.sin(ang)
    cos2 = np.concatenate([cos, cos], axis=-1)
    sin2 = np.concatenate([-sin, sin], axis=-1)
    if padded:
        z = np.zeros((seq, d // 2))
        cos2 = np.concatenate([cos, z, cos, z], axis=-1)
        sin2 = np.concatenate([-sin, z, sin, z], axis=-1)
    return jnp.asarray(cos2, jnp.float32), jnp.asarray(sin2, jnp.float32)


def kernel(x, norm_gain, w_in, q_a_norm_gain, kv_a_norm_gain, w_uq, w_ukv, mla_q_norm_gain,
           mla_k_norm_gain, dil_q_norm_gain, dil_k_norm_gain, mla_out_norm_gain,
           dil_out_norm_gain, w_out):
    batch, seq, _ = x.shape
    depth = w_in.shape[0]
    bf = jnp.bfloat16
    cos_d, sin_d = _rope_tables(seq, HEAD, padded=False)
    cos_m, sin_m = _rope_tables(seq, ROPE, padded=True)
    h2 = x.reshape(batch * seq, D_MODEL)

    for l in range(depth):
        wq = w_uq[l].reshape(Q_RANK, N_HEADS, MLA_QK)
        wq_r = jnp.concatenate([wq[..., :HEAD], _pad_rope_cols(wq[..., HEAD:])], axis=-1)
        wq_r = wq_r.reshape(Q_RANK, N_HEADS * 2 * HEAD).astype(bf)
        wkv = w_ukv[l].reshape(KV_RANK, N_HEADS, 2 * HEAD)
        wkv_r = jnp.concatenate([wkv[..., :HEAD].reshape(KV_RANK, WIDTH),
                                 wkv[..., HEAD:].reshape(KV_RANK, WIDTH)], axis=-1).astype(bf)

        gq_m = mla_q_norm_gain[l]
        gk_m = mla_k_norm_gain[l]
        gq_full = jnp.concatenate([gq_m[:HEAD], _pad_rope_cols(gq_m[HEAD:])])[None, :]
        gq_full = gq_full * (math.log2(math.e) / math.sqrt(MLA_QK))
        gk_full = jnp.concatenate([gk_m[:HEAD], _pad_rope_cols(gk_m[HEAD:])])[None, :]
        gq_d = dil_q_norm_gain[l][None, :] * (math.log2(math.e) / math.sqrt(HEAD))
        gk_d = dil_k_norm_gain[l][None, :]

        proj = _in_proj(h2, norm_gain[l][None, :], jnp.swapaxes(w_in[l], 0, 1))
        q_f, k_f, v_a = _mla_prep(proj, wq_r, wkv_r, q_a_norm_gain[l][None, :],
                                  kv_a_norm_gain[l][None, :], gq_full, gk_full, cos_m, sin_m, seq=seq)
        o_a = _mla_attn(q_f, k_f, v_a, batch=batch, seq=seq)
        o_b = _dil_attn(proj, cos_d, sin_d, gq_d, gk_d, batch=batch, seq=seq)
        h2 = _out_proj(o_a, o_b, proj, h2, mla_out_norm_gain[l][None, :],
                       dil_out_norm_gain[l][None, :], w_out[l].astype(bf))
    return h2.reshape(batch, seq, D_MODEL)
```

```python
import functools
import math

import jax
import jax.numpy as jnp
import numpy as np
from jax import lax
from jax.experimental import pallas as pl
from jax.experimental.pallas import tpu as pltpu

EPS = 1e-6
ROPE_THETA = 10000.0
NEG = -1e30

D_MODEL = 2048
N_HEADS = 8
HEAD = 128
ROPE = 64
Q_RANK = 768
KV_RANK = 512
WIDTH = N_HEADS * HEAD
MLA_QK = HEAD + ROPE
QBLK = 128
N_BACK = 128
DILATIONS = (1, 4, 16)

PROJ_COLS = 6656
LAT_COLS = 1536
VMEM_LIMIT = 56 * 1024 * 1024


def _rms_scale(x, n):
    return lax.rsqrt(jnp.sum(x * x, axis=-1, keepdims=True) * (1.0 / n) + EPS)


def _rms_scale_mxu(x, n):
    ones = jnp.ones((x.shape[-1], x.shape[-1]), jnp.bfloat16)
    ss = jnp.dot((x * x).astype(jnp.bfloat16), ones, preferred_element_type=jnp.float32)
    return lax.rsqrt(ss * (1.0 / n) + EPS)


_O_CQ, _O_CKV, _O_KR = 0, Q_RANK, Q_RANK + KV_RANK
_O_GA = _O_KR + ROPE
_O_QB = _O_GA + WIDTH
_O_KB, _O_VB, _O_GB = _O_QB + WIDTH, _O_QB + 2 * WIDTH, _O_QB + 3 * WIDTH
_IN_COLS = _O_GB + WIDTH


W_TILE = 512
_W_TILE_SRC = (_O_CKV, _O_CQ, _O_CQ + 256, _O_QB, _O_GA, _O_GA + 512, _O_GB, _O_GB + 512,
               _O_QB + 512, _O_KB, _O_KB + 512, _O_VB, _O_VB + 512)


def _w_prep_kernel(src_ref, w_ref, kr_ref, o_ref):
    del src_ref
    o_ref[...] = w_ref[...].astype(o_ref.dtype)

    @pl.when(pl.program_id(0) == 1)
    def _():
        half = ROPE // 2
        zeros = jnp.zeros((half, D_MODEL), o_ref.dtype)
        o_ref[2 * HEAD:, :] = w_ref[:2 * HEAD, :].astype(o_ref.dtype)
        o_ref[0:half, :] = kr_ref[0:half, :].astype(o_ref.dtype)
        o_ref[half:2 * half, :] = zeros
        o_ref[2 * half:3 * half, :] = kr_ref[half:, :].astype(o_ref.dtype)
        o_ref[3 * half:4 * half, :] = zeros
        o_ref[HEAD:2 * HEAD, :] = jnp.zeros((HEAD, D_MODEL), o_ref.dtype)


def _w_prep(wt):
    assert all(s % ROPE == 0 for s in _W_TILE_SRC)
    src = jnp.asarray([s // ROPE for s in _W_TILE_SRC], jnp.int32)
    return pl.pallas_call(
        _w_prep_kernel,
        grid_spec=pltpu.PrefetchScalarGridSpec(
            num_scalar_prefetch=1,
            grid=(PROJ_COLS // W_TILE,),
            in_specs=[
                pl.BlockSpec((pl.Element(W_TILE), pl.Element(D_MODEL)), lambda t, src: (src[t] * ROPE, 0)),
                pl.BlockSpec((pl.Element(ROPE), pl.Element(D_MODEL)), lambda t, src: (_O_KR, 0)),
            ],
            out_specs=pl.BlockSpec((W_TILE, D_MODEL), lambda t, src: (t, 0)),
        ),
        out_shape=jax.ShapeDtypeStruct((PROJ_COLS, D_MODEL), jnp.bfloat16),
        compiler_params=pltpu.CompilerParams(
            dimension_semantics=("arbitrary",), vmem_limit_bytes=VMEM_LIMIT),
        name="w_prep",
    )(src, wt, wt)


def _in_proj_kernel(src_ref, x_ref, g_ref, w_ref, kr_ref, o_ref, hn_ref, *, row_chunk):
    del src_ref
    j = pl.program_id(1)

    @pl.when(j == 0)
    def _():
        def body(c, carry):
            r0 = pl.multiple_of(c * row_chunk, row_chunk)
            x = x_ref[pl.ds(r0, row_chunk), :]
            hn = x * _rms_scale(x, D_MODEL) * g_ref[...]
            hn_ref[pl.ds(r0, row_chunk), :] = hn.astype(hn_ref.dtype)
            return carry
        lax.fori_loop(0, x_ref.shape[0] // row_chunk, body, 0, unroll=4)

    def project(w):
        half = o_ref.shape[0] // 2
        for r in (0, half):
            o_ref[r:r + half, :] = lax.dot_general(hn_ref[r:r + half, :], w, (((1,), (1,)), ((), ())),
                                                   preferred_element_type=jnp.float32).astype(o_ref.dtype)

    @pl.when(j != 1)
    def _():
        project(w_ref[...].astype(jnp.bfloat16))

    @pl.when(j == 1)
    def _():
        half = ROPE // 2
        kr = kr_ref[...].astype(jnp.bfloat16)
        z = jnp.zeros((half, D_MODEL), jnp.bfloat16)
        project(jnp.concatenate([kr[:half], z, kr[half:], z, jnp.zeros((HEAD, D_MODEL), jnp.bfloat16),
                                 w_ref[:2 * HEAD, :].astype(jnp.bfloat16)], axis=0))


def _in_proj(x2, gain, wt, *, tm=2048):
    m = x2.shape[0]
    assert all(s % ROPE == 0 for s in _W_TILE_SRC)
    src = jnp.asarray([s // ROPE for s in _W_TILE_SRC], jnp.int32)
    return pl.pallas_call(
        functools.partial(_in_proj_kernel, row_chunk=64),
        grid_spec=pltpu.PrefetchScalarGridSpec(
            num_scalar_prefetch=1,
            grid=(m // tm, PROJ_COLS // W_TILE),
            in_specs=[
                pl.BlockSpec((tm, D_MODEL), lambda i, j, src: (i, 0)),
                pl.BlockSpec((1, D_MODEL), lambda i, j, src: (0, 0)),
                pl.BlockSpec((pl.Element(W_TILE), pl.Element(D_MODEL)), lambda i, j, src: (src[j] * ROPE, 0)),
                pl.BlockSpec((pl.Element(ROPE), pl.Element(D_MODEL)), lambda i, j, src: (_O_KR, 0)),
            ],
            out_specs=pl.BlockSpec((tm, W_TILE), lambda i, j, src: (i, j)),
            scratch_shapes=[pltpu.VMEM((tm, D_MODEL), jnp.bfloat16)],
        ),
        out_shape=jax.ShapeDtypeStruct((m, PROJ_COLS), jnp.bfloat16),
        compiler_params=pltpu.CompilerParams(
            dimension_semantics=("arbitrary", "arbitrary"),
            vmem_limit_bytes=60 * 1024 * 1024),
        name="in_proj",
    )(src, x2, gain, wt, wt)


def _rope_pad(x, cos, sin):
    return x * cos + pltpu.roll(x, 64, 1) * sin


def _mla_prep_kernel(lat_ref, wq_ref, wkv_ref, gqa_ref, gkva_ref, gq_ref, gk_ref,
                     cos_ref, sin_ref, q_ref, k_ref, v_ref):
    cos = cos_ref[...]
    sin = sin_ref[...]
    gq = gq_ref[...]
    gk = gk_ref[...]

    c_kv = lat_ref[:, 0:KV_RANK].astype(jnp.float32)
    cn = (c_kv * _rms_scale(c_kv, KV_RANK) * gkva_ref[...]).astype(jnp.bfloat16)
    kv = jnp.dot(cn, wkv_ref[...], preferred_element_type=jnp.float32)
    ones = jnp.ones((kv.shape[0], HEAD), v_ref.dtype)
    for h in range(N_HEADS):
        v_ref[:, 2 * h * HEAD:(2 * h + 1) * HEAD] = kv[:, WIDTH + h * HEAD:WIDTH + (h + 1) * HEAD].astype(v_ref.dtype)
        v_ref[:, (2 * h + 1) * HEAD:(2 * h + 2) * HEAD] = ones

    k_r = lat_ref[:, KV_RANK:KV_RANK + HEAD].astype(jnp.float32)
    k_r = k_r * _rms_scale(k_r, ROPE) * gk[:, HEAD:]
    k_rope = _rope_pad(k_r, cos, sin).astype(k_ref.dtype)
    for h in range(N_HEADS):
        kn = kv[:, h * HEAD:(h + 1) * HEAD]
        kn = kn * _rms_scale(kn, HEAD) * gk[:, :HEAD]
        k_ref[:, 2 * h * HEAD:(2 * h + 1) * HEAD] = kn.astype(k_ref.dtype)
        k_ref[:, (2 * h + 1) * HEAD:(2 * h + 2) * HEAD] = k_rope

    c_q = lat_ref[:, 2 * HEAD + KV_RANK:].astype(jnp.float32)
    cqn = (c_q * _rms_scale(c_q, Q_RANK) * gqa_ref[...]).astype(jnp.bfloat16)
    q = jnp.dot(cqn, wq_ref[...], preferred_element_type=jnp.float32)
    for h in range(N_HEADS):
        qn = q[:, 2 * h * HEAD:(2 * h + 1) * HEAD]
        qn = qn * _rms_scale(qn, HEAD) * gq[:, :HEAD]
        q_ref[:, 2 * h * HEAD:(2 * h + 1) * HEAD] = qn.astype(q_ref.dtype)
        qr = q[:, (2 * h + 1) * HEAD:(2 * h + 2) * HEAD]
        qr = qr * _rms_scale(qr, ROPE) * gq[:, HEAD:]
        q_ref[:, (2 * h + 1) * HEAD:(2 * h + 2) * HEAD] = _rope_pad(qr, cos, sin).astype(q_ref.dtype)


def _mla_prep(proj, wq, wkv, gqa, gkva, gq, gk, cos, sin, *, seq, tm=256):
    m = proj.shape[0]
    nseq = seq // tm
    const = lambda i: (0, 0)
    return pl.pallas_call(
        _mla_prep_kernel,
        grid=(m // tm,),
        in_specs=[
            pl.BlockSpec((tm, LAT_COLS), lambda i: (i, 0)),
            pl.BlockSpec(wq.shape, const),
            pl.BlockSpec(wkv.shape, const),
            pl.BlockSpec(gqa.shape, const),
            pl.BlockSpec(gkva.shape, const),
            pl.BlockSpec(gq.shape, const),
            pl.BlockSpec(gk.shape, const),
            pl.BlockSpec((tm, HEAD), lambda i: (i % nseq, 0)),
            pl.BlockSpec((tm, HEAD), lambda i: (i % nseq, 0)),
        ],
        out_specs=[
            pl.BlockSpec((tm, 2 * WIDTH), lambda i: (i, 0)),
            pl.BlockSpec((tm, 2 * WIDTH), lambda i: (i, 0)),
            pl.BlockSpec((tm, 2 * WIDTH), lambda i: (i, 0)),
        ],
        out_shape=[
            jax.ShapeDtypeStruct((m, 2 * WIDTH), jnp.bfloat16),
            jax.ShapeDtypeStruct((m, 2 * WIDTH), jnp.bfloat16),
            jax.ShapeDtypeStruct((m, 2 * WIDTH), jnp.bfloat16),
        ],
        compiler_params=pltpu.CompilerParams(
            dimension_semantics=("parallel",), vmem_limit_bytes=VMEM_LIMIT),
        name="mla_prep",
    )(proj, wq, wkv, gqa, gkva, gq, gk, cos, sin)


def _mla_attn_kernel(q_ref, k_ref, v_ref, o_ref, m_ref, acc_ref, sa_ref, sb_ref, *, tq, tk, heads):
    assert tq == tk
    i = pl.program_id(2)
    m_ref[...] = jnp.full(m_ref.shape, NEG, jnp.float32)
    acc_ref[...] = jnp.zeros(acc_ref.shape, jnp.float32)

    def scores(j, s_ref):
        k0 = pl.multiple_of(j * tk, tk)
        for h in range(heads):
            cols = slice(2 * h * HEAD, 2 * (h + 1) * HEAD)
            s_ref[h] = lax.dot_general(q_ref[:, cols], k_ref[pl.ds(k0, tk), cols],
                                       (((1,), (1,)), ((), ())),
                                       preferred_element_type=jnp.float32)

    def update(h, rows, s, v):
        m_prev = m_ref[h, rows, :]
        m_new = jnp.maximum(m_prev, jnp.max(s, axis=-1, keepdims=True))
        alpha = jnp.exp2(m_prev - m_new)
        p = jnp.exp2(s - jnp.tile(m_new, (1, s.shape[1] // HEAD)))
        pv = jnp.dot(p.astype(v.dtype), v, preferred_element_type=jnp.float32)
        acc_ref[h, rows, :] = jnp.tile(alpha, (1, 2)) * acc_ref[h, rows, :] + pv
        m_ref[h, rows, :] = m_new

    def softmax_pv(j, s_ref, masked):
        k0 = pl.multiple_of(j * tk, tk)
        hq = tq // 2
        for h in range(heads):
            cols = slice(2 * h * HEAD, 2 * (h + 1) * HEAD)
            if not masked:
                update(h, slice(None), s_ref[h], v_ref[pl.ds(k0, tk), cols])
                continue
            row = lax.broadcasted_iota(jnp.int32, (hq, hq), 0)
            col = lax.broadcasted_iota(jnp.int32, (hq, hq), 1)
            tri = col <= row
            s0 = jnp.where(tri, s_ref[h, 0:hq, 0:hq], NEG)
            update(h, slice(0, hq), s0, v_ref[pl.ds(k0, hq), cols])
            s1 = jnp.concatenate([s_ref[h, hq:, 0:hq], jnp.where(tri, s_ref[h, hq:, hq:], NEG)], axis=1)
            update(h, slice(hq, tq), s1, v_ref[pl.ds(k0, tk), cols])

    scores(0, sa_ref)

    def body(t, carry):
        j = 2 * t
        scores(j + 1, sb_ref)
        softmax_pv(j, sa_ref, False)
        scores(j + 2, sa_ref)
        softmax_pv(j + 1, sb_ref, False)
        return carry
    lax.fori_loop(0, i // 2, body, 0)

    @pl.when(i % 2 == 0)
    def _():
        softmax_pv(i, sa_ref, True)

    @pl.when(i % 2 == 1)
    def _():
        scores(i, sb_ref)
        softmax_pv(i - 1, sa_ref, False)
        softmax_pv(i, sb_ref, True)

    for h in range(heads):
        acc = acc_ref[h]
        o_ref[:, h * HEAD:(h + 1) * HEAD] = (acc[:, :HEAD] / acc[:, HEAD:]).astype(o_ref.dtype)


def _mla_attn(q, k, v, *, batch, seq, tq=512, tk=512, heads=4):
    nq = seq // tq
    w = 2 * HEAD * heads
    return pl.pallas_call(
        functools.partial(_mla_attn_kernel, tq=tq, tk=tk, heads=heads),
        grid=(batch, N_HEADS // heads, nq),
        in_specs=[
            pl.BlockSpec((tq, w), lambda b, h, i: (b * nq + i, h)),
            pl.BlockSpec((seq, w), lambda b, h, i: (b, h)),
            pl.BlockSpec((seq, w), lambda b, h, i: (b, h)),
        ],
        out_specs=pl.BlockSpec((tq, HEAD * heads), lambda b, h, i: (b * nq + i, h)),
        out_shape=jax.ShapeDtypeStruct((batch * seq, WIDTH), jnp.bfloat16),
        scratch_shapes=[
            pltpu.VMEM((heads, tq, HEAD), jnp.float32),
            pltpu.VMEM((heads, tq, 2 * HEAD), jnp.float32),
            pltpu.VMEM((heads, tq, tk), jnp.float32),
            pltpu.VMEM((heads, tq, tk), jnp.float32),
        ],
        compiler_params=pltpu.CompilerParams(
            dimension_semantics=("parallel", "parallel", "arbitrary"),
            vmem_limit_bytes=VMEM_LIMIT),
        name="mla_attn",
    )(q, k, v)


def _dil_attn_kernel(q_ref, k_ref, v_ref, cos_ref, sin_ref, gq_ref, gk_ref, o_ref,
                     qd_ref, kd_ref, vd_ref, on_ref, ls_ref, bias_ref, st_ref, *, seq, unroll):
    chunk = 1024
    npat = len(DILATIONS)
    assert DILATIONS == (1, 4, 16)
    nat = [on_ref.at[a] for a in range(3)]
    by4 = [ls_ref.at[a] for a in range(3)]
    dst = [qd_ref, kd_ref, vd_ref]
    lead = [0, QBLK, QBLK]

    @pl.when((pl.program_id(0) == 0) & (pl.program_id(1) == 0))
    def _():
        qi = lax.broadcasted_iota(jnp.int32, (QBLK, 2 * QBLK), 0)
        kj = lax.broadcasted_iota(jnp.int32, (QBLK, 2 * QBLK), 1)
        band = (kj >= qi) & (kj <= qi + N_BACK)
        bias_ref[0] = jnp.where(band, 0.0, NEG)
        bias_ref[1] = jnp.where(band & (kj >= QBLK), 0.0, NEG)
        for p in range(npat):
            vd_ref[p, QBLK:, HEAD:] = jnp.ones((seq, HEAD), vd_ref.dtype)
            kd_ref[p, 0:QBLK, :] = jnp.zeros((QBLK, HEAD), kd_ref.dtype)
            vd_ref[p, 0:QBLK, :] = jnp.zeros((QBLK, 2 * HEAD), vd_ref.dtype)

    def prep(c, carry):
        r0 = pl.multiple_of(c * chunk, chunk)
        cos = cos_ref[pl.ds(r0, chunk), :]
        sin = sin_ref[pl.ds(r0, chunk), :]
        q = q_ref[pl.ds(r0, chunk), :].astype(jnp.float32)
        q = q * _rms_scale_mxu(q, HEAD) * gq_ref[...]
        q = q * cos + pltpu.roll(q, 64, 1) * sin
        k = k_ref[pl.ds(r0, chunk), :].astype(jnp.float32)
        k = k * _rms_scale_mxu(k, HEAD) * gk_ref[...]
        k = k * cos + pltpu.roll(k, 64, 1) * sin
        v = v_ref[pl.ds(r0, chunk), :]
        nat[0][pl.ds(r0, chunk), :] = q
        nat[1][pl.ds(r0, chunk), :] = k
        nat[2][pl.ds(r0, chunk), :] = v.astype(jnp.float32)
        qd_ref[0, pl.ds(r0, chunk), :] = q.astype(qd_ref.dtype)
        kd_ref[0, pl.ds(QBLK + r0, chunk), :] = k.astype(kd_ref.dtype)
        vd_ref[0, pl.ds(QBLK + r0, chunk), 0:HEAD] = v
        return carry
    lax.fori_loop(0, seq // chunk, prep, 0)

    sub4, sub16 = seq // 4, seq // 16
    for a in range(3):
        for r4 in range(4):
            x = nat[a][pl.ds(r4, sub4, stride=4), :]
            by4[a][r4 * sub4:(r4 + 1) * sub4, :] = x
            dst[a][1, lead[a] + r4 * sub4:lead[a] + (r4 + 1) * sub4, 0:HEAD] = x.astype(dst[a].dtype)
    for a in range(3):
        for r4 in range(4):
            for j in range(4):
                r16 = r4 + 4 * j
                x = by4[a][pl.ds(r4 * sub4 + j, sub16, stride=4), :]
                dst[a][2, lead[a] + r16 * sub16:lead[a] + (r16 + 1) * sub16, 0:HEAD] = x.astype(dst[a].dtype)

    for p, d in enumerate(DILATIONS):
        nb = seq // d // QBLK
        shift = int(math.log2(nb))

        def block(g, carry, p=p, d=d, nb=nb, shift=shift):
            g0 = pl.multiple_of(g * QBLK, QBLK)
            q = qd_ref[p, pl.ds(g0, QBLK), :]
            k = kd_ref[p, pl.ds(g0, 2 * QBLK), :]
            v = vd_ref[p, pl.ds(g0, 2 * QBLK), :]
            n = g & (nb - 1)
            s = lax.dot_general(q, k, (((1,), (1,)), ((), ())), preferred_element_type=jnp.float32)
            s = s + bias_ref[jnp.where(n == 0, 1, 0)]
            m = jnp.max(s, axis=-1, keepdims=True)
            e = jnp.exp2(s - m)
            pv = jnp.dot(e.astype(v.dtype), v, preferred_element_type=jnp.float32)
            den = pv[:, HEAD:]
            o = pv[:, :HEAD] / den
            lse = m + jnp.log2(den)
            r = g >> shift
            if d == 16:
                rows = pl.ds((r & 3) * sub4 + (r >> 2) + n * (QBLK * 4), QBLK, stride=4)
                st_ref[0, rows, :] = o
                st_ref[1, rows, :] = lse
            else:
                rows = pl.ds(r + n * (QBLK * d), QBLK, stride=d) if d > 1 else pl.ds(g0, QBLK)
                on_ref[p, rows, :] = o
                ls_ref[p, rows, :] = lse
            return carry
        lax.fori_loop(0, seq // QBLK, block, 0, unroll=unroll)

    for a, out in enumerate((on_ref, ls_ref)):
        for r4 in range(4):
            out[2, pl.ds(r4, sub4, stride=4), :] = st_ref[a, r4 * sub4:(r4 + 1) * sub4, :]

    def mix(c, carry):
        r0 = pl.multiple_of(c * chunk, chunk)
        ls = [ls_ref[p, pl.ds(r0, chunk), :] for p in range(npat)]
        mx = functools.reduce(jnp.maximum, ls)
        w = [jnp.exp2(l - mx) for l in ls]
        num = sum(w[p] * on_ref[p, pl.ds(r0, chunk), :] for p in range(npat))
        o_ref[pl.ds(r0, chunk), :] = (num / sum(w)).astype(o_ref.dtype)
        return carry
    lax.fori_loop(0, seq // chunk, mix, 0)


def _dil_attn(proj, cos, sin, gq, gk, *, batch, seq):
    npat = len(DILATIONS)
    col = HEAD

    def qcol(h):
        return jnp.where(h < 4, 1536 // col + h, 4096 // col - 4 + h)

    const = lambda b, h: (0, 0)
    return pl.pallas_call(
        functools.partial(_dil_attn_kernel, seq=seq, unroll=32),
        grid=(batch, N_HEADS),
        in_specs=[
            pl.BlockSpec((seq, HEAD), lambda b, h: (b, qcol(h))),
            pl.BlockSpec((seq, HEAD), lambda b, h: (b, 4608 // col + h)),
            pl.BlockSpec((seq, HEAD), lambda b, h: (b, 5632 // col + h)),
            pl.BlockSpec((seq, HEAD), const),
            pl.BlockSpec((seq, HEAD), const),
            pl.BlockSpec((1, HEAD), const),
            pl.BlockSpec((1, HEAD), const),
        ],
        out_specs=pl.BlockSpec((seq, HEAD), lambda b, h: (b, h)),
        out_shape=jax.ShapeDtypeStruct((batch * seq, WIDTH), jnp.bfloat16),
        scratch_shapes=[
            pltpu.VMEM((npat, seq, HEAD), jnp.bfloat16),
            pltpu.VMEM((npat, seq + QBLK, HEAD), jnp.bfloat16),
            pltpu.VMEM((npat, seq + QBLK, 2 * HEAD), jnp.bfloat16),
            pltpu.VMEM((npat, seq, HEAD), jnp.float32),
            pltpu.VMEM((npat, seq, HEAD), jnp.float32),
            pltpu.VMEM((2, QBLK, 2 * QBLK), jnp.float32),
            pltpu.VMEM((2, seq, HEAD), jnp.float32),
        ],
        compiler_params=pltpu.CompilerParams(
            dimension_semantics=("arbitrary", "arbitrary"), vmem_limit_bytes=VMEM_LIMIT),
        name="dil_attn",
    )(proj, proj, proj, cos, sin, gq, gk)


def _out_proj_kernel(oa_ref, ob_ref, g_ref, x_ref, ga_ref, gb_ref, w_ref, o_ref, *, nsplit):
    def branch(o, gain_ref, gate):
        o = o.astype(jnp.float32)
        gate = gate.astype(jnp.float32)
        y = o * _rms_scale(o, WIDTH) * gain_ref[...]
        return (y * (gate / (1.0 + jnp.exp(-gate)))).astype(jnp.bfloat16)

    step = o_ref.shape[0] // nsplit
    for r in range(0, o_ref.shape[0], step):
        rows = slice(r, r + step)
        ya = branch(oa_ref[rows, :], ga_ref, g_ref[rows, :WIDTH])
        yb = branch(ob_ref[rows, :], gb_ref, g_ref[rows, WIDTH:])
        y = jnp.concatenate([ya, yb], axis=-1)
        o_ref[rows, :] = x_ref[rows, :] + jnp.dot(y, w_ref[...], preferred_element_type=jnp.float32)


def _out_proj(o_a, o_b, proj, x2, ga, gb, w, *, tm=512):
    m = x2.shape[0]
    const = lambda i: (0, 0)
    return pl.pallas_call(
        functools.partial(_out_proj_kernel, nsplit=2),
        grid=(m // tm,),
        in_specs=[
            pl.BlockSpec((tm, WIDTH), lambda i: (i, 0)),
            pl.BlockSpec((tm, WIDTH), lambda i: (i, 0)),
            pl.BlockSpec((tm, 2 * WIDTH), lambda i: (i, 1)),
            pl.BlockSpec((tm, D_MODEL), lambda i: (i, 0)),
            pl.BlockSpec((1, WIDTH), const),
            pl.BlockSpec((1, WIDTH), const),
            pl.BlockSpec((2 * WIDTH, D_MODEL), const),
        ],
        out_specs=pl.BlockSpec((tm, D_MODEL), lambda i: (i, 0)),
        out_shape=jax.ShapeDtypeStruct((m, D_MODEL), jnp.float32),
        compiler_params=pltpu.CompilerParams(
            dimension_semantics=("parallel",), vmem_limit_bytes=VMEM_LIMIT),
        name="out_proj",
    )(o_a, o_b, proj, x2, ga, gb, w)


def _pad_rope_cols(a):
    z = jnp.zeros(a.shape[:-1] + (ROPE // 2,), a.dtype)
    return jnp.concatenate([a[..., :ROPE // 2], z, a[..., ROPE // 2:], z], axis=-1)


def _rope_tables(seq, d, padded):
    inv = ROPE_THETA ** (-np.arange(0, d, 2, dtype=np.float64) / d)
    ang = np.arange(seq, dtype=np.float64)[:, None] * inv[None, :]
    cos, sin = np.cos(ang), np.sin(ang)
    cos2 = np.concatenate([cos, cos], axis=-1)
    sin2 = np.concatenate([-sin, sin], axis=-1)
    if padded:
        z = np.zeros((seq, d // 2))
        cos2 = np.concatenate([cos, z, cos, z], axis=-1)
        sin2 = np.concatenate([-sin, z, sin, z], axis=-1)
    return jnp.asarray(cos2, jnp.float32), jnp.asarray(sin2, jnp.float32)


def kernel(x, norm_gain, w_in, q_a_norm_gain, kv_a_norm_gain, w_uq, w_ukv, mla_q_norm_gain,
           mla_k_norm_gain, dil_q_norm_gain, dil_k_norm_gain, mla_out_norm_gain,
           dil_out_norm_gain, w_out):
    batch, seq, _ = x.shape
    depth = w_in.shape[0]
    bf = jnp.bfloat16
    cos_d, sin_d = _rope_tables(seq, HEAD, padded=False)
    cos_m, sin_m = _rope_tables(seq, ROPE, padded=True)
    h2 = x.reshape(batch * seq, D_MODEL)

    for l in range(depth):
        wq = w_uq[l].reshape(Q_RANK, N_HEADS, MLA_QK)
        wq_r = jnp.concatenate([wq[..., :HEAD], _pad_rope_cols(wq[..., HEAD:])], axis=-1)
        wq_r = wq_r.reshape(Q_RANK, N_HEADS * 2 * HEAD).astype(bf)
        wkv = w_ukv[l].reshape(KV_RANK, N_HEADS, 2 * HEAD)
        wkv_r = jnp.concatenate([wkv[..., :HEAD].reshape(KV_RANK, WIDTH),
                                 wkv[..., HEAD:].reshape(KV_RANK, WIDTH)], axis=-1).astype(bf)

        gq_m = mla_q_norm_gain[l]
        gk_m = mla_k_norm_gain[l]
        gq_full = jnp.concatenate([gq_m[:HEAD], _pad_rope_cols(gq_m[HEAD:])])[None, :]
        gq_full = gq_full * (math.log2(math.e) / math.sqrt(MLA_QK))
        gk_full = jnp.concatenate([gk_m[:HEAD], _pad_rope_cols(gk_m[HEAD:])])[None, :]
        gq_d = dil_q_norm_gain[l][None, :] * (math.log2(math.e) / math.sqrt(HEAD))
        gk_d = dil_k_norm_gain[l][None, :]

        proj = _in_proj(h2, norm_gain[l][None, :], jnp.swapaxes(w_in[l], 0, 1))
        q_f, k_f, v_a = _mla_prep(proj, wq_r, wkv_r, q_a_norm_gain[l][None, :],
                                  kv_a_norm_gain[l][None, :], gq_full, gk_full, cos_m, sin_m, seq=seq)
        o_a = _mla_attn(q_f, k_f, v_a, batch=batch, seq=seq)
        o_b = _dil_attn(proj, cos_d, sin_d, gq_d, gk_d, batch=batch, seq=seq)
        h2 = _out_proj(o_a, o_b, proj, h2, mla_out_norm_gain[l][None, :],
                       dil_out_norm_gain[l][None, :], w_out[l].astype(bf))
    return h2.reshape(batch, seq, D_MODEL)
```

```python
import functools
import math

import jax
import jax.numpy as jnp
import numpy as np
from jax import lax
from jax.experimental import pallas as pl
from jax.experimental.pallas import tpu as pltpu

EPS = 1e-6
ROPE_THETA = 10000.0
NEG = -1e30

D_MODEL = 2048
N_HEADS = 8
HEAD = 128
ROPE = 64
Q_RANK = 768
KV_RANK = 512
WIDTH = N_HEADS * HEAD
MLA_QK = HEAD + ROPE
QBLK = 128
N_BACK = 128
DILATIONS = (1, 4, 16)

PROJ_COLS = 6656
LAT_COLS = 1536
VMEM_LIMIT = 56 * 1024 * 1024


def _rms_scale(x, n):
    return lax.rsqrt(jnp.sum(x * x, axis=-1, keepdims=True) * (1.0 / n) + EPS)


def _rms_scale_mxu(x, n):
    ones = jnp.ones((x.shape[-1], x.shape[-1]), jnp.bfloat16)
    ss = jnp.dot((x * x).astype(jnp.bfloat16), ones, preferred_element_type=jnp.float32)
    return lax.rsqrt(ss * (1.0 / n) + EPS)


_O_CQ, _O_CKV, _O_KR = 0, Q_RANK, Q_RANK + KV_RANK
_O_GA = _O_KR + ROPE
_O_QB = _O_GA + WIDTH
_O_KB, _O_VB, _O_GB = _O_QB + WIDTH, _O_QB + 2 * WIDTH, _O_QB + 3 * WIDTH
_IN_COLS = _O_GB + WIDTH


W_TILE = 512
_W_TILE_SRC = (_O_CKV, _O_CQ, _O_CQ + 256, _O_QB, _O_GA, _O_GA + 512, _O_GB, _O_GB + 512,
               _O_QB + 512, _O_KB, _O_KB + 512, _O_VB, _O_VB + 512)


def _w_prep_kernel(src_ref, w_ref, kr_ref, o_ref):
    del src_ref
    o_ref[...] = w_ref[...].astype(o_ref.dtype)

    @pl.when(pl.program_id(0) == 1)
    def _():
        half = ROPE // 2
        zeros = jnp.zeros((half, D_MODEL), o_ref.dtype)
        o_ref[2 * HEAD:, :] = w_ref[:2 * HEAD, :].astype(o_ref.dtype)
        o_ref[0:half, :] = kr_ref[0:half, :].astype(o_ref.dtype)
        o_ref[half:2 * half, :] = zeros
        o_ref[2 * half:3 * half, :] = kr_ref[half:, :].astype(o_ref.dtype)
        o_ref[3 * half:4 * half, :] = zeros
        o_ref[HEAD:2 * HEAD, :] = jnp.zeros((HEAD, D_MODEL), o_ref.dtype)


def _w_prep(wt):
    assert all(s % ROPE == 0 for s in _W_TILE_SRC)
    src = jnp.asarray([s // ROPE for s in _W_TILE_SRC], jnp.int32)
    return pl.pallas_call(
        _w_prep_kernel,
        grid_spec=pltpu.PrefetchScalarGridSpec(
            num_scalar_prefetch=1,
            grid=(PROJ_COLS // W_TILE,),
            in_specs=[
                pl.BlockSpec((pl.Element(W_TILE), pl.Element(D_MODEL)), lambda t, src: (src[t] * ROPE, 0)),
                pl.BlockSpec((pl.Element(ROPE), pl.Element(D_MODEL)), lambda t, src: (_O_KR, 0)),
            ],
            out_specs=pl.BlockSpec((W_TILE, D_MODEL), lambda t, src: (t, 0)),
        ),
        out_shape=jax.ShapeDtypeStruct((PROJ_COLS, D_MODEL), jnp.bfloat16),
        compiler_params=pltpu.CompilerParams(
            dimension_semantics=("arbitrary",), vmem_limit_bytes=VMEM_LIMIT),
        name="w_prep",
    )(src, wt, wt)


def _in_proj_kernel(src_ref, x_ref, g_ref, w_ref, kr_ref, o_ref, hn_ref, *, row_chunk):
    del src_ref
    j = pl.program_id(1)

    @pl.when(j == 0)
    def _():
        def body(c, carry):
            r0 = pl.multiple_of(c * row_chunk, row_chunk)
            x = x_ref[pl.ds(r0, row_chunk), :]
            hn = x * _rms_scale(x, D_MODEL) * g_ref[...]
            hn_ref[pl.ds(r0, row_chunk), :] = hn.astype(hn_ref.dtype)
            return carry
        lax.fori_loop(0, x_ref.shape[0] // row_chunk, body, 0, unroll=4)

    def project(w):
        half = o_ref.shape[0] // 2
        for r in (0, half):
            o_ref[r:r + half, :] = lax.dot_general(hn_ref[r:r + half, :], w, (((1,), (1,)), ((), ())),
                                                   preferred_element_type=jnp.float32).astype(o_ref.dtype)

    @pl.when(j != 1)
    def _():
        project(w_ref[...].astype(jnp.bfloat16))

    @pl.when(j == 1)
    def _():
        half = ROPE // 2
        kr = kr_ref[...].astype(jnp.bfloat16)
        z = jnp.zeros((half, D_MODEL), jnp.bfloat16)
        project(jnp.concatenate([kr[:half], z, kr[half:], z, jnp.zeros((HEAD, D_MODEL), jnp.bfloat16),
                                 w_ref[:2 * HEAD, :].astype(jnp.bfloat16)], axis=0))


def _in_proj(x2, gain, wt, *, tm=2048):
    m = x2.shape[0]
    assert all(s % ROPE == 0 for s in _W_TILE_SRC)
    src = jnp.asarray([s // ROPE for s in _W_TILE_SRC], jnp.int32)
    return pl.pallas_call(
        functools.partial(_in_proj_kernel, row_chunk=64),
        grid_spec=pltpu.PrefetchScalarGridSpec(
            num_scalar_prefetch=1,
            grid=(m // tm, PROJ_COLS // W_TILE),
            in_specs=[
                pl.BlockSpec((tm, D_MODEL), lambda i, j, src: (i, 0)),
                pl.BlockSpec((1, D_MODEL), lambda i, j, src: (0, 0)),
                pl.BlockSpec((pl.Element(W_TILE), pl.Element(D_MODEL)), lambda i, j, src: (src[j] * ROPE, 0)),
                pl.BlockSpec((pl.Element(ROPE), pl.Element(D_MODEL)), lambda i, j, src: (_O_KR, 0)),
            ],
            out_specs=pl.BlockSpec((tm, W_TILE), lambda i, j, src: (i, j)),
            scratch_shapes=[pltpu.VMEM((tm, D_MODEL), jnp.bfloat16)],
        ),
        out_shape=jax.ShapeDtypeStruct((m, PROJ_COLS), jnp.bfloat16),
        compiler_params=pltpu.CompilerParams(
            dimension_semantics=("arbitrary", "arbitrary"),
            vmem_limit_bytes=60 * 1024 * 1024),
        name="in_proj",
    )(src, x2, gain, wt, wt)


def _rope_pad(x, cos, sin):
    return x * cos + pltpu.roll(x, 64, 1) * sin


def _mla_prep_kernel(lat_ref, wq_ref, wkv_ref, gqa_ref, gkva_ref, gq_ref, gk_ref,
                     cos_ref, sin_ref, q_ref, k_ref, v_ref):
    cos = cos_ref[...]
    sin = sin_ref[...]
    gq = gq_ref[...]
    gk = gk_ref[...]

    c_kv = lat_ref[:, 0:KV_RANK].astype(jnp.float32)
    cn = (c_kv * _rms_scale(c_kv, KV_RANK) * gkva_ref[...]).astype(jnp.bfloat16)
    kv = jnp.dot(cn, wkv_ref[...], preferred_element_type=jnp.float32)
    ones = jnp.ones((kv.shape[0], HEAD), v_ref.dtype)
    for h in range(N_HEADS):
        v_ref[:, 2 * h * HEAD:(2 * h + 1) * HEAD] = kv[:, WIDTH + h * HEAD:WIDTH + (h + 1) * HEAD].astype(v_ref.dtype)
        v_ref[:, (2 * h + 1) * HEAD:(2 * h + 2) * HEAD] = ones

    k_r = lat_ref[:, KV_RANK:KV_RANK + HEAD].astype(jnp.float32)
    k_r = k_r * _rms_scale(k_r, ROPE) * gk[:, HEAD:]
    k_rope = _rope_pad(k_r, cos, sin).astype(k_ref.dtype)
    for h in range(N_HEADS):
        kn = kv[:, h * HEAD:(h + 1) * HEAD]
        kn = kn * _rms_scale(kn, HEAD) * gk[:, :HEAD]
        k_ref[:, 2 * h * HEAD:(2 * h + 1) * HEAD] = kn.astype(k_ref.dtype)
        k_ref[:, (2 * h + 1) * HEAD:(2 * h + 2) * HEAD] = k_rope

    c_q = lat_ref[:, 2 * HEAD + KV_RANK:].astype(jnp.float32)
    cqn = (c_q * _rms_scale(c_q, Q_RANK) * gqa_ref[...]).astype(jnp.bfloat16)
    q = jnp.dot(cqn, wq_ref[...], preferred_element_type=jnp.float32)
    for h in range(N_HEADS):
        qn = q[:, 2 * h * HEAD:(2 * h + 1) * HEAD]
        qn = qn * _rms_scale(qn, HEAD) * gq[:, :HEAD]
        q_ref[:, 2 * h * HEAD:(2 * h + 1) * HEAD] = qn.astype(q_ref.dtype)
        qr = q[:, (2 * h + 1) * HEAD:(2 * h + 2) * HEAD]
        qr = qr * _rms_scale(qr, ROPE) * gq[:, HEAD:]
        q_ref[:, (2 * h + 1) * HEAD:(2 * h + 2) * HEAD] = _rope_pad(qr, cos, sin).astype(q_ref.dtype)


def _mla_prep(proj, wq, wkv, gqa, gkva, gq, gk, cos, sin, *, seq, tm=256):
    m = proj.shape[0]
    nseq = seq // tm
    const = lambda i: (0, 0)
    return pl.pallas_call(
        _mla_prep_kernel,
        grid=(m // tm,),
        in_specs=[
            pl.BlockSpec((tm, LAT_COLS), lambda i: (i, 0)),
            pl.BlockSpec(wq.shape, const),
            pl.BlockSpec(wkv.shape, const),
            pl.BlockSpec(gqa.shape, const),
            pl.BlockSpec(gkva.shape, const),
            pl.BlockSpec(gq.shape, const),
            pl.BlockSpec(gk.shape, const),
            pl.BlockSpec((tm, HEAD), lambda i: (i % nseq, 0)),
            pl.BlockSpec((tm, HEAD), lambda i: (i % nseq, 0)),
        ],
        out_specs=[
            pl.BlockSpec((tm, 2 * WIDTH), lambda i: (i, 0)),
            pl.BlockSpec((tm, 2 * WIDTH), lambda i: (i, 0)),
            pl.BlockSpec((tm, 2 * WIDTH), lambda i: (i, 0)),
        ],
        out_shape=[
            jax.ShapeDtypeStruct((m, 2 * WIDTH), jnp.bfloat16),
            jax.ShapeDtypeStruct((m, 2 * WIDTH), jnp.bfloat16),
            jax.ShapeDtypeStruct((m, 2 * WIDTH), jnp.bfloat16),
        ],
        compiler_params=pltpu.CompilerParams(
            dimension_semantics=("parallel",), vmem_limit_bytes=VMEM_LIMIT),
        name="mla_prep",
    )(proj, wq, wkv, gqa, gkva, gq, gk, cos, sin)


def _mla_attn_kernel(q_ref, k_ref, v_ref, o_ref, m_ref, acc_ref, sa_ref, sb_ref, *, tq, tk, heads):
    assert tq == tk
    i = pl.program_id(2)
    m_ref[...] = jnp.full(m_ref.shape, NEG, jnp.float32)
    acc_ref[...] = jnp.zeros(acc_ref.shape, jnp.float32)

    def scores(j, s_ref):
        k0 = pl.multiple_of(j * tk, tk)
        for h in range(heads):
            cols = slice(2 * h * HEAD, 2 * (h + 1) * HEAD)
            s_ref[h] = lax.dot_general(q_ref[:, cols], k_ref[pl.ds(k0, tk), cols],
                                       (((1,), (1,)), ((), ())),
                                       preferred_element_type=jnp.float32)

    def update(h, rows, s, v):
        m_prev = m_ref[h, rows, :]
        m_new = jnp.maximum(m_prev, jnp.max(s, axis=-1, keepdims=True))
        alpha = jnp.exp2(m_prev - m_new)
        p = jnp.exp2(s - jnp.tile(m_new, (1, s.shape[1] // HEAD)))
        pv = jnp.dot(p.astype(v.dtype), v, preferred_element_type=jnp.float32)
        acc_ref[h, rows, :] = jnp.tile(alpha, (1, 2)) * acc_ref[h, rows, :] + pv
        m_ref[h, rows, :] = m_new

    def softmax_pv(j, s_ref, masked):
        k0 = pl.multiple_of(j * tk, tk)
        hq = tq // 2
        for h in range(heads):
            cols = slice(2 * h * HEAD, 2 * (h + 1) * HEAD)
            if not masked:
                update(h, slice(None), s_ref[h], v_ref[pl.ds(k0, tk), cols])
                continue
            row = lax.broadcasted_iota(jnp.int32, (hq, hq), 0)
            col = lax.broadcasted_iota(jnp.int32, (hq, hq), 1)
            tri = col <= row
            s0 = jnp.where(tri, s_ref[h, 0:hq, 0:hq], NEG)
            update(h, slice(0, hq), s0, v_ref[pl.ds(k0, hq), cols])
            s1 = jnp.concatenate([s_ref[h, hq:, 0:hq], jnp.where(tri, s_ref[h, hq:, hq:], NEG)], axis=1)
            update(h, slice(hq, tq), s1, v_ref[pl.ds(k0, tk), cols])

    scores(0, sa_ref)

    def body(t, carry):
        j = 2 * t
        scores(j + 1, sb_ref)
        softmax_pv(j, sa_ref, False)
        scores(j + 2, sa_ref)
        softmax_pv(j + 1, sb_ref, False)
        return carry
    lax.fori_loop(0, i // 2, body, 0)

    @pl.when(i % 2 == 0)
    def _():
        softmax_pv(i, sa_ref, True)

    @pl.when(i % 2 == 1)
    def _():
        scores(i, sb_ref)
        softmax_pv(i - 1, sa_ref, False)
        softmax_pv(i, sb_ref, True)

    for h in range(heads):
        acc = acc_ref[h]
        o_ref[:, h * HEAD:(h + 1) * HEAD] = (acc[:, :HEAD] / acc[:, HEAD:]).astype(o_ref.dtype)


def _mla_attn(q, k, v, *, batch, seq, tq=512, tk=512, heads=4):
    nq = seq // tq
    w = 2 * HEAD * heads
    return pl.pallas_call(
        functools.partial(_mla_attn_kernel, tq=tq, tk=tk, heads=heads),
        grid=(batch, N_HEADS // heads, nq),
        in_specs=[
            pl.BlockSpec((tq, w), lambda b, h, i: (b * nq + i, h)),
            pl.BlockSpec((seq, w), lambda b, h, i: (b, h)),
            pl.BlockSpec((seq, w), lambda b, h, i: (b, h)),
        ],
        out_specs=pl.BlockSpec((tq, HEAD * heads), lambda b, h, i: (b * nq + i, h)),
        out_shape=jax.ShapeDtypeStruct((batch * seq, WIDTH), jnp.bfloat16),
        scratch_shapes=[
            pltpu.VMEM((heads, tq, HEAD), jnp.float32),
            pltpu.VMEM((heads, tq, 2 * HEAD), jnp.float32),
            pltpu.VMEM((heads, tq, tk), jnp.float32),
            pltpu.VMEM((heads, tq, tk), jnp.float32),
        ],
        compiler_params=pltpu.CompilerParams(
            dimension_semantics=("parallel", "parallel", "arbitrary"),
            vmem_limit_bytes=VMEM_LIMIT),
        name="mla_attn",
    )(q, k, v)


def _dil_attn_kernel(q_ref, k_ref, v_ref, cos_ref, sin_ref, gq_ref, gk_ref, o_ref,
                     qd_ref, kd_ref, vd_ref, on_ref, ls_ref, bias_ref, st_ref, *, seq, unroll):
    chunk = 512
    npat = len(DILATIONS)
    assert DILATIONS == (1, 4, 16)
    nat = [on_ref.at[a] for a in range(3)]
    by4 = [ls_ref.at[a] for a in range(3)]
    dst = [qd_ref, kd_ref, vd_ref]
    lead = [0, QBLK, QBLK]

    @pl.when((pl.program_id(0) == 0) & (pl.program_id(1) == 0))
    def _():
        qi = lax.broadcasted_iota(jnp.int32, (QBLK, 2 * QBLK), 0)
        kj = lax.broadcasted_iota(jnp.int32, (QBLK, 2 * QBLK), 1)
        band = (kj >= qi) & (kj <= qi + N_BACK)
        bias_ref[0] = jnp.where(band, 0.0, NEG)
        bias_ref[1] = jnp.where(band & (kj >= QBLK), 0.0, NEG)
        for p in range(npat):
            vd_ref[p, QBLK:, HEAD:] = jnp.ones((seq, HEAD), vd_ref.dtype)
            kd_ref[p, 0:QBLK, :] = jnp.zeros((QBLK, HEAD), kd_ref.dtype)
            vd_ref[p, 0:QBLK, :] = jnp.zeros((QBLK, 2 * HEAD), vd_ref.dtype)

    def prep(c, carry):
        r0 = pl.multiple_of(c * chunk, chunk)
        cos = cos_ref[pl.ds(r0, chunk), :]
        sin = sin_ref[pl.ds(r0, chunk), :]
        q = q_ref[pl.ds(r0, chunk), :].astype(jnp.float32)
        q = q * _rms_scale_mxu(q, HEAD) * gq_ref[...]
        q = q * cos + pltpu.roll(q, 64, 1) * sin
        k = k_ref[pl.ds(r0, chunk), :].astype(jnp.float32)
        k = k * _rms_scale_mxu(k, HEAD) * gk_ref[...]
        k = k * cos + pltpu.roll(k, 64, 1) * sin
        v = v_ref[pl.ds(r0, chunk), :]
        nat[0][pl.ds(r0, chunk), :] = q
        nat[1][pl.ds(r0, chunk), :] = k
        nat[2][pl.ds(r0, chunk), :] = v.astype(jnp.float32)
        qd_ref[0, pl.ds(r0, chunk), :] = q.astype(qd_ref.dtype)
        kd_ref[0, pl.ds(QBLK + r0, chunk), :] = k.astype(kd_ref.dtype)
        vd_ref[0, pl.ds(QBLK + r0, chunk), 0:HEAD] = v
        return carry
    lax.fori_loop(0, seq // chunk, prep, 0, unroll=2)

    sub4, sub16 = seq // 4, seq // 16
    for a in range(3):
        for r4 in range(4):
            x = nat[a][pl.ds(r4, sub4, stride=4), :]
            by4[a][r4 * sub4:(r4 + 1) * sub4, :] = x
            dst[a][1, lead[a] + r4 * sub4:lead[a] + (r4 + 1) * sub4, 0:HEAD] = x.astype(dst[a].dtype)
    for a in range(3):
        for r4 in range(4):
            for j in range(4):
                r16 = r4 + 4 * j
                x = by4[a][pl.ds(r4 * sub4 + j, sub16, stride=4), :]
                dst[a][2, lead[a] + r16 * sub16:lead[a] + (r16 + 1) * sub16, 0:HEAD] = x.astype(dst[a].dtype)

    for p, d in enumerate(DILATIONS):
        nb = seq // d // QBLK
        shift = int(math.log2(nb))

        def block(g, carry, p=p, d=d, nb=nb, shift=shift):
            g0 = pl.multiple_of(g * QBLK, QBLK)
            q = qd_ref[p, pl.ds(g0, QBLK), :]
            k = kd_ref[p, pl.ds(g0, 2 * QBLK), :]
            v = vd_ref[p, pl.ds(g0, 2 * QBLK), :]
            n = g & (nb - 1)
            s = lax.dot_general(q, k, (((1,), (1,)), ((), ())), preferred_element_type=jnp.float32)
            s = s + bias_ref[jnp.where(n == 0, 1, 0)]
            m = jnp.max(s, axis=-1, keepdims=True)
            e = jnp.exp2(s - m)
            pv = jnp.dot(e.astype(v.dtype), v, preferred_element_type=jnp.float32)
            den = pv[:, HEAD:]
            o = pv[:, :HEAD] / den
            lse = m + jnp.log2(den)
            r = g >> shift
            if d == 16:
                rows = pl.ds((r & 3) * sub4 + (r >> 2) + n * (QBLK * 4), QBLK, stride=4)
                st_ref[0, rows, :] = o
                st_ref[1, rows, :] = lse
            else:
                rows = pl.ds(r + n * (QBLK * d), QBLK, stride=d) if d > 1 else pl.ds(g0, QBLK)
                on_ref[p, rows, :] = o
                ls_ref[p, rows, :] = lse
            return carry
        lax.fori_loop(0, seq // QBLK, block, 0, unroll=unroll)

    for a, out in enumerate((on_ref, ls_ref)):
        for r4 in range(4):
            out[2, pl.ds(r4, sub4, stride=4), :] = st_ref[a, r4 * sub4:(r4 + 1) * sub4, :]

    def mix(c, carry):
        r0 = pl.multiple_of(c * chunk, chunk)
        ls = [ls_ref[p, pl.ds(r0, chunk), :] for p in range(npat)]
        mx = functools.reduce(jnp.maximum, ls)
        w = [jnp.exp2(l - mx) for l in ls]
        num = sum(w[p] * on_ref[p, pl.ds(r0, chunk), :] for p in range(npat))
        o_ref[pl.ds(r0, chunk), :] = (num / sum(w)).astype(o_ref.dtype)
        return carry
    lax.fori_loop(0, seq // chunk, mix, 0, unroll=2)


def _dil_attn(proj, cos, sin, gq, gk, *, batch, seq):
    npat = len(DILATIONS)
    col = HEAD

    def qcol(h):
        return jnp.where(h < 4, 1536 // col + h, 4096 // col - 4 + h)

    const = lambda b, h: (0, 0)
    return pl.pallas_call(
        functools.partial(_dil_attn_kernel, seq=seq, unroll=32),
        grid=(batch, N_HEADS),
        in_specs=[
            pl.BlockSpec((seq, HEAD), lambda b, h: (b, qcol(h))),
            pl.BlockSpec((seq, HEAD), lambda b, h: (b, 4608 // col + h)),
            pl.BlockSpec((seq, HEAD), lambda b, h: (b, 5632 // col + h)),
            pl.BlockSpec((seq, HEAD), const),
            pl.BlockSpec((seq, HEAD), const),
            pl.BlockSpec((1, HEAD), const),
            pl.BlockSpec((1, HEAD), const),
        ],
        out_specs=pl.BlockSpec((seq, HEAD), lambda b, h: (b, h)),
        out_shape=jax.ShapeDtypeStruct((batch * seq, WIDTH), jnp.bfloat16),
        scratch_shapes=[
            pltpu.VMEM((npat, seq, HEAD), jnp.bfloat16),
            pltpu.VMEM((npat, seq + QBLK, HEAD), jnp.bfloat16),
            pltpu.VMEM((npat, seq + QBLK, 2 * HEAD), jnp.bfloat16),
            pltpu.VMEM((npat, seq, HEAD), jnp.float32),
            pltpu.VMEM((npat, seq, HEAD), jnp.float32),
            pltpu.VMEM((2, QBLK, 2 * QBLK), jnp.float32),
            pltpu.VMEM((2, seq, HEAD), jnp.float32),
        ],
        compiler_params=pltpu.CompilerParams(
            dimension_semantics=("arbitrary", "arbitrary"), vmem_limit_bytes=VMEM_LIMIT),
        name="dil_attn",
    )(proj, proj, proj, cos, sin, gq, gk)


def _out_proj_kernel(oa_ref, ob_ref, g_ref, x_ref, ga_ref, gb_ref, w32_ref, o_ref, w_ref, *, nsplit):
    @pl.when(pl.program_id(0) == 0)
    def _():
        rows = 256

        def cast(c, carry):
            r0 = pl.multiple_of(c * rows, rows)
            w_ref[pl.ds(r0, rows), :] = w32_ref[pl.ds(r0, rows), :].astype(w_ref.dtype)
            return carry
        lax.fori_loop(0, w_ref.shape[0] // rows, cast, 0)

    def branch(o, gain_ref, gate):
        o = o.astype(jnp.float32)
        gate = gate.astype(jnp.float32)
        y = o * _rms_scale(o, WIDTH) * gain_ref[...]
        return (y * (gate / (1.0 + jnp.exp(-gate)))).astype(jnp.bfloat16)

    step = o_ref.shape[0] // nsplit
    for r in range(0, o_ref.shape[0], step):
        rows = slice(r, r + step)
        ya = branch(oa_ref[rows, :], ga_ref, g_ref[rows, :WIDTH])
        yb = branch(ob_ref[rows, :], gb_ref, g_ref[rows, WIDTH:])
        y = jnp.concatenate([ya, yb], axis=-1)
        o_ref[rows, :] = x_ref[rows, :] + jnp.dot(y, w_ref[...], preferred_element_type=jnp.float32)


def _out_proj(o_a, o_b, proj, x2, ga, gb, w, *, tm=512):
    m = x2.shape[0]
    const = lambda i: (0, 0)
    return pl.pallas_call(
        functools.partial(_out_proj_kernel, nsplit=2),
        grid=(m // tm,),
        in_specs=[
            pl.BlockSpec((tm, WIDTH), lambda i: (i, 0)),
            pl.BlockSpec((tm, WIDTH), lambda i: (i, 0)),
            pl.BlockSpec((tm, 2 * WIDTH), lambda i: (i, 1)),
            pl.BlockSpec((tm, D_MODEL), lambda i: (i, 0)),
            pl.BlockSpec((1, WIDTH), const),
            pl.BlockSpec((1, WIDTH), const),
            pl.BlockSpec((2 * WIDTH, D_MODEL), const, pipeline_mode=pl.Buffered(1)),
        ],
        out_specs=pl.BlockSpec((tm, D_MODEL), lambda i: (i, 0)),
        out_shape=jax.ShapeDtypeStruct((m, D_MODEL), jnp.float32),
        scratch_shapes=[pltpu.VMEM((2 * WIDTH, D_MODEL), jnp.bfloat16)],
        compiler_params=pltpu.CompilerParams(
            dimension_semantics=("arbitrary",), vmem_limit_bytes=VMEM_LIMIT),
        name="out_proj",
    )(o_a, o_b, proj, x2, ga, gb, w)


def _pad_rope_cols(a):
    z = jnp.zeros(a.shape[:-1] + (ROPE // 2,), a.dtype)
    return jnp.concatenate([a[..., :ROPE // 2], z, a[..., ROPE // 2:], z], axis=-1)


def _rope_tables(seq, d, padded):
    inv = ROPE_THETA ** (-np.arange(0, d, 2, dtype=np.float64) / d)
    ang = np.arange(seq, dtype=np.float64)[:, None] * inv[None, :]
    cos, sin = np.cos(ang), np.sin(ang)
    cos2 = np.concatenate([cos, cos], axis=-1)
    sin2 = np.concatenate([-sin, sin], axis=-1)
    if padded:
        z = np.zeros((seq, d // 2))
        cos2 = np.concatenate([cos, z, cos, z], axis=-1)
        sin2 = np.concatenate([-sin, z, sin, z], axis=-1)
    return jnp.asarray(cos2, jnp.float32), jnp.asarray(sin2, jnp.float32)


def kernel(x, norm_gain, w_in, q_a_norm_gain, kv_a_norm_gain, w_uq, w_ukv, mla_q_norm_gain,
           mla_k_norm_gain, dil_q_norm_gain, dil_k_norm_gain, mla_out_norm_gain,
           dil_out_norm_gain, w_out):
    batch, seq, _ = x.shape
    depth = w_in.shape[0]
    bf = jnp.bfloat16
    cos_d, sin_d = _rope_tables(seq, HEAD, padded=False)
    cos_m, sin_m = _rope_tables(seq, ROPE, padded=True)
    h2 = x.reshape(batch * seq, D_MODEL)

    for l in range(depth):
        wq = w_uq[l].reshape(Q_RANK, N_HEADS, MLA_QK)
        wq_r = jnp.concatenate([wq[..., :HEAD], _pad_rope_cols(wq[..., HEAD:])], axis=-1)
        wq_r = wq_r.reshape(Q_RANK, N_HEADS * 2 * HEAD).astype(bf)
        wkv = w_ukv[l].reshape(KV_RANK, N_HEADS, 2 * HEAD)
        wkv_r = jnp.concatenate([wkv[..., :HEAD].reshape(KV_RANK, WIDTH),
                                 wkv[..., HEAD:].reshape(KV_RANK, WIDTH)], axis=-1).astype(bf)

        gq_m = mla_q_norm_gain[l]
        gk_m = mla_k_norm_gain[l]
        gq_full = jnp.concatenate([gq_m[:HEAD], _pad_rope_cols(gq_m[HEAD:])])[None, :]
        gq_full = gq_full * (math.log2(math.e) / math.sqrt(MLA_QK))
        gk_full = jnp.concatenate([gk_m[:HEAD], _pad_rope_cols(gk_m[HEAD:])])[None, :]
        gq_d = dil_q_norm_gain[l][None, :] * (math.log2(math.e) / math.sqrt(HEAD))
        gk_d = dil_k_norm_gain[l][None, :]

        proj = _in_proj(h2, norm_gain[l][None, :], jnp.swapaxes(w_in[l], 0, 1))
        q_f, k_f, v_a = _mla_prep(proj, wq_r, wkv_r, q_a_norm_gain[l][None, :],
                                  kv_a_norm_gain[l][None, :], gq_full, gk_full, cos_m, sin_m, seq=seq)
        o_a = _mla_attn(q_f, k_f, v_a, batch=batch, seq=seq)
        o_b = _dil_attn(proj, cos_d, sin_d, gq_d, gk_d, batch=batch, seq=seq)
        h2 = _out_proj(o_a, o_b, proj, h2, mla_out_norm_gain[l][None, :],
                       dil_out_norm_gain[l][None, :], w_out[l])
    return h2.reshape(batch, seq, D_MODEL)
```

```python
import functools
import math

import jax
import jax.numpy as jnp
import numpy as np
from jax import lax
from jax.experimental import pallas as pl
from jax.experimental.pallas import tpu as pltpu

EPS = 1e-6
ROPE_THETA = 10000.0
NEG = -1e30

D_MODEL = 2048
N_HEADS = 8
HEAD = 128
ROPE = 64
Q_RANK = 768
KV_RANK = 512
WIDTH = N_HEADS * HEAD
MLA_QK = HEAD + ROPE
QBLK = 128
N_BACK = 128
DILATIONS = (1, 4, 16)

PROJ_COLS = 6656
LAT_COLS = 1536
VMEM_LIMIT = 56 * 1024 * 1024


def _rms_scale(x, n):
    return lax.rsqrt(jnp.sum(x * x, axis=-1, keepdims=True) * (1.0 / n) + EPS)


def _rms_scale_mxu(x, n):
    ones = jnp.ones((x.shape[-1], x.shape[-1]), jnp.bfloat16)
    ss = jnp.dot((x * x).astype(jnp.bfloat16), ones, preferred_element_type=jnp.float32)
    return lax.rsqrt(ss * (1.0 / n) + EPS)


_O_CQ, _O_CKV, _O_KR = 0, Q_RANK, Q_RANK + KV_RANK
_O_GA = _O_KR + ROPE
_O_QB = _O_GA + WIDTH
_O_KB, _O_VB, _O_GB = _O_QB + WIDTH, _O_QB + 2 * WIDTH, _O_QB + 3 * WIDTH
_IN_COLS = _O_GB + WIDTH


W_TILE = 512
_W_TILE_SRC = (_O_CKV, _O_CQ, _O_CQ + 256, _O_QB, _O_GA, _O_GA + 512, _O_GB, _O_GB + 512,
               _O_QB + 512, _O_KB, _O_KB + 512, _O_VB, _O_VB + 512)


def _w_prep_kernel(src_ref, w_ref, kr_ref, o_ref):
    del src_ref
    o_ref[...] = w_ref[...].astype(o_ref.dtype)

    @pl.when(pl.program_id(0) == 1)
    def _():
        half = ROPE // 2
        zeros = jnp.zeros((half, D_MODEL), o_ref.dtype)
        o_ref[2 * HEAD:, :] = w_ref[:2 * HEAD, :].astype(o_ref.dtype)
        o_ref[0:half, :] = kr_ref[0:half, :].astype(o_ref.dtype)
        o_ref[half:2 * half, :] = zeros
        o_ref[2 * half:3 * half, :] = kr_ref[half:, :].astype(o_ref.dtype)
        o_ref[3 * half:4 * half, :] = zeros
        o_ref[HEAD:2 * HEAD, :] = jnp.zeros((HEAD, D_MODEL), o_ref.dtype)


def _w_prep(wt):
    assert all(s % ROPE == 0 for s in _W_TILE_SRC)
    src = jnp.asarray([s // ROPE for s in _W_TILE_SRC], jnp.int32)
    return pl.pallas_call(
        _w_prep_kernel,
        grid_spec=pltpu.PrefetchScalarGridSpec(
            num_scalar_prefetch=1,
            grid=(PROJ_COLS // W_TILE,),
            in_specs=[
                pl.BlockSpec((pl.Element(W_TILE), pl.Element(D_MODEL)), lambda t, src: (src[t] * ROPE, 0)),
                pl.BlockSpec((pl.Element(ROPE), pl.Element(D_MODEL)), lambda t, src: (_O_KR, 0)),
            ],
            out_specs=pl.BlockSpec((W_TILE, D_MODEL), lambda t, src: (t, 0)),
        ),
        out_shape=jax.ShapeDtypeStruct((PROJ_COLS, D_MODEL), jnp.bfloat16),
        compiler_params=pltpu.CompilerParams(
            dimension_semantics=("arbitrary",), vmem_limit_bytes=VMEM_LIMIT),
        name="w_prep",
    )(src, wt, wt)


X_CHUNK = 512
N_CHUNKS = 8


def _in_proj_kernel(src_ref, x_ref, g_ref, w_ref, kr_ref, o_ref, hn0_ref, hn1_ref, *, row_chunk, n_tiles):
    del src_ref
    i = pl.program_id(0)
    j = pl.program_id(1)

    def norm(hn_ref):
        base = pl.multiple_of(j * X_CHUNK, X_CHUNK)

        def body(c, carry):
            r0 = pl.multiple_of(c * row_chunk, row_chunk)
            x = x_ref[pl.ds(r0, row_chunk), :]
            hn = x * _rms_scale(x, D_MODEL) * g_ref[...]
            hn_ref[pl.ds(base + r0, row_chunk), :] = hn.astype(hn_ref.dtype)
            return carry
        lax.fori_loop(0, X_CHUNK // row_chunk, body, 0, unroll=4)

    def weight_tile():
        generic = w_ref[...].astype(jnp.bfloat16)
        half = ROPE // 2
        kr = kr_ref[...].astype(jnp.bfloat16)
        z = jnp.zeros((half, D_MODEL), jnp.bfloat16)
        special = jnp.concatenate([kr[:half], z, kr[half:], z, jnp.zeros((HEAD, D_MODEL), jnp.bfloat16),
                                   generic[:2 * HEAD, :]], axis=0)
        return jnp.where(j == 1, special, generic)

    def project(hn_ref):
        w = weight_tile()
        quarter = o_ref.shape[0] // 4
        for r in range(0, o_ref.shape[0], quarter):
            o_ref[r:r + quarter, :] = lax.dot_general(
                hn_ref[r:r + quarter, :], w, (((1,), (1,)), ((), ())),
                preferred_element_type=jnp.float32).astype(o_ref.dtype)

    do_norm = (i < n_tiles) & (j < N_CHUNKS)
    even = (i % 2) == 0

    @pl.when((i == 0) & do_norm)
    def _():
        norm(hn0_ref)

    for is_even, hn_cur, hn_prev in ((True, hn0_ref, hn1_ref), (False, hn1_ref, hn0_ref)):
        parity = even if is_even else jnp.logical_not(even)

        @pl.when((i > 0) & parity & do_norm)
        def _(hn_cur=hn_cur, hn_prev=hn_prev):
            project(hn_prev)
            norm(hn_cur)

        @pl.when((i > 0) & parity & jnp.logical_not(do_norm))
        def _(hn_prev=hn_prev):
            project(hn_prev)


def _in_proj(x2, gain, wt):
    m = x2.shape[0]
    tm = X_CHUNK * N_CHUNKS
    n_tiles = m // tm
    n_col = PROJ_COLS // W_TILE
    assert N_CHUNKS <= n_col
    assert all(s % ROPE == 0 for s in _W_TILE_SRC)
    src = jnp.asarray([s // ROPE for s in _W_TILE_SRC], jnp.int32)

    def x_index(i, j, src):
        return (jnp.where(i < n_tiles, i * N_CHUNKS + jnp.minimum(j, N_CHUNKS - 1), n_tiles * N_CHUNKS - 1), 0)

    def out_index(i, j, src):
        return (jnp.maximum(i - 1, 0), jnp.where(i == 0, 0, j))

    return pl.pallas_call(
        functools.partial(_in_proj_kernel, row_chunk=64, n_tiles=n_tiles),
        grid_spec=pltpu.PrefetchScalarGridSpec(
            num_scalar_prefetch=1,
            grid=(n_tiles + 1, n_col),
            in_specs=[
                pl.BlockSpec((X_CHUNK, D_MODEL), x_index),
                pl.BlockSpec((1, D_MODEL), lambda i, j, src: (0, 0)),
                pl.BlockSpec((pl.Element(W_TILE), pl.Element(D_MODEL)), lambda i, j, src: (src[j] * ROPE, 0)),
                pl.BlockSpec((pl.Element(ROPE), pl.Element(D_MODEL)), lambda i, j, src: (_O_KR, 0)),
            ],
            out_specs=pl.BlockSpec((tm, W_TILE), out_index),
            scratch_shapes=[pltpu.VMEM((tm, D_MODEL), jnp.bfloat16),
                            pltpu.VMEM((tm, D_MODEL), jnp.bfloat16)],
        ),
        out_shape=jax.ShapeDtypeStruct((m, PROJ_COLS), jnp.bfloat16),
        compiler_params=pltpu.CompilerParams(
            dimension_semantics=("arbitrary", "arbitrary"),
            vmem_limit_bytes=60 * 1024 * 1024),
        name="in_proj",
    )(src, x2, gain, wt, wt)


def _rope_pad(x, cos, sin):
    return x * cos + pltpu.roll(x, 64, 1) * sin


def _mla_prep_kernel(lat_ref, wq_ref, wkv_ref, gqa_ref, gkva_ref, gq_ref, gk_ref,
                     cos_ref, sin_ref, q_ref, k_ref, v_ref):
    cos = cos_ref[...]
    sin = sin_ref[...]
    gq = gq_ref[...]
    gk = gk_ref[...]

    c_kv = lat_ref[:, 0:KV_RANK].astype(jnp.float32)
    cn = (c_kv * _rms_scale(c_kv, KV_RANK) * gkva_ref[...]).astype(jnp.bfloat16)
    kv = jnp.dot(cn, wkv_ref[...], preferred_element_type=jnp.float32)
    ones = jnp.ones((kv.shape[0], HEAD), v_ref.dtype)
    for h in range(N_HEADS):
        v_ref[:, 2 * h * HEAD:(2 * h + 1) * HEAD] = kv[:, WIDTH + h * HEAD:WIDTH + (h + 1) * HEAD].astype(v_ref.dtype)
        v_ref[:, (2 * h + 1) * HEAD:(2 * h + 2) * HEAD] = ones

    k_r = lat_ref[:, KV_RANK:KV_RANK + HEAD].astype(jnp.float32)
    k_r = k_r * _rms_scale(k_r, ROPE) * gk[:, HEAD:]
    k_rope = _rope_pad(k_r, cos, sin).astype(k_ref.dtype)
    for h in range(N_HEADS):
        kn = kv[:, h * HEAD:(h + 1) * HEAD]
        kn = kn * _rms_scale(kn, HEAD) * gk[:, :HEAD]
        k_ref[:, 2 * h * HEAD:(2 * h + 1) * HEAD] = kn.astype(k_ref.dtype)
        k_ref[:, (2 * h + 1) * HEAD:(2 * h + 2) * HEAD] = k_rope

    c_q = lat_ref[:, 2 * HEAD + KV_RANK:].astype(jnp.float32)
    cqn = (c_q * _rms_scale(c_q, Q_RANK) * gqa_ref[...]).astype(jnp.bfloat16)
    q = jnp.dot(cqn, wq_ref[...], preferred_element_type=jnp.float32)
    for h in range(N_HEADS):
        qn = q[:, 2 * h * HEAD:(2 * h + 1) * HEAD]
        qn = qn * _rms_scale(qn, HEAD) * gq[:, :HEAD]
        q_ref[:, 2 * h * HEAD:(2 * h + 1) * HEAD] = qn.astype(q_ref.dtype)
        qr = q[:, (2 * h + 1) * HEAD:(2 * h + 2) * HEAD]
        qr = qr * _rms_scale(qr, ROPE) * gq[:, HEAD:]
        q_ref[:, (2 * h + 1) * HEAD:(2 * h + 2) * HEAD] = _rope_pad(qr, cos, sin).astype(q_ref.dtype)


def _mla_prep(proj, wq, wkv, gqa, gkva, gq, gk, cos, sin, *, seq, tm=256):
    m = proj.shape[0]
    nseq = seq // tm
    const = lambda i: (0, 0)
    return pl.pallas_call(
        _mla_prep_kernel,
        grid=(m // tm,),
        in_specs=[
            pl.BlockSpec((tm, LAT_COLS), lambda i: (i, 0)),
            pl.BlockSpec(wq.shape, const),
            pl.BlockSpec(wkv.shape, const),
            pl.BlockSpec(gqa.shape, const),
            pl.BlockSpec(gkva.shape, const),
            pl.BlockSpec(gq.shape, const),
            pl.BlockSpec(gk.shape, const),
            pl.BlockSpec((tm, HEAD), lambda i: (i % nseq, 0)),
            pl.BlockSpec((tm, HEAD), lambda i: (i % nseq, 0)),
        ],
        out_specs=[
            pl.BlockSpec((tm, 2 * WIDTH), lambda i: (i, 0)),
            pl.BlockSpec((tm, 2 * WIDTH), lambda i: (i, 0)),
            pl.BlockSpec((tm, 2 * WIDTH), lambda i: (i, 0)),
        ],
        out_shape=[
            jax.ShapeDtypeStruct((m, 2 * WIDTH), jnp.bfloat16),
            jax.ShapeDtypeStruct((m, 2 * WIDTH), jnp.bfloat16),
            jax.ShapeDtypeStruct((m, 2 * WIDTH), jnp.bfloat16),
        ],
        compiler_params=pltpu.CompilerParams(
            dimension_semantics=("parallel",), vmem_limit_bytes=VMEM_LIMIT),
        name="mla_prep",
    )(proj, wq, wkv, gqa, gkva, gq, gk, cos, sin)


def _mla_attn_kernel(q_ref, k_ref, v_ref, o_ref, m_ref, acc_ref, sa_ref, sb_ref, *, tq, tk, heads):
    assert tq == tk
    i = pl.program_id(2)
    m_ref[...] = jnp.full(m_ref.shape, NEG, jnp.float32)
    acc_ref[...] = jnp.zeros(acc_ref.shape, jnp.float32)

    def scores(j, s_ref):
        k0 = pl.multiple_of(j * tk, tk)
        for h in range(heads):
            cols = slice(2 * h * HEAD, 2 * (h + 1) * HEAD)
            s_ref[h] = lax.dot_general(q_ref[:, cols], k_ref[pl.ds(k0, tk), cols],
                                       (((1,), (1,)), ((), ())),
                                       preferred_element_type=jnp.float32)

    def update(h, rows, s, v):
        m_prev = m_ref[h, rows, :]
        m_new = jnp.maximum(m_prev, jnp.max(s, axis=-1, keepdims=True))
        alpha = jnp.exp2(m_prev - m_new)
        p = jnp.exp2(s - jnp.tile(m_new, (1, s.shape[1] // HEAD)))
        pv = jnp.dot(p.astype(v.dtype), v, preferred_element_type=jnp.float32)
        acc_ref[h, rows, :] = jnp.tile(alpha, (1, 2)) * acc_ref[h, rows, :] + pv
        m_ref[h, rows, :] = m_new

    def softmax_pv(j, s_ref, masked):
        k0 = pl.multiple_of(j * tk, tk)
        hq = tq // 2
        for h in range(heads):
            cols = slice(2 * h * HEAD, 2 * (h + 1) * HEAD)
            if not masked:
                update(h, slice(None), s_ref[h], v_ref[pl.ds(k0, tk), cols])
                continue
            row = lax.broadcasted_iota(jnp.int32, (hq, hq), 0)
            col = lax.broadcasted_iota(jnp.int32, (hq, hq), 1)
            tri = col <= row
            s0 = jnp.where(tri, s_ref[h, 0:hq, 0:hq], NEG)
            update(h, slice(0, hq), s0, v_ref[pl.ds(k0, hq), cols])
            s1 = jnp.concatenate([s_ref[h, hq:, 0:hq], jnp.where(tri, s_ref[h, hq:, hq:], NEG)], axis=1)
            update(h, slice(hq, tq), s1, v_ref[pl.ds(k0, tk), cols])

    scores(0, sa_ref)

    def body(t, carry):
        j = 2 * t
        scores(j + 1, sb_ref)
        softmax_pv(j, sa_ref, False)
        scores(j + 2, sa_ref)
        softmax_pv(j + 1, sb_ref, False)
        return carry
    lax.fori_loop(0, i // 2, body, 0)

    @pl.when(i % 2 == 0)
    def _():
        softmax_pv(i, sa_ref, True)

    @pl.when(i % 2 == 1)
    def _():
        scores(i, sb_ref)
        softmax_pv(i - 1, sa_ref, False)
        softmax_pv(i, sb_ref, True)

    for h in range(heads):
        acc = acc_ref[h]
        o_ref[:, h * HEAD:(h + 1) * HEAD] = (acc[:, :HEAD] / acc[:, HEAD:]).astype(o_ref.dtype)


def _mla_attn(q, k, v, *, batch, seq, tq=512, tk=512, heads=4):
    nq = seq // tq
    w = 2 * HEAD * heads
    return pl.pallas_call(
        functools.partial(_mla_attn_kernel, tq=tq, tk=tk, heads=heads),
        grid=(batch, N_HEADS // heads, nq),
        in_specs=[
            pl.BlockSpec((tq, w), lambda b, h, i: (b * nq + i, h)),
            pl.BlockSpec((seq, w), lambda b, h, i: (b, h)),
            pl.BlockSpec((seq, w), lambda b, h, i: (b, h)),
        ],
        out_specs=pl.BlockSpec((tq, HEAD * heads), lambda b, h, i: (b * nq + i, h)),
        out_shape=jax.ShapeDtypeStruct((batch * seq, WIDTH), jnp.bfloat16),
        scratch_shapes=[
            pltpu.VMEM((heads, tq, HEAD), jnp.float32),
            pltpu.VMEM((heads, tq, 2 * HEAD), jnp.float32),
            pltpu.VMEM((heads, tq, tk), jnp.float32),
            pltpu.VMEM((heads, tq, tk), jnp.float32),
        ],
        compiler_params=pltpu.CompilerParams(
            dimension_semantics=("parallel", "parallel", "arbitrary"),
            vmem_limit_bytes=VMEM_LIMIT),
        name="mla_attn",
    )(q, k, v)


def _dil_attn_kernel(q_ref, k_ref, v_ref, cos_ref, sin_ref, gq_ref, gk_ref, o_ref,
                     qd_ref, kd_ref, vd_ref, on_ref, ls_ref, bias_ref, st_ref, *, seq, unroll):
    chunk = 512
    npat = len(DILATIONS)
    assert DILATIONS == (1, 4, 16)
    nat = [on_ref.at[a] for a in range(3)]
    by4 = [ls_ref.at[a] for a in range(3)]
    dst = [qd_ref, kd_ref, vd_ref]
    lead = [0, QBLK, QBLK]

    @pl.when((pl.program_id(0) == 0) & (pl.program_id(1) == 0))
    def _():
        qi = lax.broadcasted_iota(jnp.int32, (QBLK, 2 * QBLK), 0)
        kj = lax.broadcasted_iota(jnp.int32, (QBLK, 2 * QBLK), 1)
        band = (kj >= qi) & (kj <= qi + N_BACK)
        bias_ref[0] = jnp.where(band, 0.0, NEG)
        bias_ref[1] = jnp.where(band & (kj >= QBLK), 0.0, NEG)
        for p in range(npat):
            vd_ref[p, QBLK:, HEAD:] = jnp.ones((seq, HEAD), vd_ref.dtype)
            kd_ref[p, 0:QBLK, :] = jnp.zeros((QBLK, HEAD), kd_ref.dtype)
            vd_ref[p, 0:QBLK, :] = jnp.zeros((QBLK, 2 * HEAD), vd_ref.dtype)

    def prep(c, carry):
        r0 = pl.multiple_of(c * chunk, chunk)
        cos = cos_ref[pl.ds(r0, chunk), :]
        sin = sin_ref[pl.ds(r0, chunk), :]
        q = q_ref[pl.ds(r0, chunk), :].astype(jnp.float32)
        q = q * _rms_scale_mxu(q, HEAD) * gq_ref[...]
        q = q * cos + pltpu.roll(q, 64, 1) * sin
        k = k_ref[pl.ds(r0, chunk), :].astype(jnp.float32)
        k = k * _rms_scale_mxu(k, HEAD) * gk_ref[...]
        k = k * cos + pltpu.roll(k, 64, 1) * sin
        v = v_ref[pl.ds(r0, chunk), :]
        nat[0][pl.ds(r0, chunk), :] = q
        nat[1][pl.ds(r0, chunk), :] = k
        nat[2][pl.ds(r0, chunk), :] = v.astype(jnp.float32)
        qd_ref[0, pl.ds(r0, chunk), :] = q.astype(qd_ref.dtype)
        kd_ref[0, pl.ds(QBLK + r0, chunk), :] = k.astype(kd_ref.dtype)
        vd_ref[0, pl.ds(QBLK + r0, chunk), 0:HEAD] = v
        return carry
    lax.fori_loop(0, seq // chunk, prep, 0, unroll=2)

    sub4, sub16 = seq // 4, seq // 16
    for a in range(3):
        for r4 in range(4):
            x = nat[a][pl.ds(r4, sub4, stride=4), :]
            by4[a][r4 * sub4:(r4 + 1) * sub4, :] = x
            dst[a][1, lead[a] + r4 * sub4:lead[a] + (r4 + 1) * sub4, 0:HEAD] = x.astype(dst[a].dtype)
    for a in range(3):
        for r4 in range(4):
            for j in range(4):
                r16 = r4 + 4 * j
                x = by4[a][pl.ds(r4 * sub4 + j, sub16, stride=4), :]
                dst[a][2, lead[a] + r16 * sub16:lead[a] + (r16 + 1) * sub16, 0:HEAD] = x.astype(dst[a].dtype)

    for p, d in enumerate(DILATIONS):
        nb = seq // d // QBLK
        shift = int(math.log2(nb))

        def block(g, carry, p=p, d=d, nb=nb, shift=shift):
            g0 = pl.multiple_of(g * QBLK, QBLK)
            q = qd_ref[p, pl.ds(g0, QBLK), :]
            k = kd_ref[p, pl.ds(g0, 2 * QBLK), :]
            v = vd_ref[p, pl.ds(g0, 2 * QBLK), :]
            n = g & (nb - 1)
            s = lax.dot_general(q, k, (((1,), (1,)), ((), ())), preferred_element_type=jnp.float32)
            s = s + bias_ref[jnp.where(n == 0, 1, 0)]
            m = jnp.max(s, axis=-1, keepdims=True)
            e = jnp.exp2(s - m)
            pv = jnp.dot(e.astype(v.dtype), v, preferred_element_type=jnp.float32)
            den = pv[:, HEAD:]
            o = pv[:, :HEAD] / den
            lse = m + jnp.log2(den)
            r = g >> shift
            if d == 16:
                rows = pl.ds((r & 3) * sub4 + (r >> 2) + n * (QBLK * 4), QBLK, stride=4)
                st_ref[0, rows, :] = o
                st_ref[1, rows, :] = lse
            else:
                rows = pl.ds(r + n * (QBLK * d), QBLK, stride=d) if d > 1 else pl.ds(g0, QBLK)
                on_ref[p, rows, :] = o
                ls_ref[p, rows, :] = lse
            return carry
        lax.fori_loop(0, seq // QBLK, block, 0, unroll=unroll)

    for a, out in enumerate((on_ref, ls_ref)):
        for r4 in range(4):
            out[2, pl.ds(r4, sub4, stride=4), :] = st_ref[a, r4 * sub4:(r4 + 1) * sub4, :]

    def mix(c, carry):
        r0 = pl.multiple_of(c * chunk, chunk)
        ls = [ls_ref[p, pl.ds(r0, chunk), :] for p in range(npat)]
        mx = functools.reduce(jnp.maximum, ls)
        w = [jnp.exp2(l - mx) for l in ls]
        num = sum(w[p] * on_ref[p, pl.ds(r0, chunk), :] for p in range(npat))
        o_ref[pl.ds(r0, chunk), :] = (num / sum(w)).astype(o_ref.dtype)
        return carry
    lax.fori_loop(0, seq // chunk, mix, 0, unroll=2)


def _dil_attn(proj, cos, sin, gq, gk, *, batch, seq):
    npat = len(DILATIONS)
    col = HEAD

    def qcol(h):
        return jnp.where(h < 4, 1536 // col + h, 4096 // col - 4 + h)

    const = lambda b, h: (0, 0)
    return pl.pallas_call(
        functools.partial(_dil_attn_kernel, seq=seq, unroll=32),
        grid=(batch, N_HEADS),
        in_specs=[
            pl.BlockSpec((seq, HEAD), lambda b, h: (b, qcol(h))),
            pl.BlockSpec((seq, HEAD), lambda b, h: (b, 4608 // col + h)),
            pl.BlockSpec((seq, HEAD), lambda b, h: (b, 5632 // col + h)),
            pl.BlockSpec((seq, HEAD), const),
            pl.BlockSpec((seq, HEAD), const),
            pl.BlockSpec((1, HEAD), const),
            pl.BlockSpec((1, HEAD), const),
        ],
        out_specs=pl.BlockSpec((seq, HEAD), lambda b, h: (b, h)),
        out_shape=jax.ShapeDtypeStruct((batch * seq, WIDTH), jnp.bfloat16),
        scratch_shapes=[
            pltpu.VMEM((npat, seq, HEAD), jnp.bfloat16),
            pltpu.VMEM((npat, seq + QBLK, HEAD), jnp.bfloat16),
            pltpu.VMEM((npat, seq + QBLK, 2 * HEAD), jnp.bfloat16),
            pltpu.VMEM((npat, seq, HEAD), jnp.float32),
            pltpu.VMEM((npat, seq, HEAD), jnp.float32),
            pltpu.VMEM((2, QBLK, 2 * QBLK), jnp.float32),
            pltpu.VMEM((2, seq, HEAD), jnp.float32),
        ],
        compiler_params=pltpu.CompilerParams(
            dimension_semantics=("arbitrary", "arbitrary"), vmem_limit_bytes=VMEM_LIMIT),
        name="dil_attn",
    )(proj, proj, proj, cos, sin, gq, gk)


def _out_proj_kernel(oa_ref, ob_ref, g_ref, x_ref, ga_ref, gb_ref, w32_ref, o_ref, w_ref, *, nsplit):
    @pl.when(pl.program_id(0) == 0)
    def _():
        rows = 256

        def cast(c, carry):
            r0 = pl.multiple_of(c * rows, rows)
            w_ref[pl.ds(r0, rows), :] = w32_ref[pl.ds(r0, rows), :].astype(w_ref.dtype)
            return carry
        lax.fori_loop(0, w_ref.shape[0] // rows, cast, 0)

    def branch(o, gain_ref, gate):
        o = o.astype(jnp.float32)
        gate = gate.astype(jnp.float32)
        y = o * _rms_scale(o, WIDTH) * gain_ref[...]
        return (y * (gate / (1.0 + jnp.exp(-gate)))).astype(jnp.bfloat16)

    step = o_ref.shape[0] // nsplit
    for r in range(0, o_ref.shape[0], step):
        rows = slice(r, r + step)
        ya = branch(oa_ref[rows, :], ga_ref, g_ref[rows, :WIDTH])
        yb = branch(ob_ref[rows, :], gb_ref, g_ref[rows, WIDTH:])
        y = jnp.concatenate([ya, yb], axis=-1)
        o_ref[rows, :] = x_ref[rows, :] + jnp.dot(y, w_ref[...], preferred_element_type=jnp.float32)


def _out_proj(o_a, o_b, proj, x2, ga, gb, w, *, tm=512):
    m = x2.shape[0]
    const = lambda i: (0, 0)
    return pl.pallas_call(
        functools.partial(_out_proj_kernel, nsplit=2),
        grid=(m // tm,),
        in_specs=[
            pl.BlockSpec((tm, WIDTH), lambda i: (i, 0)),
            pl.BlockSpec((tm, WIDTH), lambda i: (i, 0)),
            pl.BlockSpec((tm, 2 * WIDTH), lambda i: (i, 1)),
            pl.BlockSpec((tm, D_MODEL), lambda i: (i, 0)),
            pl.BlockSpec((1, WIDTH), const),
            pl.BlockSpec((1, WIDTH), const),
            pl.BlockSpec((2 * WIDTH, D_MODEL), const, pipeline_mode=pl.Buffered(1)),
        ],
        out_specs=pl.BlockSpec((tm, D_MODEL), lambda i: (i, 0)),
        out_shape=jax.ShapeDtypeStruct((m, D_MODEL), jnp.float32),
        scratch_shapes=[pltpu.VMEM((2 * WIDTH, D_MODEL), jnp.bfloat16)],
        compiler_params=pltpu.CompilerParams(
            dimension_semantics=("arbitrary",), vmem_limit_bytes=VMEM_LIMIT),
        name="out_proj",
    )(o_a, o_b, proj, x2, ga, gb, w)


def _pad_rope_cols(a):
    z = jnp.zeros(a.shape[:-1] + (ROPE // 2,), a.dtype)
    return jnp.concatenate([a[..., :ROPE // 2], z, a[..., ROPE // 2:], z], axis=-1)


def _rope_tables(seq, d, padded):
    inv = ROPE_THETA ** (-np.arange(0, d, 2, dtype=np.float64) / d)
    ang = np.arange(seq, dtype=np.float64)[:, None] * inv[None, :]
    cos, sin = np.cos(ang), np.sin(ang)
    cos2 = np.concatenate([cos, cos], axis=-1)
    sin2 = np.concatenate([-sin, sin], axis=-1)
    if padded:
        z = np.zeros((seq, d // 2))
        cos2 = np.concatenate([cos, z, cos, z], axis=-1)
        sin2 = np.concatenate([-sin, z, sin, z], axis=-1)
    return jnp.asarray(cos2, jnp.float32), jnp.asarray(sin2, jnp.float32)


def kernel(x, norm_gain, w_in, q_a_norm_gain, kv_a_norm_gain, w_uq, w_ukv, mla_q_norm_gain,
           mla_k_norm_gain, dil_q_norm_gain, dil_k_norm_gain, mla_out_norm_gain,
           dil_out_norm_gain, w_out):
    batch, seq, _ = x.shape
    depth = w_in.shape[0]
    bf = jnp.bfloat16
    cos_d, sin_d = _rope_tables(seq, HEAD, padded=False)
    cos_m, sin_m = _rope_tables(seq, ROPE, padded=True)
    h2 = x.reshape(batch * seq, D_MODEL)

    for l in range(depth):
        wq = w_uq[l].reshape(Q_RANK, N_HEADS, MLA_QK)
        wq_r = jnp.concatenate([wq[..., :HEAD], _pad_rope_cols(wq[..., HEAD:])], axis=-1)
        wq_r = wq_r.reshape(Q_RANK, N_HEADS * 2 * HEAD).astype(bf)
        wkv = w_ukv[l].reshape(KV_RANK, N_HEADS, 2 * HEAD)
        wkv_r = jnp.concatenate([wkv[..., :HEAD].reshape(KV_RANK, WIDTH),
                                 wkv[..., HEAD:].reshape(KV_RANK, WIDTH)], axis=-1).astype(bf)

        gq_m = mla_q_norm_gain[l]
        gk_m = mla_k_norm_gain[l]
        gq_full = jnp.concatenate([gq_m[:HEAD], _pad_rope_cols(gq_m[HEAD:])])[None, :]
        gq_full = gq_full * (math.log2(math.e) / math.sqrt(MLA_QK))
        gk_full = jnp.concatenate([gk_m[:HEAD], _pad_rope_cols(gk_m[HEAD:])])[None, :]
        gq_d = dil_q_norm_gain[l][None, :] * (math.log2(math.e) / math.sqrt(HEAD))
        gk_d = dil_k_norm_gain[l][None, :]

        proj = _in_proj(h2, norm_gain[l][None, :], jnp.swapaxes(w_in[l], 0, 1))
        q_f, k_f, v_a = _mla_prep(proj, wq_r, wkv_r, q_a_norm_gain[l][None, :],
                                  kv_a_norm_gain[l][None, :], gq_full, gk_full, cos_m, sin_m, seq=seq)
        o_a = _mla_attn(q_f, k_f, v_a, batch=batch, seq=seq)
        o_b = _dil_attn(proj, cos_d, sin_d, gq_d, gk_d, batch=batch, seq=seq)
        h2 = _out_proj(o_a, o_b, proj, h2, mla_out_norm_gain[l][None, :],
                       dil_out_norm_gain[l][None, :], w_out[l])
    return h2.reshape(batch, seq, D_MODEL)
```

```python
import functools
import math

import jax
import jax.numpy as jnp
import numpy as np
from jax import lax
from jax.experimental import pallas as pl
from jax.experimental.pallas import tpu as pltpu

EPS = 1e-6
ROPE_THETA = 10000.0
NEG = -1e30

D_MODEL = 2048
N_HEADS = 8
HEAD = 128
ROPE = 64
Q_RANK = 768
KV_RANK = 512
WIDTH = N_HEADS * HEAD
MLA_QK = HEAD + ROPE
QBLK = 128
N_BACK = 128
DILATIONS = (1, 4, 16)

C_CKV = 0
C_KR = C_CKV + KV_RANK
C_CQ = C_KR + 2 * HEAD
C_QB_LO = C_CQ + Q_RANK
C_GATE = C_QB_LO + 4 * HEAD
C_QB_HI = C_GATE + 2 * WIDTH
C_KB = C_QB_HI + 4 * HEAD
C_VB = C_KB + WIDTH
PROJ_COLS = C_VB + WIDTH
LAT_COLS = C_QB_LO
assert C_GATE % (2 * WIDTH) == 0 and PROJ_COLS == 6656

V7X_VMEM_BYTES = 64 * 1024 * 1024
VMEM_LIMIT = V7X_VMEM_BYTES - 8 * 1024 * 1024
IN_PROJ_VMEM_LIMIT = V7X_VMEM_BYTES - 4 * 1024 * 1024


def _rms_scale(x, n):
    return lax.rsqrt(jnp.sum(x * x, axis=-1, keepdims=True) * (1.0 / n) + EPS)


def _rms_scale_mxu(x, n):
    ones = jnp.ones((x.shape[-1], x.shape[-1]), jnp.bfloat16)
    ss = jnp.dot((x * x).astype(jnp.bfloat16), ones, preferred_element_type=jnp.float32)
    return lax.rsqrt(ss * (1.0 / n) + EPS)


_O_CQ, _O_CKV, _O_KR = 0, Q_RANK, Q_RANK + KV_RANK
_O_GA = _O_KR + ROPE
_O_QB = _O_GA + WIDTH
_O_KB, _O_VB, _O_GB = _O_QB + WIDTH, _O_QB + 2 * WIDTH, _O_QB + 3 * WIDTH
_IN_COLS = _O_GB + WIDTH


W_TILE = 512
_W_TILE_SRC = (_O_CKV, _O_CQ, _O_CQ + 256, _O_QB, _O_GA, _O_GA + 512, _O_GB, _O_GB + 512,
               _O_QB + 512, _O_KB, _O_KB + 512, _O_VB, _O_VB + 512)
assert len(_W_TILE_SRC) * W_TILE == PROJ_COLS


X_CHUNK = 512
N_CHUNKS = 8


def _in_proj_kernel(src_ref, x_ref, g_ref, w_ref, kr_ref, o_ref, hn0_ref, hn1_ref, *, row_chunk, n_tiles):
    del src_ref
    i = pl.program_id(0)
    j = pl.program_id(1)

    def norm(hn_ref):
        base = pl.multiple_of(j * X_CHUNK, X_CHUNK)

        def body(c, carry):
            r0 = pl.multiple_of(c * row_chunk, row_chunk)
            x = x_ref[pl.ds(r0, row_chunk), :]
            hn = x * _rms_scale(x, D_MODEL) * g_ref[...]
            hn_ref[pl.ds(base + r0, row_chunk), :] = hn.astype(hn_ref.dtype)
            return carry
        lax.fori_loop(0, X_CHUNK // row_chunk, body, 0, unroll=4)

    def weight_tile():
        generic = w_ref[...].astype(jnp.bfloat16)
        half = ROPE // 2
        kr = kr_ref[...].astype(jnp.bfloat16)
        z = jnp.zeros((half, D_MODEL), jnp.bfloat16)
        special = jnp.concatenate([kr[:half], z, kr[half:], z, jnp.zeros((HEAD, D_MODEL), jnp.bfloat16),
                                   generic[:2 * HEAD, :]], axis=0)
        return jnp.where(j == 1, special, generic)

    def project(hn_ref):
        w = weight_tile()
        quarter = o_ref.shape[0] // 4
        for r in range(0, o_ref.shape[0], quarter):
            o_ref[r:r + quarter, :] = lax.dot_general(
                hn_ref[r:r + quarter, :], w, (((1,), (1,)), ((), ())),
                preferred_element_type=jnp.float32).astype(o_ref.dtype)

    do_norm = (i < n_tiles) & (j < N_CHUNKS)
    even = (i % 2) == 0

    @pl.when((i == 0) & do_norm)
    def _():
        norm(hn0_ref)

    for is_even, hn_cur, hn_prev in ((True, hn0_ref, hn1_ref), (False, hn1_ref, hn0_ref)):
        parity = even if is_even else jnp.logical_not(even)

        @pl.when((i > 0) & parity & do_norm)
        def _(hn_cur=hn_cur, hn_prev=hn_prev):
            project(hn_prev)
            norm(hn_cur)

        @pl.when((i > 0) & parity & jnp.logical_not(do_norm))
        def _(hn_prev=hn_prev):
            project(hn_prev)


def _in_proj(x2, gain, wt):
    m = x2.shape[0]
    tm = X_CHUNK * N_CHUNKS
    n_tiles = m // tm
    n_col = PROJ_COLS // W_TILE
    assert N_CHUNKS <= n_col
    assert all(s % ROPE == 0 for s in _W_TILE_SRC)
    src = jnp.asarray([s // ROPE for s in _W_TILE_SRC], jnp.int32)

    def x_index(i, j, src):
        return (jnp.where(i < n_tiles, i * N_CHUNKS + jnp.minimum(j, N_CHUNKS - 1), n_tiles * N_CHUNKS - 1), 0)

    def out_index(i, j, src):
        return (jnp.maximum(i - 1, 0), jnp.where(i == 0, 0, j))

    return pl.pallas_call(
        functools.partial(_in_proj_kernel, row_chunk=64, n_tiles=n_tiles),
        grid_spec=pltpu.PrefetchScalarGridSpec(
            num_scalar_prefetch=1,
            grid=(n_tiles + 1, n_col),
            in_specs=[
                pl.BlockSpec((X_CHUNK, D_MODEL), x_index),
                pl.BlockSpec((1, D_MODEL), lambda i, j, src: (0, 0)),
                pl.BlockSpec((pl.Element(W_TILE), pl.Element(D_MODEL)), lambda i, j, src: (src[j] * ROPE, 0)),
                pl.BlockSpec((pl.Element(ROPE), pl.Element(D_MODEL)), lambda i, j, src: (_O_KR, 0)),
            ],
            out_specs=pl.BlockSpec((tm, W_TILE), out_index),
            scratch_shapes=[pltpu.VMEM((tm, D_MODEL), jnp.bfloat16),
                            pltpu.VMEM((tm, D_MODEL), jnp.bfloat16)],
        ),
        out_shape=jax.ShapeDtypeStruct((m, PROJ_COLS), jnp.bfloat16),
        compiler_params=pltpu.CompilerParams(
            dimension_semantics=("arbitrary", "arbitrary"),
            vmem_limit_bytes=IN_PROJ_VMEM_LIMIT),
        name="in_proj",
    )(src, x2, gain, wt, wt)


def _rope_pad(x, cos, sin):
    return x * cos + pltpu.roll(x, 64, 1) * sin


def _mla_prep_kernel(lat_ref, wq_ref, wkv_ref, gqa_ref, gkva_ref, gq_ref, gk_ref,
                     cos_ref, sin_ref, q_ref, k_ref, v_ref):
    cos = cos_ref[...]
    sin = sin_ref[...]
    gq = gq_ref[...]
    gk = gk_ref[...]

    c_kv = lat_ref[:, 0:KV_RANK].astype(jnp.float32)
    cn = (c_kv * _rms_scale(c_kv, KV_RANK) * gkva_ref[...]).astype(jnp.bfloat16)
    kv = jnp.dot(cn, wkv_ref[...], preferred_element_type=jnp.float32)
    ones = jnp.ones((kv.shape[0], HEAD), v_ref.dtype)
    for h in range(N_HEADS):
        v_ref[:, 2 * h * HEAD:(2 * h + 1) * HEAD] = kv[:, WIDTH + h * HEAD:WIDTH + (h + 1) * HEAD].astype(v_ref.dtype)
        v_ref[:, (2 * h + 1) * HEAD:(2 * h + 2) * HEAD] = ones

    k_r = lat_ref[:, KV_RANK:KV_RANK + HEAD].astype(jnp.float32)
    k_r = k_r * _rms_scale(k_r, ROPE) * gk[:, HEAD:]
    k_rope = _rope_pad(k_r, cos, sin).astype(k_ref.dtype)
    for h in range(N_HEADS):
        kn = kv[:, h * HEAD:(h + 1) * HEAD]
        kn = kn * _rms_scale(kn, HEAD) * gk[:, :HEAD]
        k_ref[:, 2 * h * HEAD:(2 * h + 1) * HEAD] = kn.astype(k_ref.dtype)
        k_ref[:, (2 * h + 1) * HEAD:(2 * h + 2) * HEAD] = k_rope

    c_q = lat_ref[:, 2 * HEAD + KV_RANK:].astype(jnp.float32)
    cqn = (c_q * _rms_scale(c_q, Q_RANK) * gqa_ref[...]).astype(jnp.bfloat16)
    q = jnp.dot(cqn, wq_ref[...], preferred_element_type=jnp.float32)
    for h in range(N_HEADS):
        qn = q[:, 2 * h * HEAD:(2 * h + 1) * HEAD]
        qn = qn * _rms_scale(qn, HEAD) * gq[:, :HEAD]
        q_ref[:, 2 * h * HEAD:(2 * h + 1) * HEAD] = qn.astype(q_ref.dtype)
        qr = q[:, (2 * h + 1) * HEAD:(2 * h + 2) * HEAD]
        qr = qr * _rms_scale(qr, ROPE) * gq[:, HEAD:]
        q_ref[:, (2 * h + 1) * HEAD:(2 * h + 2) * HEAD] = _rope_pad(qr, cos, sin).astype(q_ref.dtype)


def _mla_prep(proj, wq, wkv, gqa, gkva, gq, gk, cos, sin, *, seq, tm=256):
    m = proj.shape[0]
    nseq = seq // tm
    const = lambda i: (0, 0)
    return pl.pallas_call(
        _mla_prep_kernel,
        grid=(m // tm,),
        in_specs=[
            pl.BlockSpec((tm, LAT_COLS), lambda i: (i, 0)),
            pl.BlockSpec(wq.shape, const),
            pl.BlockSpec(wkv.shape, const),
            pl.BlockSpec(gqa.shape, const),
            pl.BlockSpec(gkva.shape, const),
            pl.BlockSpec(gq.shape, const),
            pl.BlockSpec(gk.shape, const),
            pl.BlockSpec((tm, HEAD), lambda i: (i % nseq, 0)),
            pl.BlockSpec((tm, HEAD), lambda i: (i % nseq, 0)),
        ],
        out_specs=[
            pl.BlockSpec((tm, 2 * WIDTH), lambda i: (i, 0)),
            pl.BlockSpec((tm, 2 * WIDTH), lambda i: (i, 0)),
            pl.BlockSpec((tm, 2 * WIDTH), lambda i: (i, 0)),
        ],
        out_shape=[
            jax.ShapeDtypeStruct((m, 2 * WIDTH), jnp.bfloat16),
            jax.ShapeDtypeStruct((m, 2 * WIDTH), jnp.bfloat16),
            jax.ShapeDtypeStruct((m, 2 * WIDTH), jnp.bfloat16),
        ],
        compiler_params=pltpu.CompilerParams(
            dimension_semantics=("parallel",), vmem_limit_bytes=VMEM_LIMIT),
        name="mla_prep",
    )(proj, wq, wkv, gqa, gkva, gq, gk, cos, sin)


def _mla_attn_kernel(q_ref, k_ref, v_ref, o_ref, m_ref, acc_ref, sa_ref, sb_ref, *, tq, tk, heads):
    assert tq == tk
    i = pl.program_id(2)
    m_ref[...] = jnp.full(m_ref.shape, NEG, jnp.float32)
    acc_ref[...] = jnp.zeros(acc_ref.shape, jnp.float32)

    def scores(j, s_ref):
        k0 = pl.multiple_of(j * tk, tk)
        for h in range(heads):
            cols = slice(2 * h * HEAD, 2 * (h + 1) * HEAD)
            s_ref[h] = lax.dot_general(q_ref[:, cols], k_ref[pl.ds(k0, tk), cols],
                                       (((1,), (1,)), ((), ())),
                                       preferred_element_type=jnp.float32)

    def update(h, rows, s, v):
        m_prev = m_ref[h, rows, :]
        m_new = jnp.maximum(m_prev, jnp.max(s, axis=-1, keepdims=True))
        alpha = jnp.exp2(m_prev - m_new)
        p = jnp.exp2(s - jnp.tile(m_new, (1, s.shape[1] // HEAD)))
        pv = jnp.dot(p.astype(v.dtype), v, preferred_element_type=jnp.float32)
        acc_ref[h, rows, :] = jnp.tile(alpha, (1, 2)) * acc_ref[h, rows, :] + pv
        m_ref[h, rows, :] = m_new

    def softmax_pv(j, s_ref, masked):
        k0 = pl.multiple_of(j * tk, tk)
        hq = tq // 2
        for h in range(heads):
            cols = slice(2 * h * HEAD, 2 * (h + 1) * HEAD)
            if not masked:
                update(h, slice(None), s_ref[h], v_ref[pl.ds(k0, tk), cols])
                continue
            row = lax.broadcasted_iota(jnp.int32, (hq, hq), 0)
            col = lax.broadcasted_iota(jnp.int32, (hq, hq), 1)
            tri = col <= row
            s0 = jnp.where(tri, s_ref[h, 0:hq, 0:hq], NEG)
            update(h, slice(0, hq), s0, v_ref[pl.ds(k0, hq), cols])
            s1 = jnp.concatenate([s_ref[h, hq:, 0:hq], jnp.where(tri, s_ref[h, hq:, hq:], NEG)], axis=1)
            update(h, slice(hq, tq), s1, v_ref[pl.ds(k0, tk), cols])

    scores(0, sa_ref)

    def body(t, carry):
        j = 2 * t
        scores(j + 1, sb_ref)
        softmax_pv(j, sa_ref, False)
        scores(j + 2, sa_ref)
        softmax_pv(j + 1, sb_ref, False)
        return carry
    lax.fori_loop(0, i // 2, body, 0)

    @pl.when(i % 2 == 0)
    def _():
        softmax_pv(i, sa_ref, True)

    @pl.when(i % 2 == 1)
    def _():
        scores(i, sb_ref)
        softmax_pv(i - 1, sa_ref, False)
        softmax_pv(i, sb_ref, True)

    for h in range(heads):
        acc = acc_ref[h]
        o_ref[:, h * HEAD:(h + 1) * HEAD] = (acc[:, :HEAD] / acc[:, HEAD:]).astype(o_ref.dtype)


def _mla_attn(q, k, v, *, batch, seq, tq=512, tk=512, heads=4):
    nq = seq // tq
    w = 2 * HEAD * heads
    return pl.pallas_call(
        functools.partial(_mla_attn_kernel, tq=tq, tk=tk, heads=heads),
        grid=(batch, N_HEADS // heads, nq),
        in_specs=[
            pl.BlockSpec((tq, w), lambda b, h, i: (b * nq + i, h)),
            pl.BlockSpec((seq, w), lambda b, h, i: (b, h)),
            pl.BlockSpec((seq, w), lambda b, h, i: (b, h)),
        ],
        out_specs=pl.BlockSpec((tq, HEAD * heads), lambda b, h, i: (b * nq + i, h)),
        out_shape=jax.ShapeDtypeStruct((batch * seq, WIDTH), jnp.bfloat16),
        scratch_shapes=[
            pltpu.VMEM((heads, tq, HEAD), jnp.float32),
            pltpu.VMEM((heads, tq, 2 * HEAD), jnp.float32),
            pltpu.VMEM((heads, tq, tk), jnp.float32),
            pltpu.VMEM((heads, tq, tk), jnp.float32),
        ],
        compiler_params=pltpu.CompilerParams(
            dimension_semantics=("parallel", "parallel", "arbitrary"),
            vmem_limit_bytes=VMEM_LIMIT),
        name="mla_attn",
    )(q, k, v)


def _dil_attn_kernel(q_ref, k_ref, v_ref, cos_ref, sin_ref, gq_ref, gk_ref, o_ref,
                     qd_ref, kd_ref, vd_ref, on_ref, ls_ref, bias_ref, st_ref, *, seq, unroll):
    chunk = 512
    npat = len(DILATIONS)
    assert DILATIONS == (1, 4, 16)
    nat = [on_ref.at[a] for a in range(3)]
    by4 = [ls_ref.at[a] for a in range(3)]
    dst = [qd_ref, kd_ref, vd_ref]
    lead = [0, QBLK, QBLK]

    @pl.when((pl.program_id(0) == 0) & (pl.program_id(1) == 0))
    def _():
        qi = lax.broadcasted_iota(jnp.int32, (QBLK, 2 * QBLK), 0)
        kj = lax.broadcasted_iota(jnp.int32, (QBLK, 2 * QBLK), 1)
        band = (kj >= qi) & (kj <= qi + N_BACK)
        bias_ref[0] = jnp.where(band, 0.0, NEG)
        bias_ref[1] = jnp.where(band & (kj >= QBLK), 0.0, NEG)
        for p in range(npat):
            vd_ref[p, QBLK:, HEAD:] = jnp.ones((seq, HEAD), vd_ref.dtype)
            kd_ref[p, 0:QBLK, :] = jnp.zeros((QBLK, HEAD), kd_ref.dtype)
            vd_ref[p, 0:QBLK, :] = jnp.zeros((QBLK, 2 * HEAD), vd_ref.dtype)

    def prep(c, carry):
        r0 = pl.multiple_of(c * chunk, chunk)
        cos = cos_ref[pl.ds(r0, chunk), :]
        sin = sin_ref[pl.ds(r0, chunk), :]
        q = q_ref[pl.ds(r0, chunk), :].astype(jnp.float32)
        q = q * _rms_scale_mxu(q, HEAD) * gq_ref[...]
        q = q * cos + pltpu.roll(q, 64, 1) * sin
        k = k_ref[pl.ds(r0, chunk), :].astype(jnp.float32)
        k = k * _rms_scale_mxu(k, HEAD) * gk_ref[...]
        k = k * cos + pltpu.roll(k, 64, 1) * sin
        v = v_ref[pl.ds(r0, chunk), :]
        nat[0][pl.ds(r0, chunk), :] = q
        nat[1][pl.ds(r0, chunk), :] = k
        nat[2][pl.ds(r0, chunk), :] = v.astype(jnp.float32)
        qd_ref[0, pl.ds(r0, chunk), :] = q.astype(qd_ref.dtype)
        kd_ref[0, pl.ds(QBLK + r0, chunk), :] = k.astype(kd_ref.dtype)
        vd_ref[0, pl.ds(QBLK + r0, chunk), 0:HEAD] = v
        return carry
    lax.fori_loop(0, seq // chunk, prep, 0, unroll=2)

    sub4, sub16 = seq // 4, seq // 16
    for a in range(3):
        for r4 in range(4):
            x = nat[a][pl.ds(r4, sub4, stride=4), :]
            by4[a][r4 * sub4:(r4 + 1) * sub4, :] = x
            dst[a][1, lead[a] + r4 * sub4:lead[a] + (r4 + 1) * sub4, 0:HEAD] = x.astype(dst[a].dtype)
    for a in range(3):
        for r4 in range(4):
            for j in range(4):
                r16 = r4 + 4 * j
                x = by4[a][pl.ds(r4 * sub4 + j, sub16, stride=4), :]
                dst[a][2, lead[a] + r16 * sub16:lead[a] + (r16 + 1) * sub16, 0:HEAD] = x.astype(dst[a].dtype)

    for p, d in enumerate(DILATIONS):
        nb = seq // d // QBLK
        shift = int(math.log2(nb))

        def block(g, carry, p=p, d=d, nb=nb, shift=shift):
            g0 = pl.multiple_of(g * QBLK, QBLK)
            q = qd_ref[p, pl.ds(g0, QBLK), :]
            k = kd_ref[p, pl.ds(g0, 2 * QBLK), :]
            v = vd_ref[p, pl.ds(g0, 2 * QBLK), :]
            n = g & (nb - 1)
            s = lax.dot_general(q, k, (((1,), (1,)), ((), ())), preferred_element_type=jnp.float32)
            s = s + bias_ref[jnp.where(n == 0, 1, 0)]
            m = jnp.max(s, axis=-1, keepdims=True)
            e = jnp.exp2(s - m)
            pv = jnp.dot(e.astype(v.dtype), v, preferred_element_type=jnp.float32)
            den = pv[:, HEAD:]
            o = pv[:, :HEAD] / den
            lse = m + jnp.log2(den)
            r = g >> shift
            if d == 16:
                rows = pl.ds((r & 3) * sub4 + (r >> 2) + n * (QBLK * 4), QBLK, stride=4)
                st_ref[0, rows, :] = o
                st_ref[1, rows, :] = lse
            else:
                rows = pl.ds(r + n * (QBLK * d), QBLK, stride=d) if d > 1 else pl.ds(g0, QBLK)
                on_ref[p, rows, :] = o
                ls_ref[p, rows, :] = lse
            return carry
        lax.fori_loop(0, seq // QBLK, block, 0, unroll=unroll)

    for a, out in enumerate((on_ref, ls_ref)):
        for r4 in range(4):
            out[2, pl.ds(r4, sub4, stride=4), :] = st_ref[a, r4 * sub4:(r4 + 1) * sub4, :]

    def mix(c, carry):
        r0 = pl.multiple_of(c * chunk, chunk)
        ls = [ls_ref[p, pl.ds(r0, chunk), :] for p in range(npat)]
        mx = functools.reduce(jnp.maximum, ls)
        w = [jnp.exp2(l - mx) for l in ls]
        num = sum(w[p] * on_ref[p, pl.ds(r0, chunk), :] for p in range(npat))
        o_ref[pl.ds(r0, chunk), :] = (num / sum(w)).astype(o_ref.dtype)
        return carry
    lax.fori_loop(0, seq // chunk, mix, 0, unroll=2)


def _dil_attn(proj, cos, sin, gq, gk, *, batch, seq):
    npat = len(DILATIONS)

    def qcol(h):
        return jnp.where(h < 4, C_QB_LO // HEAD + h, C_QB_HI // HEAD - 4 + h)

    const = lambda b, h: (0, 0)
    return pl.pallas_call(
        functools.partial(_dil_attn_kernel, seq=seq, unroll=seq // QBLK),
        grid=(batch, N_HEADS),
        in_specs=[
            pl.BlockSpec((seq, HEAD), lambda b, h: (b, qcol(h))),
            pl.BlockSpec((seq, HEAD), lambda b, h: (b, C_KB // HEAD + h)),
            pl.BlockSpec((seq, HEAD), lambda b, h: (b, C_VB // HEAD + h)),
            pl.BlockSpec((seq, HEAD), const),
            pl.BlockSpec((seq, HEAD), const),
            pl.BlockSpec((1, HEAD), const),
            pl.BlockSpec((1, HEAD), const),
        ],
        out_specs=pl.BlockSpec((seq, HEAD), lambda b, h: (b, h)),
        out_shape=jax.ShapeDtypeStruct((batch * seq, WIDTH), jnp.bfloat16),
        scratch_shapes=[
            pltpu.VMEM((npat, seq, HEAD), jnp.bfloat16),
            pltpu.VMEM((npat, seq + QBLK, HEAD), jnp.bfloat16),
            pltpu.VMEM((npat, seq + QBLK, 2 * HEAD), jnp.bfloat16),
            pltpu.VMEM((npat, seq, HEAD), jnp.float32),
            pltpu.VMEM((npat, seq, HEAD), jnp.float32),
            pltpu.VMEM((2, QBLK, 2 * QBLK), jnp.float32),
            pltpu.VMEM((2, seq, HEAD), jnp.float32),
        ],
        compiler_params=pltpu.CompilerParams(
            dimension_semantics=("arbitrary", "arbitrary"), vmem_limit_bytes=VMEM_LIMIT),
        name="dil_attn",
    )(proj, proj, proj, cos, sin, gq, gk)


def _out_proj_kernel(oa_ref, ob_ref, g_ref, x_ref, ga_ref, gb_ref, w32_ref, o_ref, w_ref, *, nsplit):
    @pl.when(pl.program_id(0) == 0)
    def _():
        rows = 256

        def cast(c, carry):
            r0 = pl.multiple_of(c * rows, rows)
            w_ref[pl.ds(r0, rows), :] = w32_ref[pl.ds(r0, rows), :].astype(w_ref.dtype)
            return carry
        lax.fori_loop(0, w_ref.shape[0] // rows, cast, 0)

    def branch(o, gain_ref, gate):
        o = o.astype(jnp.float32)
        gate = gate.astype(jnp.float32)
        y = o * _rms_scale(o, WIDTH) * gain_ref[...]
        return (y * (gate / (1.0 + jnp.exp(-gate)))).astype(jnp.bfloat16)

    step = o_ref.shape[0] // nsplit
    for r in range(0, o_ref.shape[0], step):
        rows = slice(r, r + step)
        ya = branch(oa_ref[rows, :], ga_ref, g_ref[rows, :WIDTH])
        yb = branch(ob_ref[rows, :], gb_ref, g_ref[rows, WIDTH:])
        y = jnp.concatenate([ya, yb], axis=-1)
        o_ref[rows, :] = x_ref[rows, :] + jnp.dot(y, w_ref[...], preferred_element_type=jnp.float32)


def _out_proj(o_a, o_b, proj, x2, ga, gb, w, *, tm=512):
    m = x2.shape[0]
    const = lambda i: (0, 0)
    return pl.pallas_call(
        functools.partial(_out_proj_kernel, nsplit=2),
        grid=(m // tm,),
        in_specs=[
            pl.BlockSpec((tm, WIDTH), lambda i: (i, 0)),
            pl.BlockSpec((tm, WIDTH), lambda i: (i, 0)),
            pl.BlockSpec((tm, 2 * WIDTH), lambda i: (i, C_GATE // (2 * WIDTH))),
            pl.BlockSpec((tm, D_MODEL), lambda i: (i, 0)),
            pl.BlockSpec((1, WIDTH), const),
            pl.BlockSpec((1, WIDTH), const),
            pl.BlockSpec((2 * WIDTH, D_MODEL), const, pipeline_mode=pl.Buffered(1)),
        ],
        out_specs=pl.BlockSpec((tm, D_MODEL), lambda i: (i, 0)),
        out_shape=jax.ShapeDtypeStruct((m, D_MODEL), jnp.float32),
        scratch_shapes=[pltpu.VMEM((2 * WIDTH, D_MODEL), jnp.bfloat16)],
        compiler_params=pltpu.CompilerParams(
            dimension_semantics=("arbitrary",), vmem_limit_bytes=VMEM_LIMIT),
        name="out_proj",
    )(o_a, o_b, proj, x2, ga, gb, w)


def _pad_rope_cols(a):
    z = jnp.zeros(a.shape[:-1] + (ROPE // 2,), a.dtype)
    return jnp.concatenate([a[..., :ROPE // 2], z, a[..., ROPE // 2:], z], axis=-1)


def _rope_tables(seq, d, padded):
    inv = ROPE_THETA ** (-np.arange(0, d, 2, dtype=np.float64) / d)
    ang = np.arange(seq, dtype=np.float64)[:, None] * inv[None, :]
    cos, sin = np.cos(ang), np.sin(ang)
    cos2 = np.concatenate([cos, cos], axis=-1)
    sin2 = np.concatenate([-sin, sin], axis=-1)
    if padded:
        z = np.zeros((seq, d // 2))
        cos2 = np.concatenate([cos, z, cos, z], axis=-1)
        sin2 = np.concatenate([-sin, z, sin, z], axis=-1)
    return jnp.asarray(cos2, jnp.float32), jnp.asarray(sin2, jnp.float32)


def kernel(x, norm_gain, w_in, q_a_norm_gain, kv_a_norm_gain, w_uq, w_ukv, mla_q_norm_gain,
           mla_k_norm_gain, dil_q_norm_gain, dil_k_norm_gain, mla_out_norm_gain,
           dil_out_norm_gain, w_out):
    batch, seq, _ = x.shape
    depth = w_in.shape[0]
    bf = jnp.bfloat16
    cos_d, sin_d = _rope_tables(seq, HEAD, padded=False)
    cos_m, sin_m = _rope_tables(seq, ROPE, padded=True)
    h2 = x.reshape(batch * seq, D_MODEL)

    for l in range(depth):
        wq = w_uq[l].reshape(Q_RANK, N_HEADS, MLA_QK)
        wq_r = jnp.concatenate([wq[..., :HEAD], _pad_rope_cols(wq[..., HEAD:])], axis=-1)
        wq_r = wq_r.reshape(Q_RANK, N_HEADS * 2 * HEAD).astype(bf)
        wkv = w_ukv[l].reshape(KV_RANK, N_HEADS, 2 * HEAD)
        wkv_r = jnp.concatenate([wkv[..., :HEAD].reshape(KV_RANK, WIDTH),
                                 wkv[..., HEAD:].reshape(KV_RANK, WIDTH)], axis=-1).astype(bf)

        gq_m = mla_q_norm_gain[l]
        gk_m = mla_k_norm_gain[l]
        gq_full = jnp.concatenate([gq_m[:HEAD], _pad_rope_cols(gq_m[HEAD:])])[None, :]
        gq_full = gq_full * (math.log2(math.e) / math.sqrt(MLA_QK))
        gk_full = jnp.concatenate([gk_m[:HEAD], _pad_rope_cols(gk_m[HEAD:])])[None, :]
        gq_d = dil_q_norm_gain[l][None, :] * (math.log2(math.e) / math.sqrt(HEAD))
        gk_d = dil_k_norm_gain[l][None, :]

        proj = _in_proj(h2, norm_gain[l][None, :], jnp.swapaxes(w_in[l], 0, 1))
        q_f, k_f, v_a = _mla_prep(proj, wq_r, wkv_r, q_a_norm_gain[l][None, :],
                                  kv_a_norm_gain[l][None, :], gq_full, gk_full, cos_m, sin_m, seq=seq)
        o_a = _mla_attn(q_f, k_f, v_a, batch=batch, seq=seq)
        o_b = _dil_attn(proj, cos_d, sin_d, gq_d, gk_d, batch=batch, seq=seq)
        h2 = _out_proj(o_a, o_b, proj, h2, mla_out_norm_gain[l][None, :],
                       dil_out_norm_gain[l][None, :], w_out[l])
    return h2.reshape(batch, seq, D_MODEL)
```

```python
import functools
import math

import jax
import jax.numpy as jnp
import numpy as np
from jax import lax
from jax.experimental import pallas as pl
from jax.experimental.pallas import tpu as pltpu

EPS = 1e-6
ROPE_THETA = 10000.0
NEG = -1e30

D_MODEL = 2048
N_HEADS = 8
HEAD = 128
ROPE = 64
Q_RANK = 768
KV_RANK = 512
WIDTH = N_HEADS * HEAD
MLA_QK = HEAD + ROPE
QBLK = 128
N_BACK = 128
DILATIONS = (1, 4, 16)

C_CKV = 0
C_KR = C_CKV + KV_RANK
C_CQ = C_KR + 2 * HEAD
C_QB_LO = C_CQ + Q_RANK
C_GATE = C_QB_LO + 4 * HEAD
C_QB_HI = C_GATE + 2 * WIDTH
C_KB = C_QB_HI + 4 * HEAD
C_VB = C_KB + WIDTH
PROJ_COLS = C_VB + WIDTH
LAT_COLS = C_QB_LO
assert C_GATE % (2 * WIDTH) == 0 and PROJ_COLS == 6656

V7X_VMEM_BYTES = 64 * 1024 * 1024
VMEM_LIMIT = V7X_VMEM_BYTES - 8 * 1024 * 1024
IN_PROJ_VMEM_LIMIT = V7X_VMEM_BYTES - 4 * 1024 * 1024


def _rms_scale(x, n):
    return lax.rsqrt(jnp.sum(x * x, axis=-1, keepdims=True) * (1.0 / n) + EPS)


def _rms_scale_mxu(x, n):
    ones = jnp.ones((x.shape[-1], x.shape[-1]), jnp.bfloat16)
    ss = jnp.dot((x * x).astype(jnp.bfloat16), ones, preferred_element_type=jnp.float32)
    return lax.rsqrt(ss * (1.0 / n) + EPS)


_O_CQ, _O_CKV, _O_KR = 0, Q_RANK, Q_RANK + KV_RANK
_O_GA = _O_KR + ROPE
_O_QB = _O_GA + WIDTH
_O_KB, _O_VB, _O_GB = _O_QB + WIDTH, _O_QB + 2 * WIDTH, _O_QB + 3 * WIDTH
_IN_COLS = _O_GB + WIDTH


W_TILE = 512
_W_TILE_SRC = (_O_CKV, _O_CQ, _O_CQ + 256, _O_QB, _O_GA, _O_GA + 512, _O_GB, _O_GB + 512,
               _O_QB + 512, _O_KB, _O_KB + 512, _O_VB, _O_VB + 512)
assert len(_W_TILE_SRC) * W_TILE == PROJ_COLS


X_CHUNK = 512
N_CHUNKS = 8


def _in_proj_kernel(src_ref, x_ref, g_ref, w_ref, kr_ref, o_ref, hn0_ref, hn1_ref, *, row_chunk, n_tiles):
    del src_ref
    i = pl.program_id(0)
    j = pl.program_id(1)

    def norm(hn_ref):
        base = pl.multiple_of(j * X_CHUNK, X_CHUNK)

        def body(c, carry):
            r0 = pl.multiple_of(c * row_chunk, row_chunk)
            x = x_ref[pl.ds(r0, row_chunk), :]
            hn = x * _rms_scale(x, D_MODEL) * g_ref[...]
            hn_ref[pl.ds(base + r0, row_chunk), :] = hn.astype(hn_ref.dtype)
            return carry
        lax.fori_loop(0, X_CHUNK // row_chunk, body, 0, unroll=4)

    def weight_tile():
        generic = w_ref[...].astype(jnp.bfloat16)
        half = ROPE // 2
        kr = kr_ref[...].astype(jnp.bfloat16)
        z = jnp.zeros((half, D_MODEL), jnp.bfloat16)
        special = jnp.concatenate([kr[:half], z, kr[half:], z, jnp.zeros((HEAD, D_MODEL), jnp.bfloat16),
                                   generic[:2 * HEAD, :]], axis=0)
        return jnp.where(j == 1, special, generic)

    def project(hn_ref):
        w = weight_tile()
        quarter = o_ref.shape[0] // 4
        for r in range(0, o_ref.shape[0], quarter):
            o_ref[r:r + quarter, :] = lax.dot_general(
                hn_ref[r:r + quarter, :], w, (((1,), (1,)), ((), ())),
                preferred_element_type=jnp.float32).astype(o_ref.dtype)

    do_norm = (i < n_tiles) & (j < N_CHUNKS)
    even = (i % 2) == 0

    @pl.when((i == 0) & do_norm)
    def _():
        norm(hn0_ref)

    for is_even, hn_cur, hn_prev in ((True, hn0_ref, hn1_ref), (False, hn1_ref, hn0_ref)):
        parity = even if is_even else jnp.logical_not(even)

        @pl.when((i > 0) & parity & do_norm)
        def _(hn_cur=hn_cur, hn_prev=hn_prev):
            project(hn_prev)
            norm(hn_cur)

        @pl.when((i > 0) & parity & jnp.logical_not(do_norm))
        def _(hn_prev=hn_prev):
            project(hn_prev)


def _in_proj(x2, gain, wt):
    m = x2.shape[0]
    tm = X_CHUNK * N_CHUNKS
    n_tiles = m // tm
    n_col = PROJ_COLS // W_TILE
    assert N_CHUNKS <= n_col
    assert all(s % ROPE == 0 for s in _W_TILE_SRC)
    src = jnp.asarray([s // ROPE for s in _W_TILE_SRC], jnp.int32)

    def x_index(i, j, src):
        return (jnp.where(i < n_tiles, i * N_CHUNKS + jnp.minimum(j, N_CHUNKS - 1), n_tiles * N_CHUNKS - 1), 0)

    def out_index(i, j, src):
        return (jnp.maximum(i - 1, 0), jnp.where(i == 0, 0, j))

    return pl.pallas_call(
        functools.partial(_in_proj_kernel, row_chunk=64, n_tiles=n_tiles),
        grid_spec=pltpu.PrefetchScalarGridSpec(
            num_scalar_prefetch=1,
            grid=(n_tiles + 1, n_col),
            in_specs=[
                pl.BlockSpec((X_CHUNK, D_MODEL), x_index),
                pl.BlockSpec((1, D_MODEL), lambda i, j, src: (0, 0)),
                pl.BlockSpec((pl.Element(W_TILE), pl.Element(D_MODEL)), lambda i, j, src: (src[j] * ROPE, 0)),
                pl.BlockSpec((pl.Element(ROPE), pl.Element(D_MODEL)), lambda i, j, src: (_O_KR, 0)),
            ],
            out_specs=pl.BlockSpec((tm, W_TILE), out_index),
            scratch_shapes=[pltpu.VMEM((tm, D_MODEL), jnp.bfloat16),
                            pltpu.VMEM((tm, D_MODEL), jnp.bfloat16)],
        ),
        out_shape=jax.ShapeDtypeStruct((m, PROJ_COLS), jnp.bfloat16),
        compiler_params=pltpu.CompilerParams(
            dimension_semantics=("arbitrary", "arbitrary"),
            vmem_limit_bytes=IN_PROJ_VMEM_LIMIT),
        name="in_proj",
    )(src, x2, gain, wt, wt)


def _rope_pad(x, cos, sin):
    return x * cos + pltpu.roll(x, 64, 1) * sin


def _mla_prep_kernel(lat_ref, wq_ref, wkv_ref, gqa_ref, gkva_ref, gq_ref, gk_ref,
                     cos_ref, sin_ref, q_ref, k_ref, v_ref):
    cos = cos_ref[...]
    sin = sin_ref[...]
    gq = gq_ref[...]
    gk = gk_ref[...]

    c_kv = lat_ref[:, 0:KV_RANK].astype(jnp.float32)
    cn = (c_kv * _rms_scale(c_kv, KV_RANK) * gkva_ref[...]).astype(jnp.bfloat16)
    kv = jnp.dot(cn, wkv_ref[...], preferred_element_type=jnp.float32)
    ones = jnp.ones((kv.shape[0], HEAD), v_ref.dtype)
    for h in range(N_HEADS):
        v_ref[:, 2 * h * HEAD:(2 * h + 1) * HEAD] = kv[:, WIDTH + h * HEAD:WIDTH + (h + 1) * HEAD].astype(v_ref.dtype)
        v_ref[:, (2 * h + 1) * HEAD:(2 * h + 2) * HEAD] = ones

    k_r = lat_ref[:, KV_RANK:KV_RANK + HEAD].astype(jnp.float32)
    k_r = k_r * _rms_scale(k_r, ROPE) * gk[:, HEAD:]
    k_rope = _rope_pad(k_r, cos, sin).astype(k_ref.dtype)
    for h in range(N_HEADS):
        kn = kv[:, h * HEAD:(h + 1) * HEAD]
        kn = kn * _rms_scale(kn, HEAD) * gk[:, :HEAD]
        k_ref[:, 2 * h * HEAD:(2 * h + 1) * HEAD] = kn.astype(k_ref.dtype)
        k_ref[:, (2 * h + 1) * HEAD:(2 * h + 2) * HEAD] = k_rope

    c_q = lat_ref[:, 2 * HEAD + KV_RANK:].astype(jnp.float32)
    cqn = (c_q * _rms_scale(c_q, Q_RANK) * gqa_ref[...]).astype(jnp.bfloat16)
    q = jnp.dot(cqn, wq_ref[...], preferred_element_type=jnp.float32)
    for h in range(N_HEADS):
        qn = q[:, 2 * h * HEAD:(2 * h + 1) * HEAD]
        qn = qn * _rms_scale(qn, HEAD) * gq[:, :HEAD]
        q_ref[:, 2 * h * HEAD:(2 * h + 1) * HEAD] = qn.astype(q_ref.dtype)
        qr = q[:, (2 * h + 1) * HEAD:(2 * h + 2) * HEAD]
        qr = qr * _rms_scale(qr, ROPE) * gq[:, HEAD:]
        q_ref[:, (2 * h + 1) * HEAD:(2 * h + 2) * HEAD] = _rope_pad(qr, cos, sin).astype(q_ref.dtype)


def _mla_prep(proj, wq, wkv, gqa, gkva, gq, gk, cos, sin, *, seq, tm=256):
    m = proj.shape[0]
    nseq = seq // tm
    const = lambda i: (0, 0)
    return pl.pallas_call(
        _mla_prep_kernel,
        grid=(m // tm,),
        in_specs=[
            pl.BlockSpec((tm, LAT_COLS), lambda i: (i, 0)),
            pl.BlockSpec(wq.shape, const),
            pl.BlockSpec(wkv.shape, const),
            pl.BlockSpec(gqa.shape, const),
            pl.BlockSpec(gkva.shape, const),
            pl.BlockSpec(gq.shape, const),
            pl.BlockSpec(gk.shape, const),
            pl.BlockSpec((tm, HEAD), lambda i: (i % nseq, 0)),
            pl.BlockSpec((tm, HEAD), lambda i: (i % nseq, 0)),
        ],
        out_specs=[
            pl.BlockSpec((tm, 2 * WIDTH), lambda i: (i, 0)),
            pl.BlockSpec((tm, 2 * WIDTH), lambda i: (i, 0)),
            pl.BlockSpec((tm, 2 * WIDTH), lambda i: (i, 0)),
        ],
        out_shape=[
            jax.ShapeDtypeStruct((m, 2 * WIDTH), jnp.bfloat16),
            jax.ShapeDtypeStruct((m, 2 * WIDTH), jnp.bfloat16),
            jax.ShapeDtypeStruct((m, 2 * WIDTH), jnp.bfloat16),
        ],
        compiler_params=pltpu.CompilerParams(
            dimension_semantics=("parallel",), vmem_limit_bytes=VMEM_LIMIT),
        name="mla_prep",
    )(proj, wq, wkv, gqa, gkva, gq, gk, cos, sin)


def _mla_attn_kernel(q_ref, k_ref, v_ref, o_ref, m_ref, acc_ref, sa_ref, sb_ref, *, tq, tk, heads):
    assert tq == tk
    i = pl.program_id(2)
    m_ref[...] = jnp.full(m_ref.shape, NEG, jnp.float32)
    acc_ref[...] = jnp.zeros(acc_ref.shape, jnp.float32)

    def scores(j, s_ref):
        k0 = pl.multiple_of(j * tk, tk)
        for h in range(heads):
            cols = slice(2 * h * HEAD, 2 * (h + 1) * HEAD)
            s_ref[h] = lax.dot_general(q_ref[:, cols], k_ref[pl.ds(k0, tk), cols],
                                       (((1,), (1,)), ((), ())),
                                       preferred_element_type=jnp.float32)

    def update(h, rows, s, v):
        m_prev = m_ref[h, rows, :]
        m_new = jnp.maximum(m_prev, jnp.max(s, axis=-1, keepdims=True))
        alpha = jnp.exp2(m_prev - m_new)
        p = jnp.exp2(s - jnp.tile(m_new, (1, s.shape[1] // HEAD)))
        pv = jnp.dot(p.astype(v.dtype), v, preferred_element_type=jnp.float32)
        acc_ref[h, rows, :] = jnp.tile(alpha, (1, 2)) * acc_ref[h, rows, :] + pv
        m_ref[h, rows, :] = m_new

    def softmax_pv(j, s_ref, masked):
        k0 = pl.multiple_of(j * tk, tk)
        hq = tq // 2
        for h in range(heads):
            cols = slice(2 * h * HEAD, 2 * (h + 1) * HEAD)
            if not masked:
                update(h, slice(None), s_ref[h], v_ref[pl.ds(k0, tk), cols])
                continue
            row = lax.broadcasted_iota(jnp.int32, (hq, hq), 0)
            col = lax.broadcasted_iota(jnp.int32, (hq, hq), 1)
            tri = col <= row
            s0 = jnp.where(tri, s_ref[h, 0:hq, 0:hq], NEG)
            update(h, slice(0, hq), s0, v_ref[pl.ds(k0, hq), cols])
            s1 = jnp.concatenate([s_ref[h, hq:, 0:hq], jnp.where(tri, s_ref[h, hq:, hq:], NEG)], axis=1)
            update(h, slice(hq, tq), s1, v_ref[pl.ds(k0, tk), cols])

    scores(0, sa_ref)

    def body(t, carry):
        j = 2 * t
        scores(j + 1, sb_ref)
        softmax_pv(j, sa_ref, False)
        scores(j + 2, sa_ref)
        softmax_pv(j + 1, sb_ref, False)
        return carry
    lax.fori_loop(0, i // 2, body, 0)

    @pl.when(i % 2 == 0)
    def _():
        softmax_pv(i, sa_ref, True)

    @pl.when(i % 2 == 1)
    def _():
        scores(i, sb_ref)
        softmax_pv(i - 1, sa_ref, False)
        softmax_pv(i, sb_ref, True)

    for h in range(heads):
        acc = acc_ref[h]
        o_ref[:, h * HEAD:(h + 1) * HEAD] = (acc[:, :HEAD] / acc[:, HEAD:]).astype(o_ref.dtype)


def _mla_attn(q, k, v, *, batch, seq, tq=512, tk=512, heads=4):
    nq = seq // tq
    w = 2 * HEAD * heads
    return pl.pallas_call(
        functools.partial(_mla_attn_kernel, tq=tq, tk=tk, heads=heads),
        grid=(batch, N_HEADS // heads, nq),
        in_specs=[
            pl.BlockSpec((tq, w), lambda b, h, i: (b * nq + i, h)),
            pl.BlockSpec((seq, w), lambda b, h, i: (b, h)),
            pl.BlockSpec((seq, w), lambda b, h, i: (b, h)),
        ],
        out_specs=pl.BlockSpec((tq, HEAD * heads), lambda b, h, i: (b * nq + i, h)),
        out_shape=jax.ShapeDtypeStruct((batch * seq, WIDTH), jnp.bfloat16),
        scratch_shapes=[
            pltpu.VMEM((heads, tq, HEAD), jnp.float32),
            pltpu.VMEM((heads, tq, 2 * HEAD), jnp.float32),
            pltpu.VMEM((heads, tq, tk), jnp.float32),
            pltpu.VMEM((heads, tq, tk), jnp.float32),
        ],
        compiler_params=pltpu.CompilerParams(
            dimension_semantics=("parallel", "parallel", "arbitrary"),
            vmem_limit_bytes=VMEM_LIMIT),
        name="mla_attn",
    )(q, k, v)


def _dil_attn_kernel(q_ref, k_ref, v_ref, cos_ref, sin_ref, gq_ref, gk_ref, o_ref,
                     qd_ref, kd_ref, vd_ref, on_ref, ls_ref, bias_ref, st_ref, *, seq, unroll):
    chunk = 512
    npat = len(DILATIONS)
    assert DILATIONS == (1, 4, 16)
    nat = [on_ref.at[a] for a in range(3)]
    by4 = [ls_ref.at[a] for a in range(3)]
    dst = [qd_ref, kd_ref, vd_ref]
    lead = [0, QBLK, QBLK]

    @pl.when((pl.program_id(0) == 0) & (pl.program_id(1) == 0))
    def _():
        qi = lax.broadcasted_iota(jnp.int32, (QBLK, 2 * QBLK), 0)
        kj = lax.broadcasted_iota(jnp.int32, (QBLK, 2 * QBLK), 1)
        band = (kj >= qi) & (kj <= qi + N_BACK)
        bias_ref[0] = jnp.where(band, 0.0, NEG)
        bias_ref[1] = jnp.where(band & (kj >= QBLK), 0.0, NEG)
        for p in range(npat):
            vd_ref[p, QBLK:, HEAD:] = jnp.ones((seq, HEAD), vd_ref.dtype)
            kd_ref[p, 0:QBLK, :] = jnp.zeros((QBLK, HEAD), kd_ref.dtype)
            vd_ref[p, 0:QBLK, :] = jnp.zeros((QBLK, 2 * HEAD), vd_ref.dtype)

    def prep(c, carry):
        r0 = pl.multiple_of(c * chunk, chunk)
        cos = cos_ref[pl.ds(r0, chunk), :]
        sin = sin_ref[pl.ds(r0, chunk), :]
        q = q_ref[pl.ds(r0, chunk), :].astype(jnp.float32)
        q = q * _rms_scale_mxu(q, HEAD) * gq_ref[...]
        q = q * cos + pltpu.roll(q, 64, 1) * sin
        k = k_ref[pl.ds(r0, chunk), :].astype(jnp.float32)
        k = k * _rms_scale_mxu(k, HEAD) * gk_ref[...]
        k = k * cos + pltpu.roll(k, 64, 1) * sin
        v = v_ref[pl.ds(r0, chunk), :]
        nat[0][pl.ds(r0, chunk), :] = q
        nat[1][pl.ds(r0, chunk), :] = k
        nat[2][pl.ds(r0, chunk), :] = v.astype(jnp.float32)
        qd_ref[0, pl.ds(r0, chunk), :] = q.astype(qd_ref.dtype)
        kd_ref[0, pl.ds(QBLK + r0, chunk), :] = k.astype(kd_ref.dtype)
        vd_ref[0, pl.ds(QBLK + r0, chunk), 0:HEAD] = v
        return carry
    lax.fori_loop(0, seq // chunk, prep, 0, unroll=2)

    sub4, sub16 = seq // 4, seq // 16
    for a in range(3):
        for r4 in range(4):
            x = nat[a][pl.ds(r4, sub4, stride=4), :]
            by4[a][r4 * sub4:(r4 + 1) * sub4, :] = x
            dst[a][1, lead[a] + r4 * sub4:lead[a] + (r4 + 1) * sub4, 0:HEAD] = x.astype(dst[a].dtype)
    for a in range(3):
        for r4 in range(4):
            for j in range(4):
                r16 = r4 + 4 * j
                x = by4[a][pl.ds(r4 * sub4 + j, sub16, stride=4), :]
                dst[a][2, lead[a] + r16 * sub16:lead[a] + (r16 + 1) * sub16, 0:HEAD] = x.astype(dst[a].dtype)

    for p, d in enumerate(DILATIONS):
        nb = seq // d // QBLK
        shift = int(math.log2(nb))

        def block(g, carry, p=p, d=d, nb=nb, shift=shift):
            g0 = pl.multiple_of(g * QBLK, QBLK)
            q = qd_ref[p, pl.ds(g0, QBLK), :]
            k = kd_ref[p, pl.ds(g0, 2 * QBLK), :]
            v = vd_ref[p, pl.ds(g0, 2 * QBLK), :]
            n = g & (nb - 1)
            s = lax.dot_general(q, k, (((1,), (1,)), ((), ())), preferred_element_type=jnp.float32)
            s = s + bias_ref[jnp.where(n == 0, 1, 0)]
            m = jnp.max(s, axis=-1, keepdims=True)
            e = jnp.exp2(s - m)
            pv = jnp.dot(e.astype(v.dtype), v, preferred_element_type=jnp.float32)
            den = pv[:, HEAD:]
            o = pv[:, :HEAD] / den
            lse = m + jnp.log2(den)
            r = g >> shift
            if d == 16:
                rows = pl.ds((r & 3) * sub4 + (r >> 2) + n * (QBLK * 4), QBLK, stride=4)
                st_ref[0, rows, :] = o
                st_ref[1, rows, :] = lse
            else:
                rows = pl.ds(r + n * (QBLK * d), QBLK, stride=d) if d > 1 else pl.ds(g0, QBLK)
                on_ref[p, rows, :] = o
                ls_ref[p, rows, :] = lse
            return carry
        lax.fori_loop(0, seq // QBLK, block, 0, unroll=unroll)

    for a, out in enumerate((on_ref, ls_ref)):
        for r4 in range(4):
            out[2, pl.ds(r4, sub4, stride=4), :] = st_ref[a, r4 * sub4:(r4 + 1) * sub4, :]

    def mix(c, carry):
        r0 = pl.multiple_of(c * chunk, chunk)
        ls = [ls_ref[p, pl.ds(r0, chunk), :] for p in range(npat)]
        mx = functools.reduce(jnp.maximum, ls)
        w = [jnp.exp2(l - mx) for l in ls]
        num = sum(w[p] * on_ref[p, pl.ds(r0, chunk), :] for p in range(npat))
        o_ref[pl.ds(r0, chunk), :] = (num / sum(w)).astype(o_ref.dtype)
        return carry
    lax.fori_loop(0, seq // chunk, mix, 0, unroll=2)


def _dil_attn(proj, cos, sin, gq, gk, *, batch, seq):
    npat = len(DILATIONS)

    def qcol(h):
        return jnp.where(h < 4, C_QB_LO // HEAD + h, C_QB_HI // HEAD - 4 + h)

    const = lambda b, h: (0, 0)
    return pl.pallas_call(
        functools.partial(_dil_attn_kernel, seq=seq, unroll=seq // QBLK),
        grid=(batch, N_HEADS),
        in_specs=[
            pl.BlockSpec((seq, HEAD), lambda b, h: (b, qcol(h))),
            pl.BlockSpec((seq, HEAD), lambda b, h: (b, C_KB // HEAD + h)),
            pl.BlockSpec((seq, HEAD), lambda b, h: (b, C_VB // HEAD + h)),
            pl.BlockSpec((seq, HEAD), const),
            pl.BlockSpec((seq, HEAD), const),
            pl.BlockSpec((1, HEAD), const),
            pl.BlockSpec((1, HEAD), const),
        ],
        out_specs=pl.BlockSpec((seq, HEAD), lambda b, h: (b, h)),
        out_shape=jax.ShapeDtypeStruct((batch * seq, WIDTH), jnp.bfloat16),
        scratch_shapes=[
            pltpu.VMEM((npat, seq, HEAD), jnp.bfloat16),
            pltpu.VMEM((npat, seq + QBLK, HEAD), jnp.bfloat16),
            pltpu.VMEM((npat, seq + QBLK, 2 * HEAD), jnp.bfloat16),
            pltpu.VMEM((npat, seq, HEAD), jnp.float32),
            pltpu.VMEM((npat, seq, HEAD), jnp.float32),
            pltpu.VMEM((2, QBLK, 2 * QBLK), jnp.float32),
            pltpu.VMEM((2, seq, HEAD), jnp.float32),
        ],
        compiler_params=pltpu.CompilerParams(
            dimension_semantics=("arbitrary", "arbitrary"), vmem_limit_bytes=VMEM_LIMIT),
        name="dil_attn",
    )(proj, proj, proj, cos, sin, gq, gk)


def _out_proj_kernel(oa_ref, ob_ref, g_ref, x_ref, gcol_ref, w32_ref, o_ref, w_ref, *, nsplit):
    @pl.when(pl.program_id(0) == 0)
    def _():
        rows = 256

        def cast(c, carry):
            r0 = pl.multiple_of(c * rows, rows)
            w = w32_ref[pl.ds(r0, rows), :] * gcol_ref[pl.ds(r0, rows), :]
            w_ref[pl.ds(r0, rows), :] = w.astype(w_ref.dtype)
            return carry
        lax.fori_loop(0, w_ref.shape[0] // rows, cast, 0)

    def branch(o, gate):
        o = o.astype(jnp.float32)
        h = gate.astype(jnp.float32) * 0.5
        silu = h + h * jnp.tanh(h)
        return (o * _rms_scale(o, WIDTH) * silu).astype(jnp.bfloat16)

    step = o_ref.shape[0] // nsplit
    for r in range(0, o_ref.shape[0], step):
        rows = slice(r, r + step)
        ya = branch(oa_ref[rows, :], g_ref[rows, :WIDTH])
        yb = branch(ob_ref[rows, :], g_ref[rows, WIDTH:])
        y = jnp.concatenate([ya, yb], axis=-1)
        o_ref[rows, :] = x_ref[rows, :] + jnp.dot(y, w_ref[...], preferred_element_type=jnp.float32)


def _out_proj(o_a, o_b, proj, x2, gcol, w, *, tm=512):
    m = x2.shape[0]
    const = lambda i: (0, 0)
    return pl.pallas_call(
        functools.partial(_out_proj_kernel, nsplit=2),
        grid=(m // tm,),
        in_specs=[
            pl.BlockSpec((tm, WIDTH), lambda i: (i, 0)),
            pl.BlockSpec((tm, WIDTH), lambda i: (i, 0)),
            pl.BlockSpec((tm, 2 * WIDTH), lambda i: (i, C_GATE // (2 * WIDTH))),
            pl.BlockSpec((tm, D_MODEL), lambda i: (i, 0)),
            pl.BlockSpec((2 * WIDTH, 1), const, pipeline_mode=pl.Buffered(1)),
            pl.BlockSpec((2 * WIDTH, D_MODEL), const, pipeline_mode=pl.Buffered(1)),
        ],
        out_specs=pl.BlockSpec((tm, D_MODEL), lambda i: (i, 0)),
        out_shape=jax.ShapeDtypeStruct((m, D_MODEL), jnp.float32),
        scratch_shapes=[pltpu.VMEM((2 * WIDTH, D_MODEL), jnp.bfloat16)],
        compiler_params=pltpu.CompilerParams(
            dimension_semantics=("arbitrary",), vmem_limit_bytes=VMEM_LIMIT),
        name="out_proj",
    )(o_a, o_b, proj, x2, gcol, w)


def _pad_rope_cols(a):
    z = jnp.zeros(a.shape[:-1] + (ROPE // 2,), a.dtype)
    return jnp.concatenate([a[..., :ROPE // 2], z, a[..., ROPE // 2:], z], axis=-1)


def _rope_tables(seq, d, padded):
    inv = ROPE_THETA ** (-np.arange(0, d, 2, dtype=np.float64) / d)
    ang = np.arange(seq, dtype=np.float64)[:, None] * inv[None, :]
    cos, sin = np.cos(ang), np.sin(ang)
    cos2 = np.concatenate([cos, cos], axis=-1)
    sin2 = np.concatenate([-sin, sin], axis=-1)
    if padded:
        z = np.zeros((seq, d // 2))
        cos2 = np.concatenate([cos, z, cos, z], axis=-1)
        sin2 = np.concatenate([-sin, z, sin, z], axis=-1)
    return jnp.asarray(cos2, jnp.float32), jnp.asarray(sin2, jnp.float32)


def kernel(x, norm_gain, w_in, q_a_norm_gain, kv_a_norm_gain, w_uq, w_ukv, mla_q_norm_gain,
           mla_k_norm_gain, dil_q_norm_gain, dil_k_norm_gain, mla_out_norm_gain,
           dil_out_norm_gain, w_out):
    batch, seq, _ = x.shape
    depth = w_in.shape[0]
    bf = jnp.bfloat16
    cos_d, sin_d = _rope_tables(seq, HEAD, padded=False)
    cos_m, sin_m = _rope_tables(seq, ROPE, padded=True)
    h2 = x.reshape(batch * seq, D_MODEL)

    for l in range(depth):
        wq = w_uq[l].reshape(Q_RANK, N_HEADS, MLA_QK)
        wq_r = jnp.concatenate([wq[..., :HEAD], _pad_rope_cols(wq[..., HEAD:])], axis=-1)
        wq_r = wq_r.reshape(Q_RANK, N_HEADS * 2 * HEAD).astype(bf)
        wkv = w_ukv[l].reshape(KV_RANK, N_HEADS, 2 * HEAD)
        wkv_r = jnp.concatenate([wkv[..., :HEAD].reshape(KV_RANK, WIDTH),
                                 wkv[..., HEAD:].reshape(KV_RANK, WIDTH)], axis=-1).astype(bf)

        gq_m = mla_q_norm_gain[l]
        gk_m = mla_k_norm_gain[l]
        gq_full = jnp.concatenate([gq_m[:HEAD], _pad_rope_cols(gq_m[HEAD:])])[None, :]
        gq_full = gq_full * (math.log2(math.e) / math.sqrt(MLA_QK))
        gk_full = jnp.concatenate([gk_m[:HEAD], _pad_rope_cols(gk_m[HEAD:])])[None, :]
        gq_d = dil_q_norm_gain[l][None, :] * (math.log2(math.e) / math.sqrt(HEAD))
        gk_d = dil_k_norm_gain[l][None, :]

        proj = _in_proj(h2, norm_gain[l][None, :], jnp.swapaxes(w_in[l], 0, 1))
        q_f, k_f, v_a = _mla_prep(proj, wq_r, wkv_r, q_a_norm_gain[l][None, :],
                                  kv_a_norm_gain[l][None, :], gq_full, gk_full, cos_m, sin_m, seq=seq)
        o_a = _mla_attn(q_f, k_f, v_a, batch=batch, seq=seq)
        o_b = _dil_attn(proj, cos_d, sin_d, gq_d, gk_d, batch=batch, seq=seq)
        gcol = jnp.concatenate([mla_out_norm_gain[l], dil_out_norm_gain[l]])[:, None]
        h2 = _out_proj(o_a, o_b, proj, h2, gcol, w_out[l])
    return h2.reshape(batch, seq, D_MODEL)
```

```python
import functools
import math

import jax
import jax.numpy as jnp
import numpy as np
from jax import lax
from jax.experimental import pallas as pl
from jax.experimental.pallas import tpu as pltpu

EPS = 1e-6
ROPE_THETA = 10000.0
NEG = -1e30

D_MODEL = 2048
N_HEADS = 8
HEAD = 128
ROPE = 64
Q_RANK = 768
KV_RANK = 512
WIDTH = N_HEADS * HEAD
MLA_QK = HEAD + ROPE
QBLK = 128
N_BACK = 128
DILATIONS = (1, 4, 16)

C_CKV = 0
C_KR = C_CKV + KV_RANK
C_CQ = C_KR + 2 * HEAD
C_QB_LO = C_CQ + Q_RANK
C_GATE = C_QB_LO + 4 * HEAD
C_QB_HI = C_GATE + 2 * WIDTH
C_KB = C_QB_HI + 4 * HEAD
C_VB = C_KB + WIDTH
PROJ_COLS = C_VB + WIDTH
LAT_COLS = C_QB_LO
assert C_GATE % (2 * WIDTH) == 0 and PROJ_COLS == 6656

V7X_VMEM_BYTES = 64 * 1024 * 1024
VMEM_LIMIT = V7X_VMEM_BYTES - 8 * 1024 * 1024
IN_PROJ_VMEM_LIMIT = V7X_VMEM_BYTES - 4 * 1024 * 1024


def _rms_scale(x, n):
    return lax.rsqrt(jnp.sum(x * x, axis=-1, keepdims=True) * (1.0 / n) + EPS)


def _rms_scale_mxu(x, n):
    ones = jnp.ones((x.shape[-1], x.shape[-1]), jnp.bfloat16)
    ss = jnp.dot((x * x).astype(jnp.bfloat16), ones, preferred_element_type=jnp.float32)
    return lax.rsqrt(ss * (1.0 / n) + EPS)


_O_CQ, _O_CKV, _O_KR = 0, Q_RANK, Q_RANK + KV_RANK
_O_GA = _O_KR + ROPE
_O_QB = _O_GA + WIDTH
_O_KB, _O_VB, _O_GB = _O_QB + WIDTH, _O_QB + 2 * WIDTH, _O_QB + 3 * WIDTH
_IN_COLS = _O_GB + WIDTH


W_TILE = 512
_W_TILE_SRC = (_O_CKV, _O_CQ, _O_CQ + 256, _O_QB, _O_GA, _O_GA + 512, _O_GB, _O_GB + 512,
               _O_QB + 512, _O_KB, _O_KB + 512, _O_VB, _O_VB + 512)
assert len(_W_TILE_SRC) * W_TILE == PROJ_COLS


X_CHUNK = 512
N_CHUNKS = 8


def _in_proj_kernel(src_ref, x_ref, g_ref, w_ref, kr_ref, o_ref, hn0_ref, hn1_ref, *, row_chunk, n_tiles):
    del src_ref
    i = pl.program_id(0)
    j = pl.program_id(1)

    def norm(hn_ref):
        base = pl.multiple_of(j * X_CHUNK, X_CHUNK)

        def body(c, carry):
            r0 = pl.multiple_of(c * row_chunk, row_chunk)
            x = x_ref[pl.ds(r0, row_chunk), :]
            hn = x * _rms_scale(x, D_MODEL) * g_ref[...]
            hn_ref[pl.ds(base + r0, row_chunk), :] = hn.astype(hn_ref.dtype)
            return carry
        lax.fori_loop(0, X_CHUNK // row_chunk, body, 0, unroll=4)

    def weight_tile():
        generic = w_ref[...].astype(jnp.bfloat16)
        half = ROPE // 2
        kr = kr_ref[...].astype(jnp.bfloat16)
        z = jnp.zeros((half, D_MODEL), jnp.bfloat16)
        special = jnp.concatenate([kr[:half], z, kr[half:], z, jnp.zeros((HEAD, D_MODEL), jnp.bfloat16),
                                   generic[:2 * HEAD, :]], axis=0)
        return jnp.where(j == 1, special, generic)

    def project(hn_ref):
        w = weight_tile()
        quarter = o_ref.shape[0] // 4
        for r in range(0, o_ref.shape[0], quarter):
            o_ref[r:r + quarter, :] = lax.dot_general(
                hn_ref[r:r + quarter, :], w, (((1,), (1,)), ((), ())),
                preferred_element_type=jnp.float32).astype(o_ref.dtype)

    do_norm = (i < n_tiles) & (j < N_CHUNKS)
    even = (i % 2) == 0

    @pl.when((i == 0) & do_norm)
    def _():
        norm(hn0_ref)

    for is_even, hn_cur, hn_prev in ((True, hn0_ref, hn1_ref), (False, hn1_ref, hn0_ref)):
        parity = even if is_even else jnp.logical_not(even)

        @pl.when((i > 0) & parity & do_norm)
        def _(hn_cur=hn_cur, hn_prev=hn_prev):
            project(hn_prev)
            norm(hn_cur)

        @pl.when((i > 0) & parity & jnp.logical_not(do_norm))
        def _(hn_prev=hn_prev):
            project(hn_prev)


def _in_proj(x2, gain, wt):
    m = x2.shape[0]
    tm = X_CHUNK * N_CHUNKS
    n_tiles = m // tm
    n_col = PROJ_COLS // W_TILE
    assert N_CHUNKS <= n_col
    assert all(s % ROPE == 0 for s in _W_TILE_SRC)
    src = jnp.asarray([s // ROPE for s in _W_TILE_SRC], jnp.int32)

    def x_index(i, j, src):
        return (jnp.where(i < n_tiles, i * N_CHUNKS + jnp.minimum(j, N_CHUNKS - 1), n_tiles * N_CHUNKS - 1), 0)

    def out_index(i, j, src):
        return (jnp.maximum(i - 1, 0), jnp.where(i == 0, 0, j))

    def w_index(i, j, src):
        return (src[jnp.where(i == 0, 0, j)] * ROPE, 0)

    return pl.pallas_call(
        functools.partial(_in_proj_kernel, row_chunk=64, n_tiles=n_tiles),
        grid_spec=pltpu.PrefetchScalarGridSpec(
            num_scalar_prefetch=1,
            grid=(n_tiles + 1, n_col),
            in_specs=[
                pl.BlockSpec((X_CHUNK, D_MODEL), x_index),
                pl.BlockSpec((1, D_MODEL), lambda i, j, src: (0, 0)),
                pl.BlockSpec((pl.Element(W_TILE), pl.Element(D_MODEL)), w_index),
                pl.BlockSpec((pl.Element(ROPE), pl.Element(D_MODEL)), lambda i, j, src: (_O_KR, 0)),
            ],
            out_specs=pl.BlockSpec((tm, W_TILE), out_index),
            scratch_shapes=[pltpu.VMEM((tm, D_MODEL), jnp.bfloat16),
                            pltpu.VMEM((tm, D_MODEL), jnp.bfloat16)],
        ),
        out_shape=jax.ShapeDtypeStruct((m, PROJ_COLS), jnp.bfloat16),
        compiler_params=pltpu.CompilerParams(
            dimension_semantics=("arbitrary", "arbitrary"),
            vmem_limit_bytes=IN_PROJ_VMEM_LIMIT),
        name="in_proj",
    )(src, x2, gain, wt, wt)


def _rope_pad(x, cos, sin):
    return x * cos + pltpu.roll(x, 64, 1) * sin


def _mla_prep_kernel(lat_ref, wq_ref, wkv_ref, gqa_ref, gkva_ref, gq_ref, gk_ref,
                     cos_ref, sin_ref, q_ref, k_ref, v_ref):
    cos = cos_ref[...]
    sin = sin_ref[...]
    gq = gq_ref[...]
    gk = gk_ref[...]

    c_kv = lat_ref[:, 0:KV_RANK].astype(jnp.float32)
    cn = (c_kv * _rms_scale(c_kv, KV_RANK) * gkva_ref[...]).astype(jnp.bfloat16)
    kv = jnp.dot(cn, wkv_ref[...], preferred_element_type=jnp.float32)
    ones = jnp.ones((kv.shape[0], HEAD), v_ref.dtype)
    for h in range(N_HEADS):
        v_ref[:, 2 * h * HEAD:(2 * h + 1) * HEAD] = kv[:, WIDTH + h * HEAD:WIDTH + (h + 1) * HEAD].astype(v_ref.dtype)
        v_ref[:, (2 * h + 1) * HEAD:(2 * h + 2) * HEAD] = ones

    k_r = lat_ref[:, KV_RANK:KV_RANK + HEAD].astype(jnp.float32)
    k_r = k_r * _rms_scale(k_r, ROPE) * gk[:, HEAD:]
    k_rope = _rope_pad(k_r, cos, sin).astype(k_ref.dtype)
    for h in range(N_HEADS):
        kn = kv[:, h * HEAD:(h + 1) * HEAD]
        kn = kn * _rms_scale(kn, HEAD) * gk[:, :HEAD]
        k_ref[:, 2 * h * HEAD:(2 * h + 1) * HEAD] = kn.astype(k_ref.dtype)
        k_ref[:, (2 * h + 1) * HEAD:(2 * h + 2) * HEAD] = k_rope

    c_q = lat_ref[:, 2 * HEAD + KV_RANK:].astype(jnp.float32)
    cqn = (c_q * _rms_scale(c_q, Q_RANK) * gqa_ref[...]).astype(jnp.bfloat16)
    q = jnp.dot(cqn, wq_ref[...], preferred_element_type=jnp.float32)
    for h in range(N_HEADS):
        qn = q[:, 2 * h * HEAD:(2 * h + 1) * HEAD]
        qn = qn * _rms_scale(qn, HEAD) * gq[:, :HEAD]
        q_ref[:, 2 * h * HEAD:(2 * h + 1) * HEAD] = qn.astype(q_ref.dtype)
        qr = q[:, (2 * h + 1) * HEAD:(2 * h + 2) * HEAD]
        qr = qr * _rms_scale(qr, ROPE) * gq[:, HEAD:]
        q_ref[:, (2 * h + 1) * HEAD:(2 * h + 2) * HEAD] = _rope_pad(qr, cos, sin).astype(q_ref.dtype)


def _mla_prep(proj, wq, wkv, gqa, gkva, gq, gk, cos, sin, *, seq, tm=256):
    m = proj.shape[0]
    nseq = seq // tm
    const = lambda i: (0, 0)
    return pl.pallas_call(
        _mla_prep_kernel,
        grid=(m // tm,),
        in_specs=[
            pl.BlockSpec((tm, LAT_COLS), lambda i: (i, 0)),
            pl.BlockSpec(wq.shape, const),
            pl.BlockSpec(wkv.shape, const),
            pl.BlockSpec(gqa.shape, const),
            pl.BlockSpec(gkva.shape, const),
            pl.BlockSpec(gq.shape, const),
            pl.BlockSpec(gk.shape, const),
            pl.BlockSpec((tm, HEAD), lambda i: (i % nseq, 0)),
            pl.BlockSpec((tm, HEAD), lambda i: (i % nseq, 0)),
        ],
        out_specs=[
            pl.BlockSpec((tm, 2 * WIDTH), lambda i: (i, 0)),
            pl.BlockSpec((tm, 2 * WIDTH), lambda i: (i, 0)),
            pl.BlockSpec((tm, 2 * WIDTH), lambda i: (i, 0)),
        ],
        out_shape=[
            jax.ShapeDtypeStruct((m, 2 * WIDTH), jnp.bfloat16),
            jax.ShapeDtypeStruct((m, 2 * WIDTH), jnp.bfloat16),
            jax.ShapeDtypeStruct((m, 2 * WIDTH), jnp.bfloat16),
        ],
        compiler_params=pltpu.CompilerParams(
            dimension_semantics=("parallel",), vmem_limit_bytes=VMEM_LIMIT),
        name="mla_prep",
    )(proj, wq, wkv, gqa, gkva, gq, gk, cos, sin)


def _mla_attn_kernel(q_ref, k_ref, v_ref, o_ref, m_ref, acc_ref, sa_ref, sb_ref, *, tq, tk, heads):
    assert tq == tk
    i = pl.program_id(2)
    m_ref[...] = jnp.full(m_ref.shape, NEG, jnp.float32)
    acc_ref[...] = jnp.zeros(acc_ref.shape, jnp.float32)

    def scores(j, s_ref):
        k0 = pl.multiple_of(j * tk, tk)
        for h in range(heads):
            cols = slice(2 * h * HEAD, 2 * (h + 1) * HEAD)
            s_ref[h] = lax.dot_general(q_ref[:, cols], k_ref[pl.ds(k0, tk), cols],
                                       (((1,), (1,)), ((), ())),
                                       preferred_element_type=jnp.float32)

    def update(h, rows, s, v):
        m_prev = m_ref[h, rows, :]
        m_new = jnp.maximum(m_prev, jnp.max(s, axis=-1, keepdims=True))
        alpha = jnp.exp2(m_prev - m_new)
        p = jnp.exp2(s - jnp.tile(m_new, (1, s.shape[1] // HEAD)))
        pv = jnp.dot(p.astype(v.dtype), v, preferred_element_type=jnp.float32)
        acc_ref[h, rows, :] = jnp.tile(alpha, (1, 2)) * acc_ref[h, rows, :] + pv
        m_ref[h, rows, :] = m_new

    def softmax_pv(j, s_ref, masked):
        k0 = pl.multiple_of(j * tk, tk)
        hq = tq // 2
        for h in range(heads):
            cols = slice(2 * h * HEAD, 2 * (h + 1) * HEAD)
            if not masked:
                update(h, slice(None), s_ref[h], v_ref[pl.ds(k0, tk), cols])
                continue
            row = lax.broadcasted_iota(jnp.int32, (hq, hq), 0)
            col = lax.broadcasted_iota(jnp.int32, (hq, hq), 1)
            tri = col <= row
            s0 = jnp.where(tri, s_ref[h, 0:hq, 0:hq], NEG)
            update(h, slice(0, hq), s0, v_ref[pl.ds(k0, hq), cols])
            s1 = jnp.concatenate([s_ref[h, hq:, 0:hq], jnp.where(tri, s_ref[h, hq:, hq:], NEG)], axis=1)
            update(h, slice(hq, tq), s1, v_ref[pl.ds(k0, tk), cols])

    scores(0, sa_ref)

    def body(t, carry):
        j = 2 * t
        scores(j + 1, sb_ref)
        softmax_pv(j, sa_ref, False)
        scores(j + 2, sa_ref)
        softmax_pv(j + 1, sb_ref, False)
        return carry
    lax.fori_loop(0, i // 2, body, 0)

    @pl.when(i % 2 == 0)
    def _():
        softmax_pv(i, sa_ref, True)

    @pl.when(i % 2 == 1)
    def _():
        scores(i, sb_ref)
        softmax_pv(i - 1, sa_ref, False)
        softmax_pv(i, sb_ref, True)

    for h in range(heads):
        acc = acc_ref[h]
        o_ref[:, h * HEAD:(h + 1) * HEAD] = (acc[:, :HEAD] / acc[:, HEAD:]).astype(o_ref.dtype)


def _mla_attn(q, k, v, *, batch, seq, tq=512, tk=512, heads=4):
    nq = seq // tq
    w = 2 * HEAD * heads
    return pl.pallas_call(
        functools.partial(_mla_attn_kernel, tq=tq, tk=tk, heads=heads),
        grid=(batch, N_HEADS // heads, nq),
        in_specs=[
            pl.BlockSpec((tq, w), lambda b, h, i: (b * nq + i, h)),
            pl.BlockSpec((seq, w), lambda b, h, i: (b, h)),
            pl.BlockSpec((seq, w), lambda b, h, i: (b, h)),
        ],
        out_specs=pl.BlockSpec((tq, HEAD * heads), lambda b, h, i: (b * nq + i, h)),
        out_shape=jax.ShapeDtypeStruct((batch * seq, WIDTH), jnp.bfloat16),
        scratch_shapes=[
            pltpu.VMEM((heads, tq, HEAD), jnp.float32),
            pltpu.VMEM((heads, tq, 2 * HEAD), jnp.float32),
            pltpu.VMEM((heads, tq, tk), jnp.float32),
            pltpu.VMEM((heads, tq, tk), jnp.float32),
        ],
        compiler_params=pltpu.CompilerParams(
            dimension_semantics=("parallel", "parallel", "arbitrary"),
            vmem_limit_bytes=VMEM_LIMIT),
        name="mla_attn",
    )(q, k, v)


def _dil_attn_kernel(q_ref, k_ref, v_ref, cos_ref, sin_ref, gq_ref, gk_ref, o_ref,
                     qd_ref, kd_ref, vd_ref, on_ref, ls_ref, bias_ref, st_ref, *, seq, unroll):
    chunk = 512
    npat = len(DILATIONS)
    assert DILATIONS == (1, 4, 16)
    nat = [on_ref.at[a] for a in range(3)]
    by4 = [ls_ref.at[a] for a in range(3)]
    dst = [qd_ref, kd_ref, vd_ref]
    lead = [0, QBLK, QBLK]

    @pl.when((pl.program_id(0) == 0) & (pl.program_id(1) == 0))
    def _():
        qi = lax.broadcasted_iota(jnp.int32, (QBLK, 2 * QBLK), 0)
        kj = lax.broadcasted_iota(jnp.int32, (QBLK, 2 * QBLK), 1)
        band = (kj >= qi) & (kj <= qi + N_BACK)
        bias_ref[0] = jnp.where(band, 0.0, NEG)
        bias_ref[1] = jnp.where(band & (kj >= QBLK), 0.0, NEG)
        for p in range(npat):
            vd_ref[p, QBLK:, HEAD:] = jnp.ones((seq, HEAD), vd_ref.dtype)
            kd_ref[p, 0:QBLK, :] = jnp.zeros((QBLK, HEAD), kd_ref.dtype)
            vd_ref[p, 0:QBLK, :] = jnp.zeros((QBLK, 2 * HEAD), vd_ref.dtype)

    def prep(c, carry):
        r0 = pl.multiple_of(c * chunk, chunk)
        cos = cos_ref[pl.ds(r0, chunk), :]
        sin = sin_ref[pl.ds(r0, chunk), :]
        q = q_ref[pl.ds(r0, chunk), :].astype(jnp.float32)
        q = q * _rms_scale_mxu(q, HEAD) * gq_ref[...]
        q = q * cos + pltpu.roll(q, 64, 1) * sin
        k = k_ref[pl.ds(r0, chunk), :].astype(jnp.float32)
        k = k * _rms_scale_mxu(k, HEAD) * gk_ref[...]
        k = k * cos + pltpu.roll(k, 64, 1) * sin
        v = v_ref[pl.ds(r0, chunk), :]
        nat[0][pl.ds(r0, chunk), :] = q
        nat[1][pl.ds(r0, chunk), :] = k
        nat[2][pl.ds(r0, chunk), :] = v.astype(jnp.float32)
        qd_ref[0, pl.ds(r0, chunk), :] = q.astype(qd_ref.dtype)
        kd_ref[0, pl.ds(QBLK + r0, chunk), :] = k.astype(kd_ref.dtype)
        vd_ref[0, pl.ds(QBLK + r0, chunk), 0:HEAD] = v
        return carry
    lax.fori_loop(0, seq // chunk, prep, 0, unroll=2)

    sub4, sub16 = seq // 4, seq // 16
    for a in range(3):
        for r4 in range(4):
            x = nat[a][pl.ds(r4, sub4, stride=4), :]
            by4[a][r4 * sub4:(r4 + 1) * sub4, :] = x
            dst[a][1, lead[a] + r4 * sub4:lead[a] + (r4 + 1) * sub4, 0:HEAD] = x.astype(dst[a].dtype)
    for a in range(3):
        for r4 in range(4):
            for j in range(4):
                r16 = r4 + 4 * j
                x = by4[a][pl.ds(r4 * sub4 + j, sub16, stride=4), :]
                dst[a][2, lead[a] + r16 * sub16:lead[a] + (r16 + 1) * sub16, 0:HEAD] = x.astype(dst[a].dtype)

    for p, d in enumerate(DILATIONS):
        nb = seq // d // QBLK
        shift = int(math.log2(nb))

        def block(g, carry, p=p, d=d, nb=nb, shift=shift):
            g0 = pl.multiple_of(g * QBLK, QBLK)
            q = qd_ref[p, pl.ds(g0, QBLK), :]
            k = kd_ref[p, pl.ds(g0, 2 * QBLK), :]
            v = vd_ref[p, pl.ds(g0, 2 * QBLK), :]
            n = g & (nb - 1)
            s = lax.dot_general(q, k, (((1,), (1,)), ((), ())), preferred_element_type=jnp.float32)
            s = s + bias_ref[jnp.where(n == 0, 1, 0)]
            m = jnp.max(s, axis=-1, keepdims=True)
            e = jnp.exp2(s - m)
            pv = jnp.dot(e.astype(v.dtype), v, preferred_element_type=jnp.float32)
            den = pv[:, HEAD:]
            o = pv[:, :HEAD] / den
            lse = m + jnp.log2(den)
            r = g >> shift
            if d == 16:
                rows = pl.ds((r & 3) * sub4 + (r >> 2) + n * (QBLK * 4), QBLK, stride=4)
                st_ref[0, rows, :] = o
                st_ref[1, rows, :] = lse
            else:
                rows = pl.ds(r + n * (QBLK * d), QBLK, stride=d) if d > 1 else pl.ds(g0, QBLK)
                on_ref[p, rows, :] = o
                ls_ref[p, rows, :] = lse
            return carry
        lax.fori_loop(0, seq // QBLK, block, 0, unroll=unroll)

    for a, out in enumerate((on_ref, ls_ref)):
        for r4 in range(4):
            out[2, pl.ds(r4, sub4, stride=4), :] = st_ref[a, r4 * sub4:(r4 + 1) * sub4, :]

    def mix(c, carry):
        r0 = pl.multiple_of(c * chunk, chunk)
        ls = [ls_ref[p, pl.ds(r0, chunk), :] for p in range(npat)]
        mx = functools.reduce(jnp.maximum, ls)
        w = [jnp.exp2(l - mx) for l in ls]
        num = sum(w[p] * on_ref[p, pl.ds(r0, chunk), :] for p in range(npat))
        o_ref[pl.ds(r0, chunk), :] = (num / sum(w)).astype(o_ref.dtype)
        return carry
    lax.fori_loop(0, seq // chunk, mix, 0, unroll=2)


def _dil_attn(proj, cos, sin, gq, gk, *, batch, seq):
    npat = len(DILATIONS)

    def qcol(h):
        return jnp.where(h < 4, C_QB_LO // HEAD + h, C_QB_HI // HEAD - 4 + h)

    const = lambda b, h: (0, 0)
    return pl.pallas_call(
        functools.partial(_dil_attn_kernel, seq=seq, unroll=seq // QBLK),
        grid=(batch, N_HEADS),
        in_specs=[
            pl.BlockSpec((seq, HEAD), lambda b, h: (b, qcol(h))),
            pl.BlockSpec((seq, HEAD), lambda b, h: (b, C_KB // HEAD + h)),
            pl.BlockSpec((seq, HEAD), lambda b, h: (b, C_VB // HEAD + h)),
            pl.BlockSpec((seq, HEAD), const),
            pl.BlockSpec((seq, HEAD), const),
            pl.BlockSpec((1, HEAD), const),
            pl.BlockSpec((1, HEAD), const),
        ],
        out_specs=pl.BlockSpec((seq, HEAD), lambda b, h: (b, h)),
        out_shape=jax.ShapeDtypeStruct((batch * seq, WIDTH), jnp.bfloat16),
        scratch_shapes=[
            pltpu.VMEM((npat, seq, HEAD), jnp.bfloat16),
            pltpu.VMEM((npat, seq + QBLK, HEAD), jnp.bfloat16),
            pltpu.VMEM((npat, seq + QBLK, 2 * HEAD), jnp.bfloat16),
            pltpu.VMEM((npat, seq, HEAD), jnp.float32),
            pltpu.VMEM((npat, seq, HEAD), jnp.float32),
            pltpu.VMEM((2, QBLK, 2 * QBLK), jnp.float32),
            pltpu.VMEM((2, seq, HEAD), jnp.float32),
        ],
        compiler_params=pltpu.CompilerParams(
            dimension_semantics=("arbitrary", "arbitrary"), vmem_limit_bytes=VMEM_LIMIT),
        name="dil_attn",
    )(proj, proj, proj, cos, sin, gq, gk)


def _out_proj_kernel(oa_ref, ob_ref, g_ref, x_ref, gcol_ref, w32_ref, o_ref, w_ref, *, nsplit):
    @pl.when(pl.program_id(0) == 0)
    def _():
        rows = 256

        def cast(c, carry):
            r0 = pl.multiple_of(c * rows, rows)
            w = w32_ref[pl.ds(r0, rows), :] * gcol_ref[pl.ds(r0, rows), :]
            w_ref[pl.ds(r0, rows), :] = w.astype(w_ref.dtype)
            return carry
        lax.fori_loop(0, w_ref.shape[0] // rows, cast, 0)

    def branch(o, gate):
        o = o.astype(jnp.float32)
        h = gate.astype(jnp.float32) * 0.5
        silu = h + h * jnp.tanh(h)
        return (o * _rms_scale(o, WIDTH) * silu).astype(jnp.bfloat16)

    step = o_ref.shape[0] // nsplit
    for r in range(0, o_ref.shape[0], step):
        rows = slice(r, r + step)
        ya = branch(oa_ref[rows, :], g_ref[rows, :WIDTH])
        yb = branch(ob_ref[rows, :], g_ref[rows, WIDTH:])
        y = jnp.concatenate([ya, yb], axis=-1)
        o_ref[rows, :] = x_ref[rows, :] + jnp.dot(y, w_ref[...], preferred_element_type=jnp.float32)


def _out_proj(o_a, o_b, proj, x2, gcol, w, *, tm=512):
    m = x2.shape[0]
    const = lambda i: (0, 0)
    return pl.pallas_call(
        functools.partial(_out_proj_kernel, nsplit=2),
        grid=(m // tm,),
        in_specs=[
            pl.BlockSpec((tm, WIDTH), lambda i: (i, 0)),
            pl.BlockSpec((tm, WIDTH), lambda i: (i, 0)),
            pl.BlockSpec((tm, 2 * WIDTH), lambda i: (i, C_GATE // (2 * WIDTH))),
            pl.BlockSpec((tm, D_MODEL), lambda i: (i, 0)),
            pl.BlockSpec((2 * WIDTH, 1), const, pipeline_mode=pl.Buffered(1)),
            pl.BlockSpec((2 * WIDTH, D_MODEL), const, pipeline_mode=pl.Buffered(1)),
        ],
        out_specs=pl.BlockSpec((tm, D_MODEL), lambda i: (i, 0)),
        out_shape=jax.ShapeDtypeStruct((m, D_MODEL), jnp.float32),
        scratch_shapes=[pltpu.VMEM((2 * WIDTH, D_MODEL), jnp.bfloat16)],
        compiler_params=pltpu.CompilerParams(
            dimension_semantics=("arbitrary",), vmem_limit_bytes=VMEM_LIMIT),
        name="out_proj",
    )(o_a, o_b, proj, x2, gcol, w)


def _pad_rope_cols(a):
    z = jnp.zeros(a.shape[:-1] + (ROPE // 2,), a.dtype)
    return jnp.concatenate([a[..., :ROPE // 2], z, a[..., ROPE // 2:], z], axis=-1)


def _rope_tables(seq, d, padded):
    inv = ROPE_THETA ** (-np.arange(0, d, 2, dtype=np.float64) / d)
    ang = np.arange(seq, dtype=np.float64)[:, None] * inv[None, :]
    cos, sin = np.cos(ang), np.sin(ang)
    cos2 = np.concatenate([cos, cos], axis=-1)
    sin2 = np.concatenate([-sin, sin], axis=-1)
    if padded:
        z = np.zeros((seq, d // 2))
        cos2 = np.concatenate([cos, z, cos, z], axis=-1)
        sin2 = np.concatenate([-sin, z, sin, z], axis=-1)
    return jnp.asarray(cos2, jnp.float32), jnp.asarray(sin2, jnp.float32)


def kernel(x, norm_gain, w_in, q_a_norm_gain, kv_a_norm_gain, w_uq, w_ukv, mla_q_norm_gain,
           mla_k_norm_gain, dil_q_norm_gain, dil_k_norm_gain, mla_out_norm_gain,
           dil_out_norm_gain, w_out):
    batch, seq, _ = x.shape
    depth = w_in.shape[0]
    bf = jnp.bfloat16
    cos_d, sin_d = _rope_tables(seq, HEAD, padded=False)
    cos_m, sin_m = _rope_tables(seq, ROPE, padded=True)
    h2 = x.reshape(batch * seq, D_MODEL)

    for l in range(depth):
        wq = w_uq[l].reshape(Q_RANK, N_HEADS, MLA_QK)
        wq_r = jnp.concatenate([wq[..., :HEAD], _pad_rope_cols(wq[..., HEAD:])], axis=-1)
        wq_r = wq_r.reshape(Q_RANK, N_HEADS * 2 * HEAD).astype(bf)
        wkv = w_ukv[l].reshape(KV_RANK, N_HEADS, 2 * HEAD)
        wkv_r = jnp.concatenate([wkv[..., :HEAD].reshape(KV_RANK, WIDTH),
                                 wkv[..., HEAD:].reshape(KV_RANK, WIDTH)], axis=-1).astype(bf)

        gq_m = mla_q_norm_gain[l]
        gk_m = mla_k_norm_gain[l]
        gq_full = jnp.concatenate([gq_m[:HEAD], _pad_rope_cols(gq_m[HEAD:])])[None, :]
        gq_full = gq_full * (math.log2(math.e) / math.sqrt(MLA_QK))
        gk_full = jnp.concatenate([gk_m[:HEAD], _pad_rope_cols(gk_m[HEAD:])])[None, :]
        gq_d = dil_q_norm_gain[l][None, :] * (math.log2(math.e) / math.sqrt(HEAD))
        gk_d = dil_k_norm_gain[l][None, :]

        proj = _in_proj(h2, norm_gain[l][None, :], jnp.swapaxes(w_in[l], 0, 1))
        q_f, k_f, v_a = _mla_prep(proj, wq_r, wkv_r, q_a_norm_gain[l][None, :],
                                  kv_a_norm_gain[l][None, :], gq_full, gk_full, cos_m, sin_m, seq=seq)
        o_a = _mla_attn(q_f, k_f, v_a, batch=batch, seq=seq)
        o_b = _dil_attn(proj, cos_d, sin_d, gq_d, gk_d, batch=batch, seq=seq)
        gcol = jnp.concatenate([mla_out_norm_gain[l], dil_out_norm_gain[l]])[:, None]
        h2 = _out_proj(o_a, o_b, proj, h2, gcol, w_out[l])
    return h2.reshape(batch, seq, D_MODEL)
```

```python
import functools
import math

import jax
import jax.numpy as jnp
import numpy as np
from jax import lax
from jax.experimental import pallas as pl
from jax.experimental.pallas import tpu as pltpu

EPS = 1e-6
ROPE_THETA = 10000.0
NEG = -1e30

D_MODEL = 2048
N_HEADS = 8
HEAD = 128
ROPE = 64
Q_RANK = 768
KV_RANK = 512
WIDTH = N_HEADS * HEAD
MLA_QK = HEAD + ROPE
QBLK = 128
N_BACK = 128
DILATIONS = (1, 4, 16)

C_CKV = 0
C_KR = C_CKV + KV_RANK
C_CQ = C_KR + 2 * HEAD
C_QB_LO = C_CQ + Q_RANK
C_GATE = C_QB_LO + 4 * HEAD
C_QB_HI = C_GATE + 2 * WIDTH
C_KB = C_QB_HI + 4 * HEAD
C_VB = C_KB + WIDTH
PROJ_COLS = C_VB + WIDTH
LAT_COLS = C_QB_LO
assert C_GATE % (2 * WIDTH) == 0 and PROJ_COLS == 6656

V7X_VMEM_BYTES = 64 * 1024 * 1024
VMEM_LIMIT = V7X_VMEM_BYTES - 8 * 1024 * 1024
IN_PROJ_VMEM_LIMIT = V7X_VMEM_BYTES - 4 * 1024 * 1024


def _rms_scale(x, n):
    return lax.rsqrt(jnp.sum(x * x, axis=-1, keepdims=True) * (1.0 / n) + EPS)


def _rms_scale_mxu(x, n):
    ones = jnp.ones((x.shape[-1], x.shape[-1]), jnp.bfloat16)
    ss = jnp.dot((x * x).astype(jnp.bfloat16), ones, preferred_element_type=jnp.float32)
    return lax.rsqrt(ss * (1.0 / n) + EPS)


_O_CQ, _O_CKV, _O_KR = 0, Q_RANK, Q_RANK + KV_RANK
_O_GA = _O_KR + ROPE
_O_QB = _O_GA + WIDTH
_O_KB, _O_VB, _O_GB = _O_QB + WIDTH, _O_QB + 2 * WIDTH, _O_QB + 3 * WIDTH
_IN_COLS = _O_GB + WIDTH


W_TILE = 512
_W_TILE_SRC = (_O_CKV, _O_CQ, _O_CQ + 256, _O_QB, _O_GA, _O_GA + 512, _O_GB, _O_GB + 512,
               _O_QB + 512, _O_KB, _O_KB + 512, _O_VB, _O_VB + 512)
assert len(_W_TILE_SRC) * W_TILE == PROJ_COLS


X_CHUNK = 512
N_CHUNKS = 8


def _in_proj_kernel(src_ref, x_ref, g_ref, w_ref, kr_ref, o_ref, hn0_ref, hn1_ref, *, row_chunk, n_tiles):
    del src_ref
    i = pl.program_id(0)
    j = pl.program_id(1)

    def norm(hn_ref):
        base = pl.multiple_of(j * X_CHUNK, X_CHUNK)

        def body(c, carry):
            r0 = pl.multiple_of(c * row_chunk, row_chunk)
            x = x_ref[pl.ds(r0, row_chunk), :]
            hn = x * _rms_scale(x, D_MODEL) * g_ref[...]
            hn_ref[pl.ds(base + r0, row_chunk), :] = hn.astype(hn_ref.dtype)
            return carry
        lax.fori_loop(0, X_CHUNK // row_chunk, body, 0, unroll=4)

    def weight_tile():
        generic = w_ref[...].astype(jnp.bfloat16)
        half = ROPE // 2
        kr = kr_ref[...].astype(jnp.bfloat16)
        z = jnp.zeros((half, D_MODEL), jnp.bfloat16)
        special = jnp.concatenate([kr[:half], z, kr[half:], z, jnp.zeros((HEAD, D_MODEL), jnp.bfloat16),
                                   generic[:2 * HEAD, :]], axis=0)
        return jnp.where(j == 1, special, generic)

    def project(hn_ref):
        w = weight_tile()
        quarter = o_ref.shape[0] // 4
        for r in range(0, o_ref.shape[0], quarter):
            o_ref[r:r + quarter, :] = lax.dot_general(
                hn_ref[r:r + quarter, :], w, (((1,), (1,)), ((), ())),
                preferred_element_type=jnp.float32).astype(o_ref.dtype)

    do_norm = (i < n_tiles) & (j < N_CHUNKS)
    even = (i % 2) == 0

    @pl.when((i == 0) & do_norm)
    def _():
        norm(hn0_ref)

    for is_even, hn_cur, hn_prev in ((True, hn0_ref, hn1_ref), (False, hn1_ref, hn0_ref)):
        parity = even if is_even else jnp.logical_not(even)

        @pl.when((i > 0) & parity & do_norm)
        def _(hn_cur=hn_cur, hn_prev=hn_prev):
            project(hn_prev)
            norm(hn_cur)

        @pl.when((i > 0) & parity & jnp.logical_not(do_norm))
        def _(hn_prev=hn_prev):
            project(hn_prev)


def _in_proj(x2, gain, wt):
    m = x2.shape[0]
    tm = X_CHUNK * N_CHUNKS
    n_tiles = m // tm
    n_col = PROJ_COLS // W_TILE
    assert N_CHUNKS <= n_col
    assert all(s % ROPE == 0 for s in _W_TILE_SRC)
    src = jnp.asarray([s // ROPE for s in _W_TILE_SRC], jnp.int32)

    def x_index(i, j, src):
        return (jnp.where(i < n_tiles, i * N_CHUNKS + jnp.minimum(j, N_CHUNKS - 1), n_tiles * N_CHUNKS - 1), 0)

    def out_index(i, j, src):
        return (jnp.maximum(i - 1, 0), jnp.where(i == 0, 0, j))

    def w_index(i, j, src):
        return (src[jnp.where(i == 0, 0, j)] * ROPE, 0)

    return pl.pallas_call(
        functools.partial(_in_proj_kernel, row_chunk=64, n_tiles=n_tiles),
        grid_spec=pltpu.PrefetchScalarGridSpec(
            num_scalar_prefetch=1,
            grid=(n_tiles + 1, n_col),
            in_specs=[
                pl.BlockSpec((X_CHUNK, D_MODEL), x_index),
                pl.BlockSpec((1, D_MODEL), lambda i, j, src: (0, 0)),
                pl.BlockSpec((pl.Element(W_TILE), pl.Element(D_MODEL)), w_index),
                pl.BlockSpec((pl.Element(ROPE), pl.Element(D_MODEL)), lambda i, j, src: (_O_KR, 0)),
            ],
            out_specs=pl.BlockSpec((tm, W_TILE), out_index),
            scratch_shapes=[pltpu.VMEM((tm, D_MODEL), jnp.bfloat16),
                            pltpu.VMEM((tm, D_MODEL), jnp.bfloat16)],
        ),
        out_shape=jax.ShapeDtypeStruct((m, PROJ_COLS), jnp.bfloat16),
        compiler_params=pltpu.CompilerParams(
            dimension_semantics=("arbitrary", "arbitrary"),
            vmem_limit_bytes=IN_PROJ_VMEM_LIMIT),
        name="in_proj",
    )(src, x2, gain, wt, wt)


def _rope_pad(x, cos, sin):
    return x * cos + pltpu.roll(x, 64, 1) * sin


def _mla_prep_kernel(lat_ref, wq_ref, wkv_ref, gqa_ref, gkva_ref, gq_ref, gk_ref,
                     cos_ref, sin_ref, q_ref, k_ref, v_ref, *, nsplit):
    rows = lat_ref.shape[0] // nsplit
    for r in range(0, lat_ref.shape[0], rows):
        sl = pl.ds(r, rows)
        _mla_prep_rows(lat_ref.at[sl], wq_ref, wkv_ref, gqa_ref, gkva_ref, gq_ref, gk_ref,
                       cos_ref.at[sl], sin_ref.at[sl], q_ref.at[sl], k_ref.at[sl], v_ref.at[sl])


def _mla_prep_rows(lat_ref, wq_ref, wkv_ref, gqa_ref, gkva_ref, gq_ref, gk_ref,
                   cos_ref, sin_ref, q_ref, k_ref, v_ref):
    cos = cos_ref[...]
    sin = sin_ref[...]
    gq = gq_ref[...]
    gk = gk_ref[...]

    c_kv = lat_ref[:, 0:KV_RANK].astype(jnp.float32)
    cn = (c_kv * _rms_scale(c_kv, KV_RANK) * gkva_ref[...]).astype(jnp.bfloat16)
    kv = jnp.dot(cn, wkv_ref[...], preferred_element_type=jnp.float32)
    ones = jnp.ones((kv.shape[0], HEAD), v_ref.dtype)
    for h in range(N_HEADS):
        v_ref[:, 2 * h * HEAD:(2 * h + 1) * HEAD] = kv[:, WIDTH + h * HEAD:WIDTH + (h + 1) * HEAD].astype(v_ref.dtype)
        v_ref[:, (2 * h + 1) * HEAD:(2 * h + 2) * HEAD] = ones

    k_r = lat_ref[:, KV_RANK:KV_RANK + HEAD].astype(jnp.float32)
    k_r = k_r * _rms_scale(k_r, ROPE) * gk[:, HEAD:]
    k_rope = _rope_pad(k_r, cos, sin).astype(k_ref.dtype)
    for h in range(N_HEADS):
        kn = kv[:, h * HEAD:(h + 1) * HEAD]
        kn = kn * _rms_scale(kn, HEAD) * gk[:, :HEAD]
        k_ref[:, 2 * h * HEAD:(2 * h + 1) * HEAD] = kn.astype(k_ref.dtype)
        k_ref[:, (2 * h + 1) * HEAD:(2 * h + 2) * HEAD] = k_rope

    c_q = lat_ref[:, 2 * HEAD + KV_RANK:].astype(jnp.float32)
    cqn = (c_q * _rms_scale(c_q, Q_RANK) * gqa_ref[...]).astype(jnp.bfloat16)
    q = jnp.dot(cqn, wq_ref[...], preferred_element_type=jnp.float32)
    for h in range(N_HEADS):
        qn = q[:, 2 * h * HEAD:(2 * h + 1) * HEAD]
        qn = qn * _rms_scale(qn, HEAD) * gq[:, :HEAD]
        q_ref[:, 2 * h * HEAD:(2 * h + 1) * HEAD] = qn.astype(q_ref.dtype)
        qr = q[:, (2 * h + 1) * HEAD:(2 * h + 2) * HEAD]
        qr = qr * _rms_scale(qr, ROPE) * gq[:, HEAD:]
        q_ref[:, (2 * h + 1) * HEAD:(2 * h + 2) * HEAD] = _rope_pad(qr, cos, sin).astype(q_ref.dtype)


def _mla_prep(proj, wq, wkv, gqa, gkva, gq, gk, cos, sin, *, seq, tm=1024):
    m = proj.shape[0]
    nseq = seq // tm
    const = lambda i: (0, 0)
    return pl.pallas_call(
        functools.partial(_mla_prep_kernel, nsplit=4),
        grid=(m // tm,),
        in_specs=[
            pl.BlockSpec((tm, LAT_COLS), lambda i: (i, 0)),
            pl.BlockSpec(wq.shape, const),
            pl.BlockSpec(wkv.shape, const),
            pl.BlockSpec(gqa.shape, const),
            pl.BlockSpec(gkva.shape, const),
            pl.BlockSpec(gq.shape, const),
            pl.BlockSpec(gk.shape, const),
            pl.BlockSpec((tm, HEAD), lambda i: (i % nseq, 0)),
            pl.BlockSpec((tm, HEAD), lambda i: (i % nseq, 0)),
        ],
        out_specs=[
            pl.BlockSpec((tm, 2 * WIDTH), lambda i: (i, 0)),
            pl.BlockSpec((tm, 2 * WIDTH), lambda i: (i, 0)),
            pl.BlockSpec((tm, 2 * WIDTH), lambda i: (i, 0)),
        ],
        out_shape=[
            jax.ShapeDtypeStruct((m, 2 * WIDTH), jnp.bfloat16),
            jax.ShapeDtypeStruct((m, 2 * WIDTH), jnp.bfloat16),
            jax.ShapeDtypeStruct((m, 2 * WIDTH), jnp.bfloat16),
        ],
        compiler_params=pltpu.CompilerParams(
            dimension_semantics=("parallel",), vmem_limit_bytes=VMEM_LIMIT),
        name="mla_prep",
    )(proj, wq, wkv, gqa, gkva, gq, gk, cos, sin)


def _mla_attn_kernel(q_ref, k_ref, v_ref, o_ref, m_ref, acc_ref, sa_ref, sb_ref, *, tq, tk, heads):
    assert tq == tk
    i = pl.program_id(2)
    m_ref[...] = jnp.full(m_ref.shape, NEG, jnp.float32)
    acc_ref[...] = jnp.zeros(acc_ref.shape, jnp.float32)

    def scores(j, s_ref):
        k0 = pl.multiple_of(j * tk, tk)
        for h in range(heads):
            cols = slice(2 * h * HEAD, 2 * (h + 1) * HEAD)
            s_ref[h] = lax.dot_general(q_ref[:, cols], k_ref[pl.ds(k0, tk), cols],
                                       (((1,), (1,)), ((), ())),
                                       preferred_element_type=jnp.float32)

    def update(h, rows, s, v):
        m_prev = m_ref[h, rows, :]
        m_new = jnp.maximum(m_prev, jnp.max(s, axis=-1, keepdims=True))
        alpha = jnp.exp2(m_prev - m_new)
        p = jnp.exp2(s - jnp.tile(m_new, (1, s.shape[1] // HEAD)))
        pv = jnp.dot(p.astype(v.dtype), v, preferred_element_type=jnp.float32)
        acc_ref[h, rows, :] = jnp.tile(alpha, (1, 2)) * acc_ref[h, rows, :] + pv
        m_ref[h, rows, :] = m_new

    def softmax_pv(j, s_ref, masked):
        k0 = pl.multiple_of(j * tk, tk)
        hq = tq // 2
        for h in range(heads):
            cols = slice(2 * h * HEAD, 2 * (h + 1) * HEAD)
            if not masked:
                update(h, slice(None), s_ref[h], v_ref[pl.ds(k0, tk), cols])
                continue
            row = lax.broadcasted_iota(jnp.int32, (hq, hq), 0)
            col = lax.broadcasted_iota(jnp.int32, (hq, hq), 1)
            tri = col <= row
            s0 = jnp.where(tri, s_ref[h, 0:hq, 0:hq], NEG)
            update(h, slice(0, hq), s0, v_ref[pl.ds(k0, hq), cols])
            s1 = jnp.concatenate([s_ref[h, hq:, 0:hq], jnp.where(tri, s_ref[h, hq:, hq:], NEG)], axis=1)
            update(h, slice(hq, tq), s1, v_ref[pl.ds(k0, tk), cols])

    scores(0, sa_ref)

    def body(t, carry):
        j = 2 * t
        scores(j + 1, sb_ref)
        softmax_pv(j, sa_ref, False)
        scores(j + 2, sa_ref)
        softmax_pv(j + 1, sb_ref, False)
        return carry
    lax.fori_loop(0, i // 2, body, 0)

    @pl.when(i % 2 == 0)
    def _():
        softmax_pv(i, sa_ref, True)

    @pl.when(i % 2 == 1)
    def _():
        scores(i, sb_ref)
        softmax_pv(i - 1, sa_ref, False)
        softmax_pv(i, sb_ref, True)

    for h in range(heads):
        acc = acc_ref[h]
        o_ref[:, h * HEAD:(h + 1) * HEAD] = (acc[:, :HEAD] / acc[:, HEAD:]).astype(o_ref.dtype)


def _mla_attn(q, k, v, *, batch, seq, tq=512, tk=512, heads=4):
    nq = seq // tq
    w = 2 * HEAD * heads
    return pl.pallas_call(
        functools.partial(_mla_attn_kernel, tq=tq, tk=tk, heads=heads),
        grid=(batch, N_HEADS // heads, nq),
        in_specs=[
            pl.BlockSpec((tq, w), lambda b, h, i: (b * nq + i, h)),
            pl.BlockSpec((seq, w), lambda b, h, i: (b, h)),
            pl.BlockSpec((seq, w), lambda b, h, i: (b, h)),
        ],
        out_specs=pl.BlockSpec((tq, HEAD * heads), lambda b, h, i: (b * nq + i, h)),
        out_shape=jax.ShapeDtypeStruct((batch * seq, WIDTH), jnp.bfloat16),
        scratch_shapes=[
            pltpu.VMEM((heads, tq, HEAD), jnp.float32),
            pltpu.VMEM((heads, tq, 2 * HEAD), jnp.float32),
            pltpu.VMEM((heads, tq, tk), jnp.float32),
            pltpu.VMEM((heads, tq, tk), jnp.float32),
        ],
        compiler_params=pltpu.CompilerParams(
            dimension_semantics=("parallel", "parallel", "arbitrary"),
            vmem_limit_bytes=VMEM_LIMIT),
        name="mla_attn",
    )(q, k, v)


def _dil_attn_kernel(q_ref, k_ref, v_ref, cos_ref, sin_ref, gq_ref, gk_ref, o_ref,
                     qd_ref, kd_ref, vd_ref, on_ref, ls_ref, bias_ref, st_ref, *, seq, unroll):
    chunk = 512
    npat = len(DILATIONS)
    assert DILATIONS == (1, 4, 16)
    nat = [on_ref.at[a] for a in range(3)]
    by4 = [ls_ref.at[a] for a in range(3)]
    dst = [qd_ref, kd_ref, vd_ref]
    lead = [0, QBLK, QBLK]

    @pl.when((pl.program_id(0) == 0) & (pl.program_id(1) == 0))
    def _():
        qi = lax.broadcasted_iota(jnp.int32, (QBLK, 2 * QBLK), 0)
        kj = lax.broadcasted_iota(jnp.int32, (QBLK, 2 * QBLK), 1)
        band = (kj >= qi) & (kj <= qi + N_BACK)
        bias_ref[0] = jnp.where(band, 0.0, NEG)
        bias_ref[1] = jnp.where(band & (kj >= QBLK), 0.0, NEG)
        for p in range(npat):
            vd_ref[p, QBLK:, HEAD:] = jnp.ones((seq, HEAD), vd_ref.dtype)
            kd_ref[p, 0:QBLK, :] = jnp.zeros((QBLK, HEAD), kd_ref.dtype)
            vd_ref[p, 0:QBLK, :] = jnp.zeros((QBLK, 2 * HEAD), vd_ref.dtype)

    def prep(c, carry):
        r0 = pl.multiple_of(c * chunk, chunk)
        cos = cos_ref[pl.ds(r0, chunk), :]
        sin = sin_ref[pl.ds(r0, chunk), :]
        q = q_ref[pl.ds(r0, chunk), :].astype(jnp.float32)
        q = q * _rms_scale_mxu(q, HEAD) * gq_ref[...]
        q = q * cos + pltpu.roll(q, 64, 1) * sin
        k = k_ref[pl.ds(r0, chunk), :].astype(jnp.float32)
        k = k * _rms_scale_mxu(k, HEAD) * gk_ref[...]
        k = k * cos + pltpu.roll(k, 64, 1) * sin
        v = v_ref[pl.ds(r0, chunk), :]
        nat[0][pl.ds(r0, chunk), :] = q
        nat[1][pl.ds(r0, chunk), :] = k
        nat[2][pl.ds(r0, chunk), :] = v.astype(jnp.float32)
        qd_ref[0, pl.ds(r0, chunk), :] = q.astype(qd_ref.dtype)
        kd_ref[0, pl.ds(QBLK + r0, chunk), :] = k.astype(kd_ref.dtype)
        vd_ref[0, pl.ds(QBLK + r0, chunk), 0:HEAD] = v
        return carry
    lax.fori_loop(0, seq // chunk, prep, 0, unroll=2)

    sub4, sub16 = seq // 4, seq // 16
    for a in range(3):
        for r4 in range(4):
            x = nat[a][pl.ds(r4, sub4, stride=4), :]
            by4[a][r4 * sub4:(r4 + 1) * sub4, :] = x
            dst[a][1, lead[a] + r4 * sub4:lead[a] + (r4 + 1) * sub4, 0:HEAD] = x.astype(dst[a].dtype)
    for a in range(3):
        for r4 in range(4):
            for j in range(4):
                r16 = r4 + 4 * j
                x = by4[a][pl.ds(r4 * sub4 + j, sub16, stride=4), :]
                dst[a][2, lead[a] + r16 * sub16:lead[a] + (r16 + 1) * sub16, 0:HEAD] = x.astype(dst[a].dtype)

    for p, d in enumerate(DILATIONS):
        nb = seq // d // QBLK
        shift = int(math.log2(nb))

        def block(g, carry, p=p, d=d, nb=nb, shift=shift):
            g0 = pl.multiple_of(g * QBLK, QBLK)
            q = qd_ref[p, pl.ds(g0, QBLK), :]
            k = kd_ref[p, pl.ds(g0, 2 * QBLK), :]
            v = vd_ref[p, pl.ds(g0, 2 * QBLK), :]
            n = g & (nb - 1)
            s = lax.dot_general(q, k, (((1,), (1,)), ((), ())), preferred_element_type=jnp.float32)
            s = s + bias_ref[jnp.where(n == 0, 1, 0)]
            m = jnp.max(s, axis=-1, keepdims=True)
            e = jnp.exp2(s - m)
            pv = jnp.dot(e.astype(v.dtype), v, preferred_element_type=jnp.float32)
            den = pv[:, HEAD:]
            o = pv[:, :HEAD] / den
            lse = m + jnp.log2(den)
            r = g >> shift
            if d == 16:
                rows = pl.ds((r & 3) * sub4 + (r >> 2) + n * (QBLK * 4), QBLK, stride=4)
                st_ref[0, rows, :] = o
                st_ref[1, rows, :] = lse
            else:
                rows = pl.ds(r + n * (QBLK * d), QBLK, stride=d) if d > 1 else pl.ds(g0, QBLK)
                on_ref[p, rows, :] = o
                ls_ref[p, rows, :] = lse
            return carry
        lax.fori_loop(0, seq // QBLK, block, 0, unroll=unroll)

    for a, out in enumerate((on_ref, ls_ref)):
        for r4 in range(4):
            out[2, pl.ds(r4, sub4, stride=4), :] = st_ref[a, r4 * sub4:(r4 + 1) * sub4, :]

    def mix(c, carry):
        r0 = pl.multiple_of(c * chunk, chunk)
        ls = [ls_ref[p, pl.ds(r0, chunk), :] for p in range(npat)]
        mx = functools.reduce(jnp.maximum, ls)
        w = [jnp.exp2(l - mx) for l in ls]
        num = sum(w[p] * on_ref[p, pl.ds(r0, chunk), :] for p in range(npat))
        o_ref[pl.ds(r0, chunk), :] = (num / sum(w)).astype(o_ref.dtype)
        return carry
    lax.fori_loop(0, seq // chunk, mix, 0, unroll=2)


def _dil_attn(proj, cos, sin, gq, gk, *, batch, seq):
    npat = len(DILATIONS)

    def qcol(h):
        return jnp.where(h < 4, C_QB_LO // HEAD + h, C_QB_HI // HEAD - 4 + h)

    const = lambda b, h: (0, 0)
    return pl.pallas_call(
        functools.partial(_dil_attn_kernel, seq=seq, unroll=seq // QBLK),
        grid=(batch, N_HEADS),
        in_specs=[
            pl.BlockSpec((seq, HEAD), lambda b, h: (b, qcol(h))),
            pl.BlockSpec((seq, HEAD), lambda b, h: (b, C_KB // HEAD + h)),
            pl.BlockSpec((seq, HEAD), lambda b, h: (b, C_VB // HEAD + h)),
            pl.BlockSpec((seq, HEAD), const),
            pl.BlockSpec((seq, HEAD), const),
            pl.BlockSpec((1, HEAD), const),
            pl.BlockSpec((1, HEAD), const),
        ],
        out_specs=pl.BlockSpec((seq, HEAD), lambda b, h: (b, h)),
        out_shape=jax.ShapeDtypeStruct((batch * seq, WIDTH), jnp.bfloat16),
        scratch_shapes=[
            pltpu.VMEM((npat, seq, HEAD), jnp.bfloat16),
            pltpu.VMEM((npat, seq + QBLK, HEAD), jnp.bfloat16),
            pltpu.VMEM((npat, seq + QBLK, 2 * HEAD), jnp.bfloat16),
            pltpu.VMEM((npat, seq, HEAD), jnp.float32),
            pltpu.VMEM((npat, seq, HEAD), jnp.float32),
            pltpu.VMEM((2, QBLK, 2 * QBLK), jnp.float32),
            pltpu.VMEM((2, seq, HEAD), jnp.float32),
        ],
        compiler_params=pltpu.CompilerParams(
            dimension_semantics=("arbitrary", "arbitrary"), vmem_limit_bytes=VMEM_LIMIT),
        name="dil_attn",
    )(proj, proj, proj, cos, sin, gq, gk)


def _out_proj_kernel(oa_ref, ob_ref, g_ref, x_ref, gcol_ref, w32_ref, o_ref, w_ref, *, nsplit):
    @pl.when(pl.program_id(0) == 0)
    def _():
        rows = 256

        def cast(c, carry):
            r0 = pl.multiple_of(c * rows, rows)
            w = w32_ref[pl.ds(r0, rows), :] * gcol_ref[pl.ds(r0, rows), :]
            w_ref[pl.ds(r0, rows), :] = w.astype(w_ref.dtype)
            return carry
        lax.fori_loop(0, w_ref.shape[0] // rows, cast, 0)

    def branch(o, gate):
        o = o.astype(jnp.float32)
        h = gate.astype(jnp.float32) * 0.5
        silu = h + h * jnp.tanh(h)
        return (o * _rms_scale(o, WIDTH) * silu).astype(jnp.bfloat16)

    step = o_ref.shape[0] // nsplit
    for r in range(0, o_ref.shape[0], step):
        rows = slice(r, r + step)
        ya = branch(oa_ref[rows, :], g_ref[rows, :WIDTH])
        yb = branch(ob_ref[rows, :], g_ref[rows, WIDTH:])
        y = jnp.concatenate([ya, yb], axis=-1)
        o_ref[rows, :] = x_ref[rows, :] + jnp.dot(y, w_ref[...], preferred_element_type=jnp.float32)


def _out_proj(o_a, o_b, proj, x2, gcol, w, *, tm=512):
    m = x2.shape[0]
    const = lambda i: (0, 0)
    return pl.pallas_call(
        functools.partial(_out_proj_kernel, nsplit=2),
        grid=(m // tm,),
        in_specs=[
            pl.BlockSpec((tm, WIDTH), lambda i: (i, 0)),
            pl.BlockSpec((tm, WIDTH), lambda i: (i, 0)),
            pl.BlockSpec((tm, 2 * WIDTH), lambda i: (i, C_GATE // (2 * WIDTH))),
            pl.BlockSpec((tm, D_MODEL), lambda i: (i, 0)),
            pl.BlockSpec((2 * WIDTH, 1), const, pipeline_mode=pl.Buffered(1)),
            pl.BlockSpec((2 * WIDTH, D_MODEL), const, pipeline_mode=pl.Buffered(1)),
        ],
        out_specs=pl.BlockSpec((tm, D_MODEL), lambda i: (i, 0)),
        out_shape=jax.ShapeDtypeStruct((m, D_MODEL), jnp.float32),
        scratch_shapes=[pltpu.VMEM((2 * WIDTH, D_MODEL), jnp.bfloat16)],
        compiler_params=pltpu.CompilerParams(
            dimension_semantics=("arbitrary",), vmem_limit_bytes=VMEM_LIMIT),
        name="out_proj",
    )(o_a, o_b, proj, x2, gcol, w)


def _pad_rope_cols(a):
    z = jnp.zeros(a.shape[:-1] + (ROPE // 2,), a.dtype)
    return jnp.concatenate([a[..., :ROPE // 2], z, a[..., ROPE // 2:], z], axis=-1)


def _rope_tables(seq, d, padded):
    inv = ROPE_THETA ** (-np.arange(0, d, 2, dtype=np.float64) / d)
    ang = np.arange(seq, dtype=np.float64)[:, None] * inv[None, :]
    cos, sin = np.cos(ang), np.sin(ang)
    cos2 = np.concatenate([cos, cos], axis=-1)
    sin2 = np.concatenate([-sin, sin], axis=-1)
    if padded:
        z = np.zeros((seq, d // 2))
        cos2 = np.concatenate([cos, z, cos, z], axis=-1)
        sin2 = np.concatenate([-sin, z, sin, z], axis=-1)
    return jnp.asarray(cos2, jnp.float32), jnp.asarray(sin2, jnp.float32)


def kernel(x, norm_gain, w_in, q_a_norm_gain, kv_a_norm_gain, w_uq, w_ukv, mla_q_norm_gain,
           mla_k_norm_gain, dil_q_norm_gain, dil_k_norm_gain, mla_out_norm_gain,
           dil_out_norm_gain, w_out):
    batch, seq, _ = x.shape
    depth = w_in.shape[0]
    bf = jnp.bfloat16
    cos_d, sin_d = _rope_tables(seq, HEAD, padded=False)
    cos_m, sin_m = _rope_tables(seq, ROPE, padded=True)
    h2 = x.reshape(batch * seq, D_MODEL)

    for l in range(depth):
        wq = w_uq[l].reshape(Q_RANK, N_HEADS, MLA_QK)
        wq_r = jnp.concatenate([wq[..., :HEAD], _pad_rope_cols(wq[..., HEAD:])], axis=-1)
        wq_r = wq_r.reshape(Q_RANK, N_HEADS * 2 * HEAD).astype(bf)
        wkv = w_ukv[l].reshape(KV_RANK, N_HEADS, 2 * HEAD)
        wkv_r = jnp.concatenate([wkv[..., :HEAD].reshape(KV_RANK, WIDTH),
                                 wkv[..., HEAD:].reshape(KV_RANK, WIDTH)], axis=-1).astype(bf)

        gq_m = mla_q_norm_gain[l]
        gk_m = mla_k_norm_gain[l]
        gq_full = jnp.concatenate([gq_m[:HEAD], _pad_rope_cols(gq_m[HEAD:])])[None, :]
        gq_full = gq_full * (math.log2(math.e) / math.sqrt(MLA_QK))
        gk_full = jnp.concatenate([gk_m[:HEAD], _pad_rope_cols(gk_m[HEAD:])])[None, :]
        gq_d = dil_q_norm_gain[l][None, :] * (math.log2(math.e) / math.sqrt(HEAD))
        gk_d = dil_k_norm_gain[l][None, :]

        proj = _in_proj(h2, norm_gain[l][None, :], jnp.swapaxes(w_in[l], 0, 1))
        q_f, k_f, v_a = _mla_prep(proj, wq_r, wkv_r, q_a_norm_gain[l][None, :],
                                  kv_a_norm_gain[l][None, :], gq_full, gk_full, cos_m, sin_m, seq=seq)
        o_a = _mla_attn(q_f, k_f, v_a, batch=batch, seq=seq)
        o_b = _dil_attn(proj, cos_d, sin_d, gq_d, gk_d, batch=batch, seq=seq)
        gcol = jnp.concatenate([mla_out_norm_gain[l], dil_out_norm_gain[l]])[:, None]
        h2 = _out_proj(o_a, o_b, proj, h2, gcol, w_out[l])
    return h2.reshape(batch, seq, D_MODEL)
```

```python
import functools
import math

import jax
import jax.numpy as jnp
import numpy as np
from jax import lax
from jax.experimental import pallas as pl
from jax.experimental.pallas import tpu as pltpu

EPS = 1e-6
ROPE_THETA = 10000.0
NEG = -1e30

D_MODEL = 2048
N_HEADS = 8
HEAD = 128
ROPE = 64
Q_RANK = 768
KV_RANK = 512
WIDTH = N_HEADS * HEAD
MLA_QK = HEAD + ROPE
QBLK = 128
N_BACK = 128
DILATIONS = (1, 4, 16)

C_CKV = 0
C_KR = C_CKV + KV_RANK
C_CQ = C_KR + 2 * HEAD
C_QB_LO = C_CQ + Q_RANK
C_GATE = C_QB_LO + 4 * HEAD
C_QB_HI = C_GATE + 2 * WIDTH
C_KB = C_QB_HI + 4 * HEAD
C_VB = C_KB + WIDTH
PROJ_COLS = C_VB + WIDTH
LAT_COLS = C_QB_LO
assert C_GATE % (2 * WIDTH) == 0 and PROJ_COLS == 6656

V7X_VMEM_BYTES = 64 * 1024 * 1024
VMEM_LIMIT = V7X_VMEM_BYTES - 8 * 1024 * 1024
IN_PROJ_VMEM_LIMIT = V7X_VMEM_BYTES - 4 * 1024 * 1024


def _rms_scale(x, n):
    return lax.rsqrt(jnp.sum(x * x, axis=-1, keepdims=True) * (1.0 / n) + EPS)


def _rms_scale_mxu(x, n):
    ones = jnp.ones((x.shape[-1], x.shape[-1]), jnp.bfloat16)
    ss = jnp.dot((x * x).astype(jnp.bfloat16), ones, preferred_element_type=jnp.float32)
    return lax.rsqrt(ss * (1.0 / n) + EPS)


_O_CQ, _O_CKV, _O_KR = 0, Q_RANK, Q_RANK + KV_RANK
_O_GA = _O_KR + ROPE
_O_QB = _O_GA + WIDTH
_O_KB, _O_VB, _O_GB = _O_QB + WIDTH, _O_QB + 2 * WIDTH, _O_QB + 3 * WIDTH
_IN_COLS = _O_GB + WIDTH


W_TILE = 512
_W_TILE_SRC = (_O_CKV, _O_CQ, _O_CQ + 256, _O_QB, _O_GA, _O_GA + 512, _O_GB, _O_GB + 512,
               _O_QB + 512, _O_KB, _O_KB + 512, _O_VB, _O_VB + 512)
assert len(_W_TILE_SRC) * W_TILE == PROJ_COLS


X_CHUNK = 512
N_CHUNKS = 8


def _in_proj_kernel(src_ref, x_ref, g_ref, w_ref, kr_ref, o_ref, hn0_ref, hn1_ref, *, row_chunk, n_tiles):
    del src_ref
    i = pl.program_id(0)
    j = pl.program_id(1)

    def norm(hn_ref):
        base = pl.multiple_of(j * X_CHUNK, X_CHUNK)

        def body(c, carry):
            r0 = pl.multiple_of(c * row_chunk, row_chunk)
            x = x_ref[pl.ds(r0, row_chunk), :]
            hn = x * _rms_scale(x, D_MODEL) * g_ref[...]
            hn_ref[pl.ds(base + r0, row_chunk), :] = hn.astype(hn_ref.dtype)
            return carry
        lax.fori_loop(0, X_CHUNK // row_chunk, body, 0, unroll=4)

    def weight_tile():
        generic = w_ref[...].astype(jnp.bfloat16)
        half = ROPE // 2
        kr = kr_ref[...].astype(jnp.bfloat16)
        z = jnp.zeros((half, D_MODEL), jnp.bfloat16)
        special = jnp.concatenate([kr[:half], z, kr[half:], z, jnp.zeros((HEAD, D_MODEL), jnp.bfloat16),
                                   generic[:2 * HEAD, :]], axis=0)
        return jnp.where(j == 1, special, generic)

    def project(hn_ref):
        w = weight_tile()
        quarter = o_ref.shape[0] // 4
        for r in range(0, o_ref.shape[0], quarter):
            o_ref[r:r + quarter, :] = lax.dot_general(
                hn_ref[r:r + quarter, :], w, (((1,), (1,)), ((), ())),
                preferred_element_type=jnp.float32).astype(o_ref.dtype)

    do_norm = (i < n_tiles) & (j < N_CHUNKS)
    even = (i % 2) == 0

    @pl.when((i == 0) & do_norm)
    def _():
        norm(hn0_ref)

    for is_even, hn_cur, hn_prev in ((True, hn0_ref, hn1_ref), (False, hn1_ref, hn0_ref)):
        parity = even if is_even else jnp.logical_not(even)

        @pl.when((i > 0) & parity & do_norm)
        def _(hn_cur=hn_cur, hn_prev=hn_prev):
            project(hn_prev)
            norm(hn_cur)

        @pl.when((i > 0) & parity & jnp.logical_not(do_norm))
        def _(hn_prev=hn_prev):
            project(hn_prev)


def _in_proj(x2, gain, wt):
    m = x2.shape[0]
    tm = X_CHUNK * N_CHUNKS
    n_tiles = m // tm
    n_col = PROJ_COLS // W_TILE
    assert N_CHUNKS <= n_col
    assert all(s % ROPE == 0 for s in _W_TILE_SRC)
    src = jnp.asarray([s // ROPE for s in _W_TILE_SRC], jnp.int32)

    def x_index(i, j, src):
        return (jnp.where(i < n_tiles, i * N_CHUNKS + jnp.minimum(j, N_CHUNKS - 1), n_tiles * N_CHUNKS - 1), 0)

    def out_index(i, j, src):
        return (jnp.maximum(i - 1, 0), jnp.where(i == 0, 0, j))

    def w_index(i, j, src):
        return (src[jnp.where(i == 0, 0, j)] * ROPE, 0)

    return pl.pallas_call(
        functools.partial(_in_proj_kernel, row_chunk=64, n_tiles=n_tiles),
        grid_spec=pltpu.PrefetchScalarGridSpec(
            num_scalar_prefetch=1,
            grid=(n_tiles + 1, n_col),
            in_specs=[
                pl.BlockSpec((X_CHUNK, D_MODEL), x_index),
                pl.BlockSpec((1, D_MODEL), lambda i, j, src: (0, 0)),
                pl.BlockSpec((pl.Element(W_TILE), pl.Element(D_MODEL)), w_index),
                pl.BlockSpec((pl.Element(ROPE), pl.Element(D_MODEL)), lambda i, j, src: (_O_KR, 0)),
            ],
            out_specs=pl.BlockSpec((tm, W_TILE), out_index),
            scratch_shapes=[pltpu.VMEM((tm, D_MODEL), jnp.bfloat16),
                            pltpu.VMEM((tm, D_MODEL), jnp.bfloat16)],
        ),
        out_shape=jax.ShapeDtypeStruct((m, PROJ_COLS), jnp.bfloat16),
        compiler_params=pltpu.CompilerParams(
            dimension_semantics=("arbitrary", "arbitrary"),
            vmem_limit_bytes=IN_PROJ_VMEM_LIMIT),
        name="in_proj",
    )(src, x2, gain, wt, wt)


def _rope_pad(x, cos, sin):
    return x * cos + pltpu.roll(x, 64, 1) * sin


def _mla_prep_kernel(lat_ref, wq_ref, wkv_ref, gqa_ref, gkva_ref, gq_ref, gk_ref,
                     cos_ref, sin_ref, q_ref, k_ref, v_ref, *, nsplit):
    rows = lat_ref.shape[0] // nsplit
    for r in range(0, lat_ref.shape[0], rows):
        sl = pl.ds(r, rows)
        _mla_prep_rows(lat_ref.at[sl], wq_ref, wkv_ref, gqa_ref, gkva_ref, gq_ref, gk_ref,
                       cos_ref.at[sl], sin_ref.at[sl], q_ref.at[sl], k_ref.at[sl], v_ref.at[sl])


def _mla_prep_rows(lat_ref, wq_ref, wkv_ref, gqa_ref, gkva_ref, gq_ref, gk_ref,
                   cos_ref, sin_ref, q_ref, k_ref, v_ref):
    cos = cos_ref[...]
    sin = sin_ref[...]
    gq = gq_ref[...]
    gk = gk_ref[...]

    c_kv = lat_ref[:, 0:KV_RANK].astype(jnp.float32)
    cn = (c_kv * _rms_scale(c_kv, KV_RANK) * gkva_ref[...]).astype(jnp.bfloat16)
    kv = jnp.dot(cn, wkv_ref[...], preferred_element_type=jnp.float32)
    ones = jnp.ones((kv.shape[0], HEAD), v_ref.dtype)
    for h in range(N_HEADS):
        v_ref[:, 2 * h * HEAD:(2 * h + 1) * HEAD] = kv[:, WIDTH + h * HEAD:WIDTH + (h + 1) * HEAD].astype(v_ref.dtype)
        v_ref[:, (2 * h + 1) * HEAD:(2 * h + 2) * HEAD] = ones

    k_r = lat_ref[:, KV_RANK:KV_RANK + HEAD].astype(jnp.float32)
    k_r = k_r * _rms_scale(k_r, ROPE) * gk[:, HEAD:]
    k_rope = _rope_pad(k_r, cos, sin).astype(k_ref.dtype)
    for h in range(N_HEADS):
        kn = kv[:, h * HEAD:(h + 1) * HEAD]
        kn = kn * _rms_scale(kn, HEAD) * gk[:, :HEAD]
        k_ref[:, 2 * h * HEAD:(2 * h + 1) * HEAD] = kn.astype(k_ref.dtype)
        k_ref[:, (2 * h + 1) * HEAD:(2 * h + 2) * HEAD] = k_rope

    c_q = lat_ref[:, 2 * HEAD + KV_RANK:].astype(jnp.float32)
    cqn = (c_q * _rms_scale(c_q, Q_RANK) * gqa_ref[...]).astype(jnp.bfloat16)
    q = jnp.dot(cqn, wq_ref[...], preferred_element_type=jnp.float32)
    for h in range(N_HEADS):
        qn = q[:, 2 * h * HEAD:(2 * h + 1) * HEAD]
        qn = qn * _rms_scale(qn, HEAD) * gq[:, :HEAD]
        q_ref[:, 2 * h * HEAD:(2 * h + 1) * HEAD] = qn.astype(q_ref.dtype)
        qr = q[:, (2 * h + 1) * HEAD:(2 * h + 2) * HEAD]
        qr = qr * _rms_scale(qr, ROPE) * gq[:, HEAD:]
        q_ref[:, (2 * h + 1) * HEAD:(2 * h + 2) * HEAD] = _rope_pad(qr, cos, sin).astype(q_ref.dtype)


def _mla_prep(proj, wq, wkv, gqa, gkva, gq, gk, cos, sin, *, seq, tm=1024):
    m = proj.shape[0]
    nseq = seq // tm
    const = lambda i: (0, 0)
    return pl.pallas_call(
        functools.partial(_mla_prep_kernel, nsplit=4),
        grid=(m // tm,),
        in_specs=[
            pl.BlockSpec((tm, LAT_COLS), lambda i: (i, 0)),
            pl.BlockSpec(wq.shape, const),
            pl.BlockSpec(wkv.shape, const),
            pl.BlockSpec(gqa.shape, const),
            pl.BlockSpec(gkva.shape, const),
            pl.BlockSpec(gq.shape, const),
            pl.BlockSpec(gk.shape, const),
            pl.BlockSpec((tm, HEAD), lambda i: (i % nseq, 0)),
            pl.BlockSpec((tm, HEAD), lambda i: (i % nseq, 0)),
        ],
        out_specs=[
            pl.BlockSpec((tm, 2 * WIDTH), lambda i: (i, 0)),
            pl.BlockSpec((tm, 2 * WIDTH), lambda i: (i, 0)),
            pl.BlockSpec((tm, 2 * WIDTH), lambda i: (i, 0)),
        ],
        out_shape=[
            jax.ShapeDtypeStruct((m, 2 * WIDTH), jnp.bfloat16),
            jax.ShapeDtypeStruct((m, 2 * WIDTH), jnp.bfloat16),
            jax.ShapeDtypeStruct((m, 2 * WIDTH), jnp.bfloat16),
        ],
        compiler_params=pltpu.CompilerParams(
            dimension_semantics=("parallel",), vmem_limit_bytes=VMEM_LIMIT),
        name="mla_prep",
    )(proj, wq, wkv, gqa, gkva, gq, gk, cos, sin)


def _mla_attn_kernel(q_ref, k_ref, v_ref, o_ref, m_ref, acc_ref, sa_ref, sb_ref, *, tq, tk, heads):
    assert tq == tk
    i = pl.program_id(2)
    m_ref[...] = jnp.full(m_ref.shape, NEG, jnp.float32)
    acc_ref[...] = jnp.zeros(acc_ref.shape, jnp.float32)

    def scores(j, s_ref):
        k0 = pl.multiple_of(j * tk, tk)
        for h in range(heads):
            cols = slice(2 * h * HEAD, 2 * (h + 1) * HEAD)
            s_ref[h] = lax.dot_general(q_ref[:, cols], k_ref[pl.ds(k0, tk), cols],
                                       (((1,), (1,)), ((), ())),
                                       preferred_element_type=jnp.float32)

    def update(h, rows, s, v):
        m_prev = m_ref[h, rows, :]
        m_new = jnp.maximum(m_prev, jnp.max(s, axis=-1, keepdims=True))
        alpha = jnp.exp2(m_prev - m_new)
        p = jnp.exp2(s - jnp.tile(m_new, (1, s.shape[1] // HEAD)))
        pv = jnp.dot(p.astype(v.dtype), v, preferred_element_type=jnp.float32)
        acc_ref[h, rows, :] = jnp.tile(alpha, (1, 2)) * acc_ref[h, rows, :] + pv
        m_ref[h, rows, :] = m_new

    def softmax_pv(j, s_ref, masked):
        k0 = pl.multiple_of(j * tk, tk)
        hq = tq // 2
        for h in range(heads):
            cols = slice(2 * h * HEAD, 2 * (h + 1) * HEAD)
            if not masked:
                update(h, slice(None), s_ref[h], v_ref[pl.ds(k0, tk), cols])
                continue
            row = lax.broadcasted_iota(jnp.int32, (hq, hq), 0)
            col = lax.broadcasted_iota(jnp.int32, (hq, hq), 1)
            tri = col <= row
            s0 = jnp.where(tri, s_ref[h, 0:hq, 0:hq], NEG)
            update(h, slice(0, hq), s0, v_ref[pl.ds(k0, hq), cols])
            s1 = jnp.concatenate([s_ref[h, hq:, 0:hq], jnp.where(tri, s_ref[h, hq:, hq:], NEG)], axis=1)
            update(h, slice(hq, tq), s1, v_ref[pl.ds(k0, tk), cols])

    scores(0, sa_ref)

    def body(t, carry):
        j = 2 * t
        scores(j + 1, sb_ref)
        softmax_pv(j, sa_ref, False)
        scores(j + 2, sa_ref)
        softmax_pv(j + 1, sb_ref, False)
        return carry
    lax.fori_loop(0, i // 2, body, 0)

    @pl.when(i % 2 == 0)
    def _():
        softmax_pv(i, sa_ref, True)

    @pl.when(i % 2 == 1)
    def _():
        scores(i, sb_ref)
        softmax_pv(i - 1, sa_ref, False)
        softmax_pv(i, sb_ref, True)

    for h in range(heads):
        acc = acc_ref[h]
        o_ref[:, h * HEAD:(h + 1) * HEAD] = (acc[:, :HEAD] / acc[:, HEAD:]).astype(o_ref.dtype)


def _mla_attn(q, k, v, *, batch, seq, tq=512, tk=512, heads=4):
    nq = seq // tq
    w = 2 * HEAD * heads
    return pl.pallas_call(
        functools.partial(_mla_attn_kernel, tq=tq, tk=tk, heads=heads),
        grid=(batch, N_HEADS // heads, nq),
        in_specs=[
            pl.BlockSpec((tq, w), lambda b, h, i: (b * nq + i, h)),
            pl.BlockSpec((seq, w), lambda b, h, i: (b, h)),
            pl.BlockSpec((seq, w), lambda b, h, i: (b, h)),
        ],
        out_specs=pl.BlockSpec((tq, HEAD * heads), lambda b, h, i: (b * nq + i, h)),
        out_shape=jax.ShapeDtypeStruct((batch * seq, WIDTH), jnp.bfloat16),
        scratch_shapes=[
            pltpu.VMEM((heads, tq, HEAD), jnp.float32),
            pltpu.VMEM((heads, tq, 2 * HEAD), jnp.float32),
            pltpu.VMEM((heads, tq, tk), jnp.float32),
            pltpu.VMEM((heads, tq, tk), jnp.float32),
        ],
        compiler_params=pltpu.CompilerParams(
            dimension_semantics=("parallel", "parallel", "arbitrary"),
            vmem_limit_bytes=VMEM_LIMIT),
        name="mla_attn",
    )(q, k, v)


def _dil_attn_kernel(q_ref, k_ref, v_ref, cos_ref, sin_ref, gq_ref, gk_ref, o_ref,
                     qd_ref, kd_ref, vd_ref, on_ref, ls_ref, bias_ref, st_ref, *, seq, unroll):
    chunk = 512
    npat = len(DILATIONS)
    assert DILATIONS == (1, 4, 16)
    nat = [on_ref.at[a] for a in range(3)]
    by4 = [ls_ref.at[a] for a in range(3)]
    dst = [qd_ref, kd_ref, vd_ref]
    lead = [0, QBLK, QBLK]

    @pl.when((pl.program_id(0) == 0) & (pl.program_id(1) == 0))
    def _():
        qi = lax.broadcasted_iota(jnp.int32, (QBLK, 2 * QBLK), 0)
        kj = lax.broadcasted_iota(jnp.int32, (QBLK, 2 * QBLK), 1)
        band = (kj >= qi) & (kj <= qi + N_BACK)
        bias_ref[0] = jnp.where(band, 0.0, NEG)
        bias_ref[1] = jnp.where(band & (kj >= QBLK), 0.0, NEG)
        for p in range(npat):
            vd_ref[p, QBLK:, HEAD:] = jnp.ones((seq, HEAD), vd_ref.dtype)
            kd_ref[p, 0:QBLK, :] = jnp.zeros((QBLK, HEAD), kd_ref.dtype)
            vd_ref[p, 0:QBLK, :] = jnp.zeros((QBLK, 2 * HEAD), vd_ref.dtype)

    def prep(c, carry):
        r0 = pl.multiple_of(c * chunk, chunk)
        cos = cos_ref[pl.ds(r0, chunk), :]
        sin = sin_ref[pl.ds(r0, chunk), :]
        q = q_ref[pl.ds(r0, chunk), :].astype(jnp.float32)
        q = q * _rms_scale_mxu(q, HEAD) * gq_ref[...]
        q = q * cos + pltpu.roll(q, 64, 1) * sin
        k = k_ref[pl.ds(r0, chunk), :].astype(jnp.float32)
        k = k * _rms_scale_mxu(k, HEAD) * gk_ref[...]
        k = k * cos + pltpu.roll(k, 64, 1) * sin
        v = v_ref[pl.ds(r0, chunk), :]
        nat[0][pl.ds(r0, chunk), :] = q
        nat[1][pl.ds(r0, chunk), :] = k
        nat[2][pl.ds(r0, chunk), :] = v.astype(jnp.float32)
        qd_ref[0, pl.ds(r0, chunk), :] = q.astype(qd_ref.dtype)
        kd_ref[0, pl.ds(QBLK + r0, chunk), :] = k.astype(kd_ref.dtype)
        vd_ref[0, pl.ds(QBLK + r0, chunk), 0:HEAD] = v
        return carry
    lax.fori_loop(0, seq // chunk, prep, 0, unroll=2)

    sub4, sub16 = seq // 4, seq // 16
    for a in range(3):
        for r4 in range(4):
            x = nat[a][pl.ds(r4, sub4, stride=4), :]
            by4[a][r4 * sub4:(r4 + 1) * sub4, :] = x
            dst[a][1, lead[a] + r4 * sub4:lead[a] + (r4 + 1) * sub4, 0:HEAD] = x.astype(dst[a].dtype)
    for a in range(3):
        for r4 in range(4):
            for j in range(4):
                r16 = r4 + 4 * j
                x = by4[a][pl.ds(r4 * sub4 + j, sub16, stride=4), :]
                dst[a][2, lead[a] + r16 * sub16:lead[a] + (r16 + 1) * sub16, 0:HEAD] = x.astype(dst[a].dtype)

    for p, d in enumerate(DILATIONS):
        nb = seq // d // QBLK
        shift = int(math.log2(nb))

        def block(g, carry, p=p, d=d, nb=nb, shift=shift):
            g0 = pl.multiple_of(g * QBLK, QBLK)
            q = qd_ref[p, pl.ds(g0, QBLK), :]
            k = kd_ref[p, pl.ds(g0, 2 * QBLK), :]
            v = vd_ref[p, pl.ds(g0, 2 * QBLK), :]
            n = g & (nb - 1)
            s = lax.dot_general(q, k, (((1,), (1,)), ((), ())), preferred_element_type=jnp.float32)
            s = s + bias_ref[jnp.where(n == 0, 1, 0)]
            m = jnp.max(s, axis=-1, keepdims=True)
            e = jnp.exp2(s - m)
            pv = jnp.dot(e.astype(v.dtype), v, preferred_element_type=jnp.float32)
            den = pv[:, HEAD:]
            o = pv[:, :HEAD] / den
            lse = m + jnp.log2(den)
            r = g >> shift
            if d == 16:
                rows = pl.ds((r & 3) * sub4 + (r >> 2) + n * (QBLK * 4), QBLK, stride=4)
                st_ref[0, rows, :] = o
                st_ref[1, rows, :] = lse
            else:
                rows = pl.ds(r + n * (QBLK * d), QBLK, stride=d) if d > 1 else pl.ds(g0, QBLK)
                on_ref[p, rows, :] = o
                ls_ref[p, rows, :] = lse
            return carry
        @pl.when(pl.program_id(0) >= 0)
        def _(block=block):
            lax.fori_loop(0, seq // QBLK, block, 0, unroll=unroll)

    for a, out in enumerate((on_ref, ls_ref)):
        for r4 in range(4):
            out[2, pl.ds(r4, sub4, stride=4), :] = st_ref[a, r4 * sub4:(r4 + 1) * sub4, :]

    def mix(c, carry):
        r0 = pl.multiple_of(c * chunk, chunk)
        ls = [ls_ref[p, pl.ds(r0, chunk), :] for p in range(npat)]
        mx = functools.reduce(jnp.maximum, ls)
        w = [jnp.exp2(l - mx) for l in ls]
        num = sum(w[p] * on_ref[p, pl.ds(r0, chunk), :] for p in range(npat))
        o_ref[pl.ds(r0, chunk), :] = (num / sum(w)).astype(o_ref.dtype)
        return carry
    lax.fori_loop(0, seq // chunk, mix, 0, unroll=2)


def _dil_attn(proj, cos, sin, gq, gk, *, batch, seq):
    npat = len(DILATIONS)

    def qcol(h):
        return jnp.where(h < 4, C_QB_LO // HEAD + h, C_QB_HI // HEAD - 4 + h)

    const = lambda b, h: (0, 0)
    return pl.pallas_call(
        functools.partial(_dil_attn_kernel, seq=seq, unroll=seq // QBLK),
        grid=(batch, N_HEADS),
        in_specs=[
            pl.BlockSpec((seq, HEAD), lambda b, h: (b, qcol(h))),
            pl.BlockSpec((seq, HEAD), lambda b, h: (b, C_KB // HEAD + h)),
            pl.BlockSpec((seq, HEAD), lambda b, h: (b, C_VB // HEAD + h)),
            pl.BlockSpec((seq, HEAD), const),
            pl.BlockSpec((seq, HEAD), const),
            pl.BlockSpec((1, HEAD), const),
            pl.BlockSpec((1, HEAD), const),
        ],
        out_specs=pl.BlockSpec((seq, HEAD), lambda b, h: (b, h)),
        out_shape=jax.ShapeDtypeStruct((batch * seq, WIDTH), jnp.bfloat16),
        scratch_shapes=[
            pltpu.VMEM((npat, seq, HEAD), jnp.bfloat16),
            pltpu.VMEM((npat, seq + QBLK, HEAD), jnp.bfloat16),
            pltpu.VMEM((npat, seq + QBLK, 2 * HEAD), jnp.bfloat16),
            pltpu.VMEM((npat, seq, HEAD), jnp.float32),
            pltpu.VMEM((npat, seq, HEAD), jnp.float32),
            pltpu.VMEM((2, QBLK, 2 * QBLK), jnp.float32),
            pltpu.VMEM((2, seq, HEAD), jnp.float32),
        ],
        compiler_params=pltpu.CompilerParams(
            dimension_semantics=("arbitrary", "arbitrary"), vmem_limit_bytes=VMEM_LIMIT),
        name="dil_attn",
    )(proj, proj, proj, cos, sin, gq, gk)


def _out_proj_kernel(oa_ref, ob_ref, g_ref, x_ref, gcol_ref, w32_ref, o_ref, w_ref, *, nsplit):
    @pl.when(pl.program_id(0) == 0)
    def _():
        rows = 256

        def cast(c, carry):
            r0 = pl.multiple_of(c * rows, rows)
            w = w32_ref[pl.ds(r0, rows), :] * gcol_ref[pl.ds(r0, rows), :]
            w_ref[pl.ds(r0, rows), :] = w.astype(w_ref.dtype)
            return carry
        lax.fori_loop(0, w_ref.shape[0] // rows, cast, 0)

    def branch(o, gate):
        o = o.astype(jnp.float32)
        h = gate.astype(jnp.float32) * 0.5
        silu = h + h * jnp.tanh(h)
        return (o * _rms_scale(o, WIDTH) * silu).astype(jnp.bfloat16)

    step = o_ref.shape[0] // nsplit
    for r in range(0, o_ref.shape[0], step):
        rows = slice(r, r + step)
        ya = branch(oa_ref[rows, :], g_ref[rows, :WIDTH])
        yb = branch(ob_ref[rows, :], g_ref[rows, WIDTH:])
        y = jnp.concatenate([ya, yb], axis=-1)
        o_ref[rows, :] = x_ref[rows, :] + jnp.dot(y, w_ref[...], preferred_element_type=jnp.float32)


def _out_proj(o_a, o_b, proj, x2, gcol, w, *, tm=512):
    m = x2.shape[0]
    const = lambda i: (0, 0)
    return pl.pallas_call(
        functools.partial(_out_proj_kernel, nsplit=2),
        grid=(m // tm,),
        in_specs=[
            pl.BlockSpec((tm, WIDTH), lambda i: (i, 0)),
            pl.BlockSpec((tm, WIDTH), lambda i: (i, 0)),
            pl.BlockSpec((tm, 2 * WIDTH), lambda i: (i, C_GATE // (2 * WIDTH))),
            pl.BlockSpec((tm, D_MODEL), lambda i: (i, 0)),
            pl.BlockSpec((2 * WIDTH, 1), const, pipeline_mode=pl.Buffered(1)),
            pl.BlockSpec((2 * WIDTH, D_MODEL), const, pipeline_mode=pl.Buffered(1)),
        ],
        out_specs=pl.BlockSpec((tm, D_MODEL), lambda i: (i, 0)),
        out_shape=jax.ShapeDtypeStruct((m, D_MODEL), jnp.float32),
        scratch_shapes=[pltpu.VMEM((2 * WIDTH, D_MODEL), jnp.bfloat16)],
        compiler_params=pltpu.CompilerParams(
            dimension_semantics=("arbitrary",), vmem_limit_bytes=VMEM_LIMIT),
        name="out_proj",
    )(o_a, o_b, proj, x2, gcol, w)


def _pad_rope_cols(a):
    z = jnp.zeros(a.shape[:-1] + (ROPE // 2,), a.dtype)
    return jnp.concatenate([a[..., :ROPE // 2], z, a[..., ROPE // 2:], z], axis=-1)


def _rope_tables(seq, d, padded):
    inv = ROPE_THETA ** (-np.arange(0, d, 2, dtype=np.float64) / d)
    ang = np.arange(seq, dtype=np.float64)[:, None] * inv[None, :]
    cos, sin = np.cos(ang), np.sin(ang)
    cos2 = np.concatenate([cos, cos], axis=-1)
    sin2 = np.concatenate([-sin, sin], axis=-1)
    if padded:
        z = np.zeros((seq, d // 2))
        cos2 = np.concatenate([cos, z, cos, z], axis=-1)
        sin2 = np.concatenate([-sin, z, sin, z], axis=-1)
    return jnp.asarray(cos2, jnp.float32), jnp.asarray(sin2, jnp.float32)


def kernel(x, norm_gain, w_in, q_a_norm_gain, kv_a_norm_gain, w_uq, w_ukv, mla_q_norm_gain,
           mla_k_norm_gain, dil_q_norm_gain, dil_k_norm_gain, mla_out_norm_gain,
           dil_out_norm_gain, w_out):
    batch, seq, _ = x.shape
    depth = w_in.shape[0]
    bf = jnp.bfloat16
    cos_d, sin_d = _rope_tables(seq, HEAD, padded=False)
    cos_m, sin_m = _rope_tables(seq, ROPE, padded=True)
    h2 = x.reshape(batch * seq, D_MODEL)

    for l in range(depth):
        wq = w_uq[l].reshape(Q_RANK, N_HEADS, MLA_QK)
        wq_r = jnp.concatenate([wq[..., :HEAD], _pad_rope_cols(wq[..., HEAD:])], axis=-1)
        wq_r = wq_r.reshape(Q_RANK, N_HEADS * 2 * HEAD).astype(bf)
        wkv = w_ukv[l].reshape(KV_RANK, N_HEADS, 2 * HEAD)
        wkv_r = jnp.concatenate([wkv[..., :HEAD].reshape(KV_RANK, WIDTH),
                                 wkv[..., HEAD:].reshape(KV_RANK, WIDTH)], axis=-1).astype(bf)

        gq_m = mla_q_norm_gain[l]
        gk_m = mla_k_norm_gain[l]
        gq_full = jnp.concatenate([gq_m[:HEAD], _pad_rope_cols(gq_m[HEAD:])])[None, :]
        gq_full = gq_full * (math.log2(math.e) / math.sqrt(MLA_QK))
        gk_full = jnp.concatenate([gk_m[:HEAD], _pad_rope_cols(gk_m[HEAD:])])[None, :]
        gq_d = dil_q_norm_gain[l][None, :] * (math.log2(math.e) / math.sqrt(HEAD))
        gk_d = dil_k_norm_gain[l][None, :]

        proj = _in_proj(h2, norm_gain[l][None, :], jnp.swapaxes(w_in[l], 0, 1))
        q_f, k_f, v_a = _mla_prep(proj, wq_r, wkv_r, q_a_norm_gain[l][None, :],
                                  kv_a_norm_gain[l][None, :], gq_full, gk_full, cos_m, sin_m, seq=seq)
        o_a = _mla_attn(q_f, k_f, v_a, batch=batch, seq=seq)
        o_b = _dil_attn(proj, cos_d, sin_d, gq_d, gk_d, batch=batch, seq=seq)
        gcol = jnp.concatenate([mla_out_norm_gain[l], dil_out_norm_gain[l]])[:, None]
        h2 = _out_proj(o_a, o_b, proj, h2, gcol, w_out[l])
    return h2.reshape(batch, seq, D_MODEL)
```
